```python
import jax, jax.numpy as jnp
from jax import lax
import numpy as np

D_MODEL = 1024
BATCH = 4
SEQ = 4096
DEPTH = 2

HEAD_DIM = 64
NSA_HEADS = 8
NSA_KV_HEADS = 2
NSA_GROUP = NSA_HEADS // NSA_KV_HEADS
MOBA_HEADS = 8
MIX_WIDTH = (NSA_HEADS + MOBA_HEADS) * HEAD_DIM
ROT_DIM = HEAD_DIM // 4
ROPE_THETA = 500000.0
CMP_LEN = 32
CMP_STRIDE = 16
CMP_HIDDEN = 256
SLC_BLOCK = 64
SLC_TOPN = 16
WINDOW = 512
MOBA_BLOCK = 256
MOBA_TOPK = 3
Q_BLOCK = 128
SLC_CHUNK = 128
MOBA_CHUNK = 32
D_FF = 2816
EPS = 1e-6
NEG = -1e30
FORCE_BONUS = 1e4

NSA_Q_COLS = NSA_HEADS * HEAD_DIM
NSA_KV_COLS = 3 * 2 * NSA_KV_HEADS * HEAD_DIM
NSA_GATE_COLS = NSA_HEADS * 3
MOBA_COLS = 3 * MOBA_HEADS * HEAD_DIM
IN_COLS = NSA_Q_COLS + NSA_KV_COLS + NSA_GATE_COLS + MOBA_COLS

kernel_name = "hymba_nsa_moba_macaron"


def rms_norm(x, g):
    xf = x.astype(jnp.float32)
    y = xf * lax.rsqrt(jnp.mean(xf * xf, axis=-1, keepdims=True) + EPS)
    return (y * g.astype(jnp.float32)).astype(x.dtype)


def swiglu(x, w_gate, w_up, w_down):
    return (jax.nn.silu(x @ w_gate) * (x @ w_up)) @ w_down


def rope_tables(S):
    inv_freq = jnp.power(ROPE_THETA, -(jnp.arange(0, ROT_DIM, 2, dtype=jnp.float32) / ROT_DIM))
    ang = jnp.arange(S, dtype=jnp.float32)[:, None] * inv_freq[None, :]
    return jnp.cos(ang), jnp.sin(ang)


def apply_partial_rope(x, cos, sin):
    half = ROT_DIM // 2
    cos = cos.astype(x.dtype)
    sin = sin.astype(x.dtype)
    x1 = x[..., :half]
    x2 = x[..., half:ROT_DIM]
    return jnp.concatenate([x1 * cos - x2 * sin, x2 * cos + x1 * sin, x[..., ROT_DIM:]], axis=-1)


def masked_softmax(s, valid):
    s = jnp.where(valid, s.astype(jnp.float32), NEG)
    m = jnp.max(s, axis=-1, keepdims=True)
    p = jnp.where(valid, jnp.exp(s - m), 0.0)
    return p / jnp.maximum(jnp.sum(p, axis=-1, keepdims=True), 1.0)


def nsa_mixer(q, k_cmp, v_cmp, k_slc, v_slc, k_win, v_win, gates,
              pos_ck, w_ck1, w_ck2, pos_cv, w_cv1, w_cv2, cos, sin):
    B, G, R, S, Dh = q.shape
    scale = Dh ** -0.5
    t = jnp.arange(S)
    q_rot = apply_partial_rope(q, cos, sin)
    k_slc = apply_partial_rope(k_slc, cos, sin)
    k_win = apply_partial_rope(k_win, cos, sin)

    n_cmp = (S - CMP_LEN) // CMP_STRIDE + 1
    starts = jnp.arange(n_cmp) * CMP_STRIDE
    cidx = starts[:, None] + jnp.arange(CMP_LEN)[None, :]

    def compress(k, pos, w1, w2):
        blocks = k[:, :, cidx, :] + pos
        flat = blocks.reshape(B, G, n_cmp, CMP_LEN * Dh)
        return jax.nn.gelu(flat @ w1) @ w2

    kc = compress(k_cmp, pos_ck, w_ck1, w_ck2)
    vc = compress(v_cmp, pos_cv, w_cv1, w_cv2)
    s_c = jnp.einsum('bgrqd,bgnd->bgrqn', q, kc) * scale
    valid_c = (starts + CMP_LEN - 1)[None, :] <= t[:, None]
    p_c = masked_softmax(s_c, valid_c)
    o_cmp = jnp.einsum('bgrqn,bgnd->bgrqd', p_c.astype(vc.dtype), vc)

    n_slc = S // SLC_BLOCK
    top_n = min(SLC_TOPN, n_slc)
    sb = jnp.arange(n_slc) * SLC_BLOCK
    overlap = ((starts[:, None] < (sb + SLC_BLOCK)[None, :]) &
               ((starts + CMP_LEN)[:, None] > sb[None, :])).astype(jnp.float32)
    imp = jnp.einsum('bgrqn,nj->bgqj', p_c, overlap)
    qblk = t // SLC_BLOCK
    j = jnp.arange(n_slc)
    forced = (j[None, :] == 0) | (j[None, :] == qblk[:, None]) | (j[None, :] == qblk[:, None] - 1)
    visible = sb[None, :] <= t[:, None]
    score = jnp.where(visible, jnp.where(forced, FORCE_BONUS, imp), NEG)
    sel_idx = lax.top_k(score, top_n)[1]

    ks_blocks = k_slc.reshape(B, G, n_slc, SLC_BLOCK, Dh)
    vs_blocks = v_slc.reshape(B, G, n_slc, SLC_BLOCK, Dh)
    bi = jnp.arange(B)[:, None, None, None]
    gi = jnp.arange(G)[None, :, None, None]

    def slc_chunk(c):
        q0 = c * SLC_CHUNK
        qc = lax.dynamic_slice_in_dim(q_rot, q0, SLC_CHUNK, axis=3)
        ic = lax.dynamic_slice_in_dim(sel_idx, q0, SLC_CHUNK, axis=2)
        kg = ks_blocks[bi, gi, ic]
        vg = vs_blocks[bi, gi, ic]
        s = jnp.einsum('bgrqd,bgqnld->bgrqnl', qc, kg) * scale
        kpos = ic[..., None] * SLC_BLOCK + jnp.arange(SLC_BLOCK)
        tq = q0 + jnp.arange(SLC_CHUNK)
        valid = (kpos <= tq[None, None, :, None, None]).reshape(B, G, 1, SLC_CHUNK, top_n * SLC_BLOCK)
        p = masked_softmax(s.reshape(B, G, R, SLC_CHUNK, top_n * SLC_BLOCK), valid)
        p = p.reshape(B, G, R, SLC_CHUNK, top_n, SLC_BLOCK).astype(vg.dtype)
        return jnp.einsum('bgrqnl,bgqnld->bgrqd', p, vg)

    o_slc = lax.map(slc_chunk, jnp.arange(S // SLC_CHUNK))
    o_slc = jnp.moveaxis(o_slc, 0, 3).reshape(B, G, R, S, Dh)

    n_qb = S // Q_BLOCK
    kw_pad = jnp.pad(k_win, ((0, 0), (0, 0), (WINDOW, 0), (0, 0)))
    vw_pad = jnp.pad(v_win, ((0, 0), (0, 0), (WINDOW, 0), (0, 0)))
    widx = jnp.arange(n_qb)[:, None] * Q_BLOCK + jnp.arange(WINDOW + Q_BLOCK)[None, :]
    kw = kw_pad[:, :, widx]
    vw = vw_pad[:, :, widx]
    qw = q_rot.reshape(B, G, R, n_qb, Q_BLOCK, Dh)
    s_w = jnp.einsum('bgrcqd,bgckd->bgrcqk', qw, kw) * scale
    kpos = widx - WINDOW
    tq = jnp.arange(n_qb)[:, None] * Q_BLOCK + jnp.arange(Q_BLOCK)[None, :]
    diff = tq[:, :, None] - kpos[:, None, :]
    valid_w = (diff >= 0) & (diff < WINDOW) & (kpos[:, None, :] >= 0)
    p_w = masked_softmax(s_w, valid_w).astype(vw.dtype)
    o_win = jnp.einsum('bgrcqk,bgckd->bgrcqd', p_w, vw).reshape(B, G, R, S, Dh)

    g = jax.nn.sigmoid(gates)
    return g[..., 0:1] * o_cmp + g[..., 1:2] * o_slc + g[..., 2:3] * o_win


def moba_mixer(q, k, v, cos, sin):
    B, H, S, Dh = q.shape
    scale = Dh ** -0.5
    q = apply_partial_rope(q, cos, sin)
    k = apply_partial_rope(k, cos, sin)
    n_blk = -(-S // MOBA_BLOCK)
    pad = n_blk * MOBA_BLOCK - S
    kb = jnp.pad(k, ((0, 0), (0, 0), (0, pad), (0, 0))).reshape(B, H, n_blk, MOBA_BLOCK, Dh)
    vb = jnp.pad(v, ((0, 0), (0, 0), (0, pad), (0, 0))).reshape(B, H, n_blk, MOBA_BLOCK, Dh)
    k_mean = jnp.mean(kb.astype(jnp.float32), axis=3)
    t = jnp.arange(S)
    qblk = t // MOBA_BLOCK
    past = jnp.arange(n_blk)[None, :] < qblk[:, None]
    gate = jnp.einsum('bhqd,bhnd->bhqn', q.astype(jnp.float32), k_mean)
    gate = jnp.where(past, gate, NEG)
    kk = max(min(MOBA_TOPK, n_blk - 1), 1)
    idx = lax.top_k(gate, kk)[1]
    sel_valid = idx < qblk[None, None, :, None]
    bi = jnp.arange(B)[:, None, None, None]
    hi = jnp.arange(H)[None, :, None, None]

    def chunk(c):
        q0 = c * MOBA_CHUNK
        qc = lax.dynamic_slice_in_dim(q, q0, MOBA_CHUNK, axis=2)
        ic = lax.dynamic_slice_in_dim(idx, q0, MOBA_CHUNK, axis=2)
        vc_ = lax.dynamic_slice_in_dim(sel_valid, q0, MOBA_CHUNK, axis=2)
        kg = kb[bi, hi, ic]
        vg = vb[bi, hi, ic]
        s_past = jnp.einsum('bhqd,bhqnld->bhqnl', qc, kg).reshape(B, H, MOBA_CHUNK, kk * MOBA_BLOCK)
        own = q0 // MOBA_BLOCK
        ko = lax.dynamic_index_in_dim(kb, own, axis=2, keepdims=False)
        vo = lax.dynamic_index_in_dim(vb, own, axis=2, keepdims=False)
        s_own = jnp.einsum('bhqd,bhld->bhql', qc, ko)
        tq = q0 + jnp.arange(MOBA_CHUNK)
        own_valid = (own * MOBA_BLOCK + jnp.arange(MOBA_BLOCK))[None, :] <= tq[:, None]
        valid = jnp.concatenate([
            jnp.broadcast_to(vc_[..., None], (B, H, MOBA_CHUNK, kk, MOBA_BLOCK)).reshape(B, H, MOBA_CHUNK, kk * MOBA_BLOCK),
            jnp.broadcast_to(own_valid, (B, H, MOBA_CHUNK, MOBA_BLOCK))], axis=-1)
        s = jnp.concatenate([s_past, s_own], axis=-1) * scale
        p = masked_softmax(s, valid).astype(vg.dtype)
        p_past = p[..., :kk * MOBA_BLOCK].reshape(B, H, MOBA_CHUNK, kk, MOBA_BLOCK)
        p_own = p[..., kk * MOBA_BLOCK:]
        return (jnp.einsum('bhqnl,bhqnld->bhqd', p_past, vg) +
                jnp.einsum('bhql,bhld->bhqd', p_own, vo))

    o = lax.map(chunk, jnp.arange(S // MOBA_CHUNK))
    return jnp.moveaxis(o, 0, 2).reshape(B, H, S, Dh)


def hybrid_mixer(h, w_in, pos_ck, w_ck1, w_ck2, pos_cv, w_cv1, w_cv2, w_out, cos, sin):
    B, S, _ = h.shape
    G, R, Dh = NSA_KV_HEADS, NSA_GROUP, HEAD_DIM
    proj = h @ w_in
    q_n, kv_n, gate_n, moba = jnp.split(
        proj, [NSA_Q_COLS, NSA_Q_COLS + NSA_KV_COLS, NSA_Q_COLS + NSA_KV_COLS + NSA_GATE_COLS], axis=-1)
    q_n = q_n.reshape(B, S, G, R, Dh).transpose(0, 2, 3, 1, 4)
    kv_n = kv_n.reshape(B, S, 6, G, Dh).transpose(2, 0, 3, 1, 4)
    gate_n = gate_n.reshape(B, S, G, R, 3).transpose(0, 2, 3, 1, 4)
    o_nsa = nsa_mixer(q_n, kv_n[0], kv_n[1], kv_n[2], kv_n[3], kv_n[4], kv_n[5], gate_n,
                      pos_ck, w_ck1, w_ck2, pos_cv, w_cv1, w_cv2, cos, sin)
    o_nsa = o_nsa.transpose(0, 3, 1, 2, 4).reshape(B, S, NSA_HEADS * Dh)
    qkv_m = moba.reshape(B, S, 3, MOBA_HEADS, Dh).transpose(2, 0, 3, 1, 4)
    o_moba = moba_mixer(qkv_m[0], qkv_m[1], qkv_m[2], cos, sin)
    o_moba = o_moba.transpose(0, 2, 1, 3).reshape(B, S, MOBA_HEADS * Dh)
    return jnp.concatenate([o_nsa, o_moba], axis=-1) @ w_out


def setup_inputs(seed: int = 0) -> dict:
    key = jax.random.key(seed)
    ks = jax.random.split(key, 19)

    def w(k, shape, fan_in):
        return jax.random.normal(k, shape, jnp.float32) * fan_in ** -0.5

    def gain(k, shape):
        return 1.0 + 0.02 * jax.random.normal(k, shape, jnp.float32)

    return {
        "x": jax.random.normal(ks[0], (BATCH, SEQ, D_MODEL), jnp.float32),
        "norm_ffn1": gain(ks[1], (DEPTH, D_MODEL)),
        "w_ffn1_gate": w(ks[2], (DEPTH, D_MODEL, D_FF), D_MODEL),
        "w_ffn1_up": w(ks[3], (DEPTH, D_MODEL, D_FF), D_MODEL),
        "w_ffn1_down": w(ks[4], (DEPTH, D_FF, D_MODEL), D_FF),
        "norm_mix": gain(ks[5], (DEPTH, D_MODEL)),
        "w_in": w(ks[6], (DEPTH, D_MODEL, IN_COLS), D_MODEL),
        "pos_ck": 0.1 * jax.random.normal(ks[7], (DEPTH, CMP_LEN, HEAD_DIM), jnp.float32),
        "w_ck1": w(ks[8], (DEPTH, CMP_LEN * HEAD_DIM, CMP_HIDDEN), CMP_LEN * HEAD_DIM),
        "w_ck2": w(ks[9], (DEPTH, CMP_HIDDEN, HEAD_DIM), CMP_HIDDEN),
        "pos_cv": 0.1 * jax.random.normal(ks[10], (DEPTH, CMP_LEN, HEAD_DIM), jnp.float32),
        "w_cv1": w(ks[11], (DEPTH, CMP_LEN * HEAD_DIM, CMP_HIDDEN), CMP_LEN * HEAD_DIM),
        "w_cv2": w(ks[12], (DEPTH, CMP_HIDDEN, HEAD_DIM), CMP_HIDDEN),
        "w_out": w(ks[13], (DEPTH, MIX_WIDTH, D_MODEL), MIX_WIDTH),
        "norm_ffn2": gain(ks[14], (DEPTH, D_MODEL)),
        "w_ffn2_gate": w(ks[15], (DEPTH, D_MODEL, D_FF), D_MODEL),
        "w_ffn2_up": w(ks[16], (DEPTH, D_MODEL, D_FF), D_MODEL),
        "w_ffn2_down": w(ks[17], (DEPTH, D_FF, D_MODEL), D_FF),
        "norm_final": gain(ks[18], (D_MODEL,)),
    }


def reference(x, norm_ffn1, w_ffn1_gate, w_ffn1_up, w_ffn1_down, norm_mix, w_in,
              pos_ck, w_ck1, w_ck2, pos_cv, w_cv1, w_cv2, w_out,
              norm_ffn2, w_ffn2_gate, w_ffn2_up, w_ffn2_down, norm_final):
    cos, sin = rope_tables(x.shape[1])
    for l in range(DEPTH):
        x = x + 0.5 * swiglu(rms_norm(x, norm_ffn1[l]), w_ffn1_gate[l], w_ffn1_up[l], w_ffn1_down[l])
        x = x + hybrid_mixer(rms_norm(x, norm_mix[l]), w_in[l], pos_ck[l], w_ck1[l], w_ck2[l],
                             pos_cv[l], w_cv1[l], w_cv2[l], w_out[l], cos, sin)
        x = x + 0.5 * swiglu(rms_norm(x, norm_ffn2[l]), w_ffn2_gate[l], w_ffn2_up[l], w_ffn2_down[l])
    return rms_norm(x, norm_final)
```

```python
import functools

import numpy as np
import jax
import jax.numpy as jnp
from jax import lax
from jax.experimental import pallas as pl
from jax.experimental.pallas import tpu as pltpu

D_MODEL = 1024
HEAD_DIM = 64
NSA_HEADS = 8
NSA_KV_HEADS = 2
NSA_GROUP = NSA_HEADS // NSA_KV_HEADS
MOBA_HEADS = 8
ROT_DIM = HEAD_DIM // 4
ROPE_THETA = 500000.0
CMP_LEN = 32
CMP_STRIDE = 16
CMP_HIDDEN = 256
SLC_BLOCK = 64
SLC_TOPN = 16
WINDOW = 512
MOBA_BLOCK = 256
MOBA_TOPK = 3
D_FF = 2816
EPS = 1e-6
NEG = -1e30
FORCE_BONUS = 1e4
SCALE = HEAD_DIM ** -0.5

NSA_Q_COLS = NSA_HEADS * HEAD_DIM
NSA_KV_COLS = 3 * 2 * NSA_KV_HEADS * HEAD_DIM
NSA_GATE_COLS = NSA_HEADS * 3
MOBA_COLS = 3 * MOBA_HEADS * HEAD_DIM

LANES = 128
VMEM_LIMIT = 56 * 1024 * 1024

SEC_Q = 0
SEC_GATE = 512
SEC_SW = 768
SEC_CMP = 1280
SEC_MQ = 1536
SEC_MKV = 2048
IN_COLS_P = 3072

F32 = jnp.float32
BF16 = jnp.bfloat16

FFN_ROWS = 512
FFN_CHUNK = 256
PROJ_ROWS = 512
NSA_TQ = 128
NSA_KC = 256
WIN_KC = 128


def _dot(a, b):
    return jnp.dot(a, b, preferred_element_type=F32)


def _dot_nt(a, b):
    return lax.dot_general(a, b, (((1,), (1,)), ((), ())), preferred_element_type=F32)


def _rms(x, g):
    return x * lax.rsqrt(jnp.mean(x * x, axis=-1, keepdims=True) + EPS) * g


def _params(sem):
    return pltpu.CompilerParams(dimension_semantics=sem, vmem_limit_bytes=VMEM_LIMIT)


def _resident(shape):
    nd = len(shape)
    return pl.BlockSpec(shape, lambda *_: (0,) * nd, pipeline_mode=pl.Buffered(1))


def _ffn_kernel(x_ref, g_ref, wg_ref, wu_ref, wd_ref, *rest, final_norm):
    if final_norm:
        gf_ref, o_ref, h_ref = rest
    else:
        o_ref, h_ref = rest
    x = x_ref[...]
    xb = _rms(x, g_ref[...]).astype(BF16)
    for c in range(D_FF // FFN_CHUNK):
        sl = slice(c * FFN_CHUNK, (c + 1) * FFN_CHUNK)
        gate = _dot(xb, wg_ref[:, sl])
        up = _dot(xb, wu_ref[:, sl])
        h_ref[:, sl] = (jax.nn.silu(gate) * up).astype(BF16)
    y = x + 0.5 * _dot(h_ref[...], wd_ref[...])
    if final_norm:
        y = _rms(y, gf_ref[...])
    o_ref[...] = y


def _ffn(x2, g, wg, wu, wd, g_final=None):
    n = x2.shape[0]
    final_norm = g_final is not None
    row = pl.BlockSpec((FFN_ROWS, D_MODEL), lambda i: (i, 0))
    in_specs = [row, _resident((1, D_MODEL)), _resident((D_MODEL, D_FF)), _resident((D_MODEL, D_FF)),
                _resident((D_FF, D_MODEL))]
    args = [x2, g.reshape(1, D_MODEL), wg, wu, wd]
    if final_norm:
        in_specs.append(_resident((1, D_MODEL)))
        args.append(g_final.reshape(1, D_MODEL))
    return pl.pallas_call(
        functools.partial(_ffn_kernel, final_norm=final_norm),
        grid=(n // FFN_ROWS,),
        in_specs=in_specs,
        out_specs=row,
        out_shape=jax.ShapeDtypeStruct((n, D_MODEL), F32),
        scratch_shapes=[pltpu.VMEM((FFN_ROWS, D_FF), BF16)],
        compiler_params=_params(("parallel",)),
        name="ffn_final" if final_norm else "ffn",
    )(*args)


def _rope_tables(c, sl, sr, mode):
    if mode == "both":
        return c, sl, sr
    lane = lax.broadcasted_iota(jnp.int32, c.shape, 1)
    keep = lane < HEAD_DIM if mode == "lo" else lane >= HEAD_DIM
    return jnp.where(keep, c, 1.0), jnp.where(keep, sl, 0.0), jnp.where(keep, sr, 0.0)


def _rope(v, tabs):
    c, sl, sr = tabs
    return v * c + pltpu.roll(v, LANES - ROT_DIM // 2, 1) * sl + pltpu.roll(v, ROT_DIM // 2, 1) * sr


def _inproj_kernel(x_ref, g_ref, w_ref, c_ref, sl_ref, sr_ref,
                   qn_ref, gt_ref, sw_ref, cmp_ref, mq_ref, mkv_ref, km_ref):
    xb = _rms(x_ref[...], g_ref[...]).astype(BF16)
    base = (c_ref[...], sl_ref[...], sr_ref[...])
    tabs = {m: _rope_tables(*base, m) for m in ("both", "lo", "hi")}

    qn_ref[...] = _dot(xb, w_ref[:, SEC_Q:SEC_GATE])
    gt_ref[...] = _dot(xb, w_ref[:, SEC_GATE:SEC_SW])
    cmp_ref[...] = _dot(xb, w_ref[:, SEC_CMP:SEC_MQ])

    sw = _dot(xb, w_ref[:, SEC_SW:SEC_CMP])
    for i in range(4):
        sl = slice(i * LANES, (i + 1) * LANES)
        sw_ref[:, sl] = _rope(sw[:, sl], tabs["lo"]).astype(BF16)

    mq = _dot(xb, w_ref[:, SEC_MQ:SEC_MKV])
    for i in range(4):
        sl = slice(i * LANES, (i + 1) * LANES)
        mq_ref[:, sl] = (_rope(mq[:, sl], tabs["both"]) * SCALE).astype(BF16)

    mkv = _dot(xb, w_ref[:, SEC_MKV:IN_COLS_P])
    nblk = PROJ_ROWS // MOBA_BLOCK
    for h in range(MOBA_HEADS):
        sl = slice(h * LANES, (h + 1) * LANES)
        r = _rope(mkv[:, sl], tabs["lo" if h % 2 == 0 else "hi"])
        mkv_ref[:, sl] = r.astype(BF16)
        km_ref[:, 0, sl] = jnp.mean(r.reshape(nblk, MOBA_BLOCK, LANES), axis=1)


def _inproj(x2, g, w_p, tabs, seq):
    n = x2.shape[0]
    tiles_per_seq = seq // PROJ_ROWS
    nblk = PROJ_ROWS // MOBA_BLOCK

    def rows(width):
        return pl.BlockSpec((PROJ_ROWS, width), lambda i: (i, 0))

    tab = pl.BlockSpec((PROJ_ROWS, LANES), lambda i: (i % tiles_per_seq, 0))
    out_shapes = (
        jax.ShapeDtypeStruct((n, 512), F32),
        jax.ShapeDtypeStruct((n, 256), F32),
        jax.ShapeDtypeStruct((n, 512), BF16),
        jax.ShapeDtypeStruct((n, 256), F32),
        jax.ShapeDtypeStruct((n, 512), BF16),
        jax.ShapeDtypeStruct((n, 1024), BF16),
        jax.ShapeDtypeStruct((n // MOBA_BLOCK, 1, 1024), F32),
    )
    out_specs = (rows(512), rows(256), rows(512), rows(256), rows(512), rows(1024),
                 pl.BlockSpec((nblk, 1, 1024), lambda i: (i, 0, 0)))
    return pl.pallas_call(
        _inproj_kernel,
        grid=(n // PROJ_ROWS,),
        in_specs=[rows(D_MODEL), _resident((1, D_MODEL)), _resident((D_MODEL, IN_COLS_P)), tab, tab, tab],
        out_specs=out_specs,
        out_shape=out_shapes,
        compiler_params=_params(("parallel",)),
        name="inproj",
    )(x2, g.reshape(1, D_MODEL), w_p, *tabs)


def _compress_kernel(h_ref, pos_ref, w1_ref, w2_ref, o_ref):
    half = CMP_STRIDE * HEAD_DIM
    n_pad = h_ref.shape[3]
    out = jnp.zeros((n_pad, LANES), F32)
    for kv in range(2):
        h = h_ref[0, 0, kv]
        first = (h + pos_ref[kv, 0:1, :]).astype(BF16)
        second = (h + pos_ref[kv, 1:2, :]).astype(BF16)
        ya = _dot(first, w1_ref[kv, :half, :])
        zb = _dot(second, w1_ref[kv, half:, :])
        pre = ya + pltpu.roll(zb, n_pad - 1, 0)
        out = out + _dot(jax.nn.gelu(pre).astype(BF16), w2_ref[kv])
    row = lax.broadcasted_iota(jnp.int32, out.shape, 0)
    o_ref[0, 0] = jnp.where(row < n_pad - 1, out, 0.0).astype(BF16)


def _compress(hcmp, pos, w1, w2p):
    b, g = hcmp.shape[:2]
    n_pad = hcmp.shape[3]
    return pl.pallas_call(
        _compress_kernel,
        grid=(b, g),
        in_specs=[pl.BlockSpec((1, 1, 2, n_pad, CMP_STRIDE * HEAD_DIM), lambda i, j: (i, j, 0, 0, 0)),
                  _resident(pos.shape), _resident(w1.shape), _resident(w2p.shape)],
        out_specs=pl.BlockSpec((1, 1, n_pad, LANES), lambda i, j: (i, j, 0, 0)),
        out_shape=jax.ShapeDtypeStruct((b, g, n_pad, LANES), BF16),
        compiler_params=_params(("parallel", "parallel")),
        name="compress",
    )(hcmp, pos, w1, w2p)


def _rank_below(s_t, n_rows, limit):
    idx = lax.broadcasted_iota(jnp.int32, s_t.shape, 0)
    cnt = jnp.zeros(s_t.shape, F32)
    for i in range(n_rows):
        ri = s_t[i:i + 1, :]
        beats = (ri > s_t) | ((ri == s_t) & (idx > i))
        cnt = cnt + jnp.where(beats, 1.0, 0.0)
    return cnt < limit


def _to_rows(sel_t, t):
    pad = jnp.zeros((LANES - sel_t.shape[0], t), F32)
    return jnp.concatenate([sel_t, pad], axis=0).T.astype(BF16)


def _attend(q, kv_ref, c_lo, c_hi, kc, heads, tq, valid_fn, m_sc, l_sc, acc_sc):
    m_sc[...] = jnp.full(m_sc.shape, NEG, F32)
    l_sc[...] = jnp.zeros(l_sc.shape, F32)
    acc_sc[...] = jnp.zeros(acc_sc.shape, F32)

    def body(c, carry):
        k0 = pl.multiple_of(c * kc, kc)
        kv = kv_ref[0, pl.ds(k0, kc), :]
        valid = valid_fn(k0)[None]
        s = jnp.where(valid, _dot_nt(q, kv).reshape(heads, tq, kc), NEG)
        m_old = m_sc[...]
        m_new = jnp.maximum(m_old, jnp.max(s, axis=-1, keepdims=True))
        alpha = jnp.exp(m_old - m_new)
        p = jnp.where(valid, jnp.exp(s - m_new), 0.0)
        l_sc[...] = alpha * l_sc[...] + jnp.sum(p, axis=-1, keepdims=True)
        m_sc[...] = m_new
        pv = _dot(p.reshape(heads * tq, kc).astype(BF16), kv)
        acc_sc[...] = alpha.reshape(heads * tq, 1) * acc_sc[...] + pv
        return carry

    lax.fori_loop(c_lo, c_hi, body, 0)
    den = jnp.maximum(l_sc[...], 1.0).reshape(heads * tq, 1)
    return acc_sc[...] / den


def _nsa_kernel(qn_ref, gt_ref, c_ref, sl_ref, sr_ref, kvc_ref, kvs_ref, kvw_ref, ov_ref,
                o_ref, m_sc, l_sc, acc_sc):
    tq, r_heads = NSA_TQ, NSA_GROUP
    rows = r_heads * tq
    q0 = pl.program_id(2) * tq
    tabs = (c_ref[...], sl_ref[...], sr_ref[...])
    lo = lax.broadcasted_iota(jnp.int32, (tq, LANES), 1) < HEAD_DIM

    def stack(rotate):
        parts = []
        for r in range(r_heads):
            pair = qn_ref[0, :, (r // 2) * LANES:(r // 2 + 1) * LANES]
            if rotate:
                pair = _rope(pair, tabs)
            if r % 2:
                pair = pltpu.roll(pair, HEAD_DIM, 1)
            parts.append(jnp.where(lo, pair * SCALE, 0.0).astype(BF16))
        return jnp.concatenate(parts, axis=0)

    q_plain = stack(False)
    q_rot = stack(True)

    kvc = kvc_ref[0, 0]
    n_pad = kvc.shape[0]
    n_idx = lax.broadcasted_iota(jnp.int32, (tq, n_pad), 1)
    t_idx = q0 + lax.broadcasted_iota(jnp.int32, (tq, n_pad), 0)
    valid_c = (n_idx * CMP_STRIDE + CMP_LEN - 1 <= t_idx)[None]
    s = jnp.where(valid_c, _dot_nt(q_plain, kvc).reshape(r_heads, tq, n_pad), NEG)
    p = jnp.where(valid_c, jnp.exp(s - jnp.max(s, axis=-1, keepdims=True)), 0.0)
    p = p / jnp.maximum(jnp.sum(p, axis=-1, keepdims=True), 1.0)
    o_cmp = _dot(p.reshape(rows, n_pad).astype(BF16), kvc)

    p_sum = p[0] + p[1] + p[2] + p[3]
    ov = ov_ref[...]
    hi = p_sum.astype(BF16)
    r1 = p_sum - hi.astype(F32)
    mid = r1.astype(BF16)
    low = (r1 - mid.astype(F32)).astype(BF16)
    imp = _dot(hi, ov) + _dot(mid, ov) + _dot(low, ov)

    j_idx = lax.broadcasted_iota(jnp.int32, (tq, LANES), 1)
    t_row = q0 + lax.broadcasted_iota(jnp.int32, (tq, LANES), 0)
    blk_shift = SLC_BLOCK.bit_length() - 1
    qblk = jnp.right_shift(t_row, blk_shift)
    forced = (j_idx == 0) | (j_idx == qblk) | (j_idx == qblk - 1)
    visible = j_idx * SLC_BLOCK <= t_row
    score = jnp.where(visible, jnp.where(forced, FORCE_BONUS, imp), NEG)
    n_slc = kvs_ref.shape[1] // SLC_BLOCK
    sel_t = _rank_below(score.T[:n_slc], n_slc, SLC_TOPN).astype(F32)
    sel = _to_rows(sel_t, tq)

    def valid_slc(k0):
        blk = lax.broadcasted_iota(jnp.int32, (LANES, NSA_KC), 0)
        kk = k0 + lax.broadcasted_iota(jnp.int32, (LANES, NSA_KC), 1)
        expand = (blk == jnp.right_shift(kk, blk_shift)).astype(BF16)
        chosen = _dot(sel, expand) > 0.5
        kpos = k0 + lax.broadcasted_iota(jnp.int32, (tq, NSA_KC), 1)
        tpos = q0 + lax.broadcasted_iota(jnp.int32, (tq, NSA_KC), 0)
        return chosen & (kpos <= tpos)

    o_slc = _attend(q_rot, kvs_ref, 0, (q0 + tq + NSA_KC - 1) // NSA_KC, NSA_KC, r_heads, tq,
                    valid_slc, m_sc, l_sc, acc_sc)

    def valid_win(k0):
        kpos = k0 + lax.broadcasted_iota(jnp.int32, (tq, WIN_KC), 1)
        tpos = q0 + lax.broadcasted_iota(jnp.int32, (tq, WIN_KC), 0)
        d = tpos - kpos
        return (d >= 0) & (d < WINDOW)

    o_win = _attend(q_rot, kvw_ref, jnp.maximum(q0 - WINDOW, 0) // WIN_KC, (q0 + tq) // WIN_KC, WIN_KC,
                    r_heads, tq, valid_win, m_sc, l_sc, acc_sc)

    gate = jax.nn.sigmoid(gt_ref[0])
    outs = []
    for r in range(r_heads):
        rs = slice(r * tq, (r + 1) * tq)
        outs.append(gate[:, 3 * r:3 * r + 1] * o_cmp[rs] + gate[:, 3 * r + 1:3 * r + 2] * o_slc[rs]
                    + gate[:, 3 * r + 2:3 * r + 3] * o_win[rs])
    for pr in range(r_heads // 2):
        pair = jnp.where(lo, pltpu.roll(outs[2 * pr], HEAD_DIM, 1), outs[2 * pr + 1])
        o_ref[0, :, pr * LANES:(pr + 1) * LANES] = pair.astype(BF16)


def _nsa(qn, gt, tabs, kvc, sw, ov):
    b, seq = qn.shape[:2]
    g = NSA_KV_HEADS
    tq = NSA_TQ
    rows = NSA_GROUP * tq
    n_pad = kvc.shape[2]
    tab = pl.BlockSpec((tq, LANES), lambda i, j, k: (k, 0))
    return pl.pallas_call(
        _nsa_kernel,
        grid=(b, g, seq // tq),
        in_specs=[
            pl.BlockSpec((1, tq, 2 * LANES), lambda i, j, k: (i, k, j)),
            pl.BlockSpec((1, tq, LANES), lambda i, j, k: (i, k, j)),
            tab, tab, tab,
            pl.BlockSpec((1, 1, n_pad, LANES), lambda i, j, k: (i, j, 0, 0)),
            pl.BlockSpec((1, seq, LANES), lambda i, j, k: (i, 0, 2 * j)),
            pl.BlockSpec((1, seq, LANES), lambda i, j, k: (i, 0, 2 * j + 1)),
            pl.BlockSpec((n_pad, LANES), lambda i, j, k: (0, 0)),
        ],
        out_specs=pl.BlockSpec((1, tq, 2 * LANES), lambda i, j, k: (i, k, j)),
        out_shape=jax.ShapeDtypeStruct((b, seq, NSA_Q_COLS), BF16),
        scratch_shapes=[pltpu.VMEM((NSA_GROUP, tq, 1), F32), pltpu.VMEM((NSA_GROUP, tq, 1), F32),
                        pltpu.VMEM((rows, LANES), F32)],
        compiler_params=_params(("parallel", "parallel", "arbitrary")),
        name="nsa",
    )(qn, gt, *tabs, kvc, sw, sw, ov)


def _moba_kernel(q_ref, kv0_ref, kv1_ref, km0_ref, km1_ref, o_ref, m_sc, l_sc, acc_sc):
    tq = MOBA_BLOCK
    qi = pl.program_id(2)
    q0 = qi * tq
    lo = lax.broadcasted_iota(jnp.int32, (tq, LANES), 1) < HEAD_DIM
    n_blk = km0_ref.shape[1]
    outs = []
    for e, (kv_ref, km_ref) in enumerate(((kv0_ref, km0_ref), (kv1_ref, km1_ref))):
        q = jnp.where(lo if e == 0 else ~lo, q_ref[0], jnp.zeros((), BF16))
        gate_t = _dot_nt(km_ref[0].astype(BF16), q)
        blk = lax.broadcasted_iota(jnp.int32, (n_blk, tq), 0)
        past = blk < qi
        top = _rank_below(jnp.where(past, gate_t, NEG), n_blk, MOBA_TOPK)
        sel = _to_rows((top & past).astype(F32), tq)

        def valid_fn(k0, sel=sel):
            c = k0 // tq
            blk_l = lax.broadcasted_iota(jnp.int32, (LANES, tq), 0)
            chosen = _dot(sel, (blk_l == c).astype(BF16)) > 0.5
            kpos = k0 + lax.broadcasted_iota(jnp.int32, (tq, tq), 1)
            tpos = q0 + lax.broadcasted_iota(jnp.int32, (tq, tq), 0)
            return (kpos <= tpos) & (chosen | (kpos >= q0))

        outs.append(_attend(q, kv_ref, 0, qi + 1, tq, 1, tq, valid_fn, m_sc, l_sc, acc_sc))
    o_ref[0] = jnp.where(lo, outs[1], outs[0]).astype(BF16)


def _moba(mq, mkv, km):
    b, seq = mq.shape[:2]
    tq = MOBA_BLOCK
    n_blk = seq // tq
    return pl.pallas_call(
        _moba_kernel,
        grid=(b, MOBA_HEADS // 2, seq // tq),
        in_specs=[
            pl.BlockSpec((1, tq, LANES), lambda i, j, k: (i, k, j)),
            pl.BlockSpec((1, seq, LANES), lambda i, j, k: (i, 0, 2 * j)),
            pl.BlockSpec((1, seq, LANES), lambda i, j, k: (i, 0, 2 * j + 1)),
            pl.BlockSpec((1, n_blk, LANES), lambda i, j, k: (i, 0, 2 * j)),
            pl.BlockSpec((1, n_blk, LANES), lambda i, j, k: (i, 0, 2 * j + 1)),
        ],
        out_specs=pl.BlockSpec((1, tq, LANES), lambda i, j, k: (i, k, j)),
        out_shape=jax.ShapeDtypeStruct((b, seq, MOBA_HEADS * HEAD_DIM), BF16),
        scratch_shapes=[pltpu.VMEM((1, tq, 1), F32), pltpu.VMEM((1, tq, 1), F32),
                        pltpu.VMEM((tq, LANES), F32)],
        compiler_params=_params(("parallel", "parallel", "arbitrary")),
        name="moba",
    )(mq, mkv, mkv, km, km)


def _outproj_kernel(x_ref, a_ref, m_ref, wa_ref, wm_ref, o_ref):
    o_ref[...] = x_ref[...] + _dot(a_ref[...], wa_ref[...]) + _dot(m_ref[...], wm_ref[...])


def _outproj(x2, o_nsa, o_moba, wa, wm):
    n = x2.shape[0]
    half = NSA_Q_COLS

    def rows(width):
        return pl.BlockSpec((PROJ_ROWS, width), lambda i: (i, 0))

    return pl.pallas_call(
        _outproj_kernel,
        grid=(n // PROJ_ROWS,),
        in_specs=[rows(D_MODEL), rows(half), rows(half), _resident((half, D_MODEL)), _resident((half, D_MODEL))],
        out_specs=rows(D_MODEL),
        out_shape=jax.ShapeDtypeStruct((n, D_MODEL), F32),
        compiler_params=_params(("parallel",)),
        name="outproj",
    )(x2, o_nsa, o_moba, wa, wm)


def _in_perm():
    d = HEAD_DIM
    perm = np.full((IN_COLS_P,), -1, np.int64)
    perm[SEC_Q:SEC_Q + NSA_Q_COLS] = np.arange(NSA_Q_COLS)
    kv0 = NSA_Q_COLS
    gate0 = kv0 + NSA_KV_COLS
    moba0 = gate0 + NSA_GATE_COLS
    per_group = NSA_GROUP * 3
    for g in range(NSA_KV_HEADS):
        perm[SEC_GATE + g * LANES:SEC_GATE + g * LANES + per_group] = gate0 + g * per_group + np.arange(per_group)

    def kv_cols(which, g):
        return kv0 + (which * NSA_KV_HEADS + g) * d + np.arange(d)

    for g in range(NSA_KV_HEADS):
        base = SEC_SW + g * 2 * LANES
        perm[base:base + d] = kv_cols(2, g)
        perm[base + d:base + 2 * d] = kv_cols(3, g)
        perm[base + 2 * d:base + 3 * d] = kv_cols(4, g)
        perm[base + 3 * d:base + 4 * d] = kv_cols(5, g)
        base = SEC_CMP + g * LANES
        perm[base:base + d] = kv_cols(0, g)
        perm[base + d:base + 2 * d] = kv_cols(1, g)
    perm[SEC_MQ:SEC_MQ + MOBA_HEADS * d] = moba0 + np.arange(MOBA_HEADS * d)
    for h in range(MOBA_HEADS):
        k_cols = moba0 + (MOBA_HEADS + h) * d + np.arange(d)
        v_cols = moba0 + (2 * MOBA_HEADS + h) * d + np.arange(d)
        base = SEC_MKV + h * LANES
        first, second = (k_cols, v_cols) if h % 2 == 0 else (v_cols, k_cols)
        perm[base:base + d] = first
        perm[base + d:base + 2 * d] = second
    return perm


def _moba_out_rows():
    d = HEAD_DIM
    rows = []
    for pr in range(MOBA_HEADS // 2):
        rows.append(NSA_Q_COLS + (2 * pr + 1) * d + np.arange(d))
        rows.append(NSA_Q_COLS + (2 * pr) * d + np.arange(d))
    return np.concatenate(rows)


def _overlap_table(seq):
    n_pad = seq // CMP_STRIDE
    n = np.arange(n_pad)[:, None]
    j = np.arange(LANES)[None, :]
    starts = n * CMP_STRIDE
    sb = j * SLC_BLOCK
    ov = (starts < sb + SLC_BLOCK) & (starts + CMP_LEN > sb) & (n < n_pad - 1) & (j < seq // SLC_BLOCK)
    return ov.astype(np.float32)


def _rope_lane_tables(seq):
    half = ROT_DIM // 2
    inv_freq = jnp.power(ROPE_THETA, -(jnp.arange(0, ROT_DIM, 2, dtype=F32) / ROT_DIM))
    ang = jnp.arange(seq, dtype=F32)[:, None] * inv_freq[None, :]
    cos, sin = jnp.cos(ang), jnp.sin(ang)
    rest = HEAD_DIM - ROT_DIM
    zeros_h = jnp.zeros((seq, half), F32)
    c = jnp.concatenate([cos, cos, jnp.ones((seq, rest), F32)], axis=1)
    sl = jnp.concatenate([-sin, zeros_h, jnp.zeros((seq, rest), F32)], axis=1)
    sr = jnp.concatenate([zeros_h, sin, jnp.zeros((seq, rest), F32)], axis=1)
    return tuple(jnp.tile(t, (1, LANES // HEAD_DIM)) for t in (c, sl, sr))


def _mixer(x2, b, seq, g_mix, w_in, pos_ck, w_ck1, w_ck2, pos_cv, w_cv1, w_cv2, w_out, tabs, ov):
    perm = _in_perm()
    w_p = jnp.where(perm[None, :] >= 0, w_in[:, np.maximum(perm, 0)], 0.0).astype(BF16)
    qn, gt, sw, cmp_in, mq, mkv, km = _inproj(x2, g_mix, w_p, tabs, seq)

    n_half = seq // CMP_STRIDE
    hcmp = cmp_in.reshape(b, n_half, CMP_STRIDE, NSA_KV_HEADS, 2, HEAD_DIM)
    hcmp = hcmp.transpose(0, 3, 4, 1, 2, 5).reshape(b, NSA_KV_HEADS, 2, n_half, CMP_STRIDE * HEAD_DIM)
    pos = jnp.stack([pos_ck.reshape(2, -1), pos_cv.reshape(2, -1)])
    w1 = jnp.stack([w_ck1, w_cv1]).astype(BF16)
    zeros = jnp.zeros_like(w_ck2)
    w2p = jnp.stack([jnp.concatenate([w_ck2, zeros], axis=1),
                     jnp.concatenate([zeros, w_cv2], axis=1)]).astype(BF16)
    kvc = _compress(hcmp, pos, w1, w2p)

    o_nsa = _nsa(qn.reshape(b, seq, -1), gt.reshape(b, seq, -1), tabs, kvc, sw.reshape(b, seq, -1), ov)
    o_moba = _moba(mq.reshape(b, seq, -1), mkv.reshape(b, seq, -1), km.reshape(b, seq // MOBA_BLOCK, -1))

    wa = w_out[:NSA_Q_COLS].astype(BF16)
    wm = w_out[_moba_out_rows()].astype(BF16)
    n = b * seq
    return _outproj(x2, o_nsa.reshape(n, -1), o_moba.reshape(n, -1), wa, wm)


def kernel(x, norm_ffn1, w_ffn1_gate, w_ffn1_up, w_ffn1_down, norm_mix, w_in, pos_ck, w_ck1, w_ck2,
           pos_cv, w_cv1, w_cv2, w_out, norm_ffn2, w_ffn2_gate, w_ffn2_up, w_ffn2_down, norm_final):
    b, seq, d = x.shape
    assert d == D_MODEL and seq % PROJ_ROWS == 0 and SLC_TOPN <= seq // SLC_BLOCK <= LANES
    depth = norm_ffn1.shape[0]
    tabs = _rope_lane_tables(seq)
    ov = jnp.asarray(_overlap_table(seq), BF16)
    x2 = x.reshape(b * seq, d)
    for l in range(depth):
        x2 = _ffn(x2, norm_ffn1[l], w_ffn1_gate[l].astype(BF16), w_ffn1_up[l].astype(BF16),
                  w_ffn1_down[l].astype(BF16))
        x2 = _mixer(x2, b, seq, norm_mix[l], w_in[l], pos_ck[l], w_ck1[l], w_ck2[l],
                    pos_cv[l], w_cv1[l], w_cv2[l], w_out[l], tabs, ov)
        x2 = _ffn(x2, norm_ffn2[l], w_ffn2_gate[l].astype(BF16), w_ffn2_up[l].astype(BF16),
                  w_ffn2_down[l].astype(BF16), norm_final if l == depth - 1 else None)
    return x2.reshape(b, seq, d)
```

```python
import functools

import numpy as np
import jax
import jax.numpy as jnp
from jax import lax
from jax.experimental import pallas as pl
from jax.experimental.pallas import tpu as pltpu

D_MODEL = 1024
HEAD_DIM = 64
NSA_HEADS = 8
NSA_KV_HEADS = 2
NSA_GROUP = NSA_HEADS // NSA_KV_HEADS
MOBA_HEADS = 8
ROT_DIM = HEAD_DIM // 4
ROPE_THETA = 500000.0
CMP_LEN = 32
CMP_STRIDE = 16
CMP_HIDDEN = 256
SLC_BLOCK = 64
SLC_TOPN = 16
WINDOW = 512
MOBA_BLOCK = 256
MOBA_TOPK = 3
D_FF = 2816
EPS = 1e-6
NEG = -1e30
FORCE_BONUS = 1e4
SCALE = HEAD_DIM ** -0.5

NSA_Q_COLS = NSA_HEADS * HEAD_DIM
NSA_KV_COLS = 3 * 2 * NSA_KV_HEADS * HEAD_DIM
NSA_GATE_COLS = NSA_HEADS * 3
MOBA_COLS = 3 * MOBA_HEADS * HEAD_DIM

LANES = 128
VMEM_LIMIT = 56 * 1024 * 1024

SEC_Q = 0
SEC_GATE = 512
SEC_SW = 768
SEC_CMP = 1280
SEC_MQ = 1536
SEC_MKV = 2048
IN_COLS_P = 3072

F32 = jnp.float32
BF16 = jnp.bfloat16

FFN_ROWS = 512
FFN_CHUNK = 256
PROJ_ROWS = 512
NSA_TQ = 128
NSA_KC = 512
MOBA_HEADS_PER_STEP = 4
MASK_BIAS = 2.0 ** 100


def _dot(a, b):
    return jnp.dot(a, b, preferred_element_type=F32)


def _dot_nt(a, b):
    return lax.dot_general(a, b, (((1,), (1,)), ((), ())), preferred_element_type=F32)


def _rms(x, g):
    return x * lax.rsqrt(jnp.mean(x * x, axis=-1, keepdims=True) + EPS) * g


def _params(sem):
    return pltpu.CompilerParams(dimension_semantics=sem, vmem_limit_bytes=VMEM_LIMIT)


def _resident(shape):
    nd = len(shape)
    return pl.BlockSpec(shape, lambda *_: (0,) * nd, pipeline_mode=pl.Buffered(1))


def _ffn_kernel(x_ref, g_ref, wg_ref, wu_ref, wd_ref, *rest, final_norm):
    if final_norm:
        gf_ref, o_ref, h_ref = rest
    else:
        o_ref, h_ref = rest
    x = x_ref[...]
    xb = _rms(x, g_ref[...]).astype(BF16)
    for c in range(D_FF // FFN_CHUNK):
        sl = slice(c * FFN_CHUNK, (c + 1) * FFN_CHUNK)
        gate = _dot(xb, wg_ref[:, sl])
        up = _dot(xb, wu_ref[:, sl])
        h_ref[:, sl] = (jax.nn.silu(gate) * up).astype(BF16)
    y = x + 0.5 * _dot(h_ref[...], wd_ref[...])
    if final_norm:
        y = _rms(y, gf_ref[...])
    o_ref[...] = y


def _ffn(x2, g, wg, wu, wd, g_final=None):
    n = x2.shape[0]
    final_norm = g_final is not None
    row = pl.BlockSpec((FFN_ROWS, D_MODEL), lambda i: (i, 0))
    in_specs = [row, _resident((1, D_MODEL)), _resident((D_MODEL, D_FF)), _resident((D_MODEL, D_FF)),
                _resident((D_FF, D_MODEL))]
    args = [x2, g.reshape(1, D_MODEL), wg, wu, wd]
    if final_norm:
        in_specs.append(_resident((1, D_MODEL)))
        args.append(g_final.reshape(1, D_MODEL))
    return pl.pallas_call(
        functools.partial(_ffn_kernel, final_norm=final_norm),
        grid=(n // FFN_ROWS,),
        in_specs=in_specs,
        out_specs=row,
        out_shape=jax.ShapeDtypeStruct((n, D_MODEL), F32),
        scratch_shapes=[pltpu.VMEM((FFN_ROWS, D_FF), BF16)],
        compiler_params=_params(("parallel",)),
        name="ffn_final" if final_norm else "ffn",
    )(*args)


def _rope_tables(c, sl, sr, mode):
    if mode == "both":
        return c, sl, sr
    lane = lax.broadcasted_iota(jnp.int32, c.shape, 1)
    keep = lane < HEAD_DIM if mode == "lo" else lane >= HEAD_DIM
    return jnp.where(keep, c, 1.0), jnp.where(keep, sl, 0.0), jnp.where(keep, sr, 0.0)


def _rope(v, tabs):
    c, sl, sr = tabs
    return v * c + pltpu.roll(v, LANES - ROT_DIM // 2, 1) * sl + pltpu.roll(v, ROT_DIM // 2, 1) * sr


def _inproj_kernel(x_ref, g_ref, w_ref, c_ref, sl_ref, sr_ref,
                   qn_ref, gt_ref, sw_ref, cmp_ref, mq_ref, mkv_ref, km_ref):
    xb = _rms(x_ref[...], g_ref[...]).astype(BF16)
    base = (c_ref[...], sl_ref[...], sr_ref[...])
    tabs = {m: _rope_tables(*base, m) for m in ("both", "lo", "hi")}

    qn_ref[...] = _dot(xb, w_ref[:, SEC_Q:SEC_GATE])
    gt_ref[...] = _dot(xb, w_ref[:, SEC_GATE:SEC_SW])
    cmp_ref[...] = _dot(xb, w_ref[:, SEC_CMP:SEC_MQ])

    sw = _dot(xb, w_ref[:, SEC_SW:SEC_CMP])
    for i in range(4):
        sl = slice(i * LANES, (i + 1) * LANES)
        sw_ref[:, sl] = _rope(sw[:, sl], tabs["lo"]).astype(BF16)

    mq = _dot(xb, w_ref[:, SEC_MQ:SEC_MKV])
    for i in range(4):
        sl = slice(i * LANES, (i + 1) * LANES)
        mq_ref[:, sl] = (_rope(mq[:, sl], tabs["both"]) * SCALE).astype(BF16)

    mkv = _dot(xb, w_ref[:, SEC_MKV:IN_COLS_P])
    nblk = PROJ_ROWS // MOBA_BLOCK
    for h in range(MOBA_HEADS):
        sl = slice(h * LANES, (h + 1) * LANES)
        r = _rope(mkv[:, sl], tabs["lo" if h % 2 == 0 else "hi"])
        mkv_ref[:, sl] = r.astype(BF16)
        km_ref[:, 0, sl] = jnp.mean(r.reshape(nblk, MOBA_BLOCK, LANES), axis=1)


def _inproj(x2, g, w_p, tabs, seq):
    n = x2.shape[0]
    tiles_per_seq = seq // PROJ_ROWS
    nblk = PROJ_ROWS // MOBA_BLOCK

    def rows(width):
        return pl.BlockSpec((PROJ_ROWS, width), lambda i: (i, 0))

    tab = pl.BlockSpec((PROJ_ROWS, LANES), lambda i: (i % tiles_per_seq, 0))
    out_shapes = (
        jax.ShapeDtypeStruct((n, 512), F32),
        jax.ShapeDtypeStruct((n, 256), F32),
        jax.ShapeDtypeStruct((n, 512), BF16),
        jax.ShapeDtypeStruct((n, 256), F32),
        jax.ShapeDtypeStruct((n, 512), BF16),
        jax.ShapeDtypeStruct((n, 1024), BF16),
        jax.ShapeDtypeStruct((n // MOBA_BLOCK, 1, 1024), F32),
    )
    out_specs = (rows(512), rows(256), rows(512), rows(256), rows(512), rows(1024),
                 pl.BlockSpec((nblk, 1, 1024), lambda i: (i, 0, 0)))
    return pl.pallas_call(
        _inproj_kernel,
        grid=(n // PROJ_ROWS,),
        in_specs=[rows(D_MODEL), _resident((1, D_MODEL)), _resident((D_MODEL, IN_COLS_P)), tab, tab, tab],
        out_specs=out_specs,
        out_shape=out_shapes,
        compiler_params=_params(("parallel",)),
        name="inproj",
    )(x2, g.reshape(1, D_MODEL), w_p, *tabs)


def _compress_kernel(h_ref, pos_ref, w1_ref, w2_ref, o_ref):
    half = CMP_STRIDE * HEAD_DIM
    n_pad = h_ref.shape[3]
    out = jnp.zeros((n_pad, LANES), F32)
    for kv in range(2):
        h = h_ref[0, 0, kv]
        first = (h + pos_ref[kv, 0:1, :]).astype(BF16)
        second = (h + pos_ref[kv, 1:2, :]).astype(BF16)
        ya = _dot(first, w1_ref[kv, :half, :])
        zb = _dot(second, w1_ref[kv, half:, :])
        pre = ya + pltpu.roll(zb, n_pad - 1, 0)
        out = out + _dot(jax.nn.gelu(pre).astype(BF16), w2_ref[kv])
    row = lax.broadcasted_iota(jnp.int32, out.shape, 0)
    o_ref[0, 0] = jnp.where(row < n_pad - 1, out, 0.0).astype(BF16)


def _compress(hcmp, pos, w1, w2p):
    b, g = hcmp.shape[:2]
    n_pad = hcmp.shape[3]
    return pl.pallas_call(
        _compress_kernel,
        grid=(b, g),
        in_specs=[pl.BlockSpec((1, 1, 2, n_pad, CMP_STRIDE * HEAD_DIM), lambda i, j: (i, j, 0, 0, 0)),
                  _resident(pos.shape), _resident(w1.shape), _resident(w2p.shape)],
        out_specs=pl.BlockSpec((1, 1, n_pad, LANES), lambda i, j: (i, j, 0, 0)),
        out_shape=jax.ShapeDtypeStruct((b, g, n_pad, LANES), BF16),
        compiler_params=_params(("parallel", "parallel")),
        name="compress",
    )(hcmp, pos, w1, w2p)


def _rank_below(s_t, n_rows, limit):
    idx = lax.broadcasted_iota(jnp.int32, s_t.shape, 0)
    cnt = jnp.zeros(s_t.shape, F32)
    for i in range(n_rows):
        ri = s_t[i:i + 1, :]
        beats = (ri > s_t) | ((ri == s_t) & (idx > i))
        cnt = cnt + jnp.where(beats, 1.0, 0.0)
    return cnt < limit


def _block_bias(sel_t, t):
    pad = jnp.zeros((LANES - sel_t.shape[0], t), F32)
    sel = jnp.concatenate([sel_t, pad], axis=0).T
    return ((sel - 1.0) * MASK_BIAS).astype(BF16)


def _softmax_step(s, kvx, m_ref, acc_ref):
    m_old = m_ref[...]
    m_new = jnp.maximum(m_old, jnp.max(s, axis=-1, keepdims=True))
    alpha = jnp.exp(m_old - m_new)
    p = jnp.exp(s - m_new).astype(BF16)
    acc_ref[...] = alpha * acc_ref[...] + _dot(p, kvx)
    m_ref[...] = m_new


def _softmax_finish(acc):
    den = jnp.sum(acc[:, LANES:], axis=-1, keepdims=True)
    return acc[:, :LANES] / jnp.maximum(den, 1.0)


def _nsa_kernel(qn_ref, gt_ref, c_ref, sl_ref, sr_ref, kvc_ref, kvs_ref, kvw_ref, ov_ref, oh_ref,
                o_ref, m_sc, acc_sc):
    tq, r_heads = NSA_TQ, NSA_GROUP
    rows = r_heads * tq
    q0 = pl.program_id(2) * tq
    tabs = (c_ref[...], sl_ref[...], sr_ref[...])
    lo = lax.broadcasted_iota(jnp.int32, (tq, LANES), 1) < HEAD_DIM

    def stack(rotate):
        parts = []
        for r in range(r_heads):
            pair = qn_ref[0, :, (r // 2) * LANES:(r // 2 + 1) * LANES]
            if rotate:
                pair = _rope(pair, tabs)
            if r % 2:
                pair = pltpu.roll(pair, HEAD_DIM, 1)
            parts.append(jnp.where(lo, pair * SCALE, 0.0).astype(BF16))
        return jnp.concatenate(parts, axis=0)

    q_plain = stack(False)
    q_rot = stack(True)

    kvc = kvc_ref[0, 0]
    n_pad = kvc.shape[0]
    n_idx = lax.broadcasted_iota(jnp.int32, (tq, n_pad), 1)
    t_idx = q0 + lax.broadcasted_iota(jnp.int32, (tq, n_pad), 0)
    valid_c = (n_idx * CMP_STRIDE + CMP_LEN - 1 <= t_idx)[None]
    s = jnp.where(valid_c, _dot_nt(q_plain, kvc).reshape(r_heads, tq, n_pad), NEG)
    p = jnp.where(valid_c, jnp.exp(s - jnp.max(s, axis=-1, keepdims=True)), 0.0)
    p = p / jnp.maximum(jnp.sum(p, axis=-1, keepdims=True), 1.0)
    o_cmp = _dot(p.reshape(rows, n_pad).astype(BF16), kvc)

    p_sum = p[0] + p[1] + p[2] + p[3]
    ov = ov_ref[...]
    hi = p_sum.astype(BF16)
    r1 = p_sum - hi.astype(F32)
    mid = r1.astype(BF16)
    low = (r1 - mid.astype(F32)).astype(BF16)
    imp = _dot(hi, ov) + _dot(mid, ov) + _dot(low, ov)

    j_idx = lax.broadcasted_iota(jnp.int32, (tq, LANES), 1)
    t_row = q0 + lax.broadcasted_iota(jnp.int32, (tq, LANES), 0)
    blk_shift = SLC_BLOCK.bit_length() - 1
    qblk = jnp.right_shift(t_row, blk_shift)
    forced = (j_idx == 0) | (j_idx == qblk) | (j_idx == qblk - 1)
    visible = j_idx * SLC_BLOCK <= t_row
    score = jnp.where(visible, jnp.where(forced, FORCE_BONUS, imp), NEG)
    n_slc = kvs_ref.shape[1] // SLC_BLOCK
    sel_t = _rank_below(score.T[:n_slc], n_slc, SLC_TOPN).astype(F32)
    bias = _block_bias(sel_t, tq)

    q_ext = jnp.concatenate([q_rot, jnp.concatenate([bias] * r_heads, axis=0)], axis=1)
    m_sc[...] = jnp.full(m_sc.shape, NEG, F32)
    acc_sc[...] = jnp.zeros(acc_sc.shape, F32)

    def slc_keys(c):
        ks = pl.ds(pl.multiple_of(c * NSA_KC, NSA_KC), NSA_KC)
        return jnp.concatenate([kvs_ref[0, ks, :], oh_ref[ks, :]], axis=1)

    def slc_body(c, carry):
        kvx = slc_keys(c)
        _softmax_step(_dot_nt(q_ext, kvx), kvx, m_sc, acc_sc)
        return carry

    c_last = q0 // NSA_KC
    lax.fori_loop(0, c_last, slc_body, 0)
    kvx = slc_keys(c_last)
    kpos = c_last * NSA_KC + lax.broadcasted_iota(jnp.int32, (tq, NSA_KC), 1)
    tpos = q0 + lax.broadcasted_iota(jnp.int32, (tq, NSA_KC), 0)
    s = jnp.where((kpos <= tpos)[None], _dot_nt(q_ext, kvx).reshape(r_heads, tq, NSA_KC), -MASK_BIAS)
    _softmax_step(s.reshape(rows, NSA_KC), kvx, m_sc, acc_sc)
    o_slc = _softmax_finish(acc_sc[...])

    w0 = pl.multiple_of(jnp.maximum(q0 - WINDOW, 0), tq)
    kvw = kvw_ref[0, pl.ds(w0, WINDOW + tq), :]
    d = (q0 - w0) + (lax.broadcasted_iota(jnp.int32, (tq, WINDOW + tq), 0)
                     - lax.broadcasted_iota(jnp.int32, (tq, WINDOW + tq), 1))
    valid_w = ((d >= 0) & (d < WINDOW))[None]
    s = jnp.where(valid_w, _dot_nt(q_rot, kvw).reshape(r_heads, tq, WINDOW + tq), NEG)
    p = jnp.exp(s - jnp.max(s, axis=-1, keepdims=True))
    den = jnp.maximum(jnp.sum(p, axis=-1, keepdims=True), 1.0).reshape(rows, 1)
    o_win = _dot(p.reshape(rows, WINDOW + tq).astype(BF16), kvw) / den

    gate = jax.nn.sigmoid(gt_ref[0])
    outs = []
    for r in range(r_heads):
        rs = slice(r * tq, (r + 1) * tq)
        outs.append(gate[:, 3 * r:3 * r + 1] * o_cmp[rs] + gate[:, 3 * r + 1:3 * r + 2] * o_slc[rs]
                    + gate[:, 3 * r + 2:3 * r + 3] * o_win[rs])
    for pr in range(r_heads // 2):
        pair = jnp.where(lo, pltpu.roll(outs[2 * pr], HEAD_DIM, 1), outs[2 * pr + 1])
        o_ref[0, :, pr * LANES:(pr + 1) * LANES] = pair.astype(BF16)


def _nsa(qn, gt, tabs, kvc, sw, ov):
    b, seq = qn.shape[:2]
    g = NSA_KV_HEADS
    tq = NSA_TQ
    rows = NSA_GROUP * tq
    n_pad = kvc.shape[2]
    tab = pl.BlockSpec((tq, LANES), lambda i, j, k: (k, 0))
    return pl.pallas_call(
        _nsa_kernel,
        grid=(b, g, seq // tq),
        in_specs=[
            pl.BlockSpec((1, tq, 2 * LANES), lambda i, j, k: (i, k, j)),
            pl.BlockSpec((1, tq, LANES), lambda i, j, k: (i, k, j)),
            tab, tab, tab,
            pl.BlockSpec((1, 1, n_pad, LANES), lambda i, j, k: (i, j, 0, 0)),
            pl.BlockSpec((1, seq, LANES), lambda i, j, k: (i, 0, 2 * j)),
            pl.BlockSpec((1, seq, LANES), lambda i, j, k: (i, 0, 2 * j + 1)),
            pl.BlockSpec((n_pad, LANES), lambda i, j, k: (0, 0)),
            pl.BlockSpec((seq, LANES), lambda i, j, k: (0, 0)),
        ],
        out_specs=pl.BlockSpec((1, tq, 2 * LANES), lambda i, j, k: (i, k, j)),
        out_shape=jax.ShapeDtypeStruct((b, seq, NSA_Q_COLS), BF16),
        scratch_shapes=[pltpu.VMEM((rows, 1), F32), pltpu.VMEM((rows, 2 * LANES), F32)],
        compiler_params=_params(("parallel", "parallel", "arbitrary")),
        name="nsa",
    )(qn, gt, *tabs, kvc, sw, sw, ov, _block_onehot(seq, SLC_BLOCK))


def _moba_kernel(q_ref, kv_ref, km_ref, oh_ref, o_ref, m_sc, acc_sc):
    tq = MOBA_BLOCK
    heads = MOBA_HEADS_PER_STEP
    qi = pl.program_id(2)
    q0 = pl.multiple_of(qi * tq, tq)
    lo = lax.broadcasted_iota(jnp.int32, (tq, LANES), 1) < HEAD_DIM
    n_blk = km_ref.shape[1]
    blk = lax.broadcasted_iota(jnp.int32, (n_blk, tq), 0)
    past = blk < qi

    def slab(h):
        return slice(h * LANES, (h + 1) * LANES)

    q_own, q_ext = [], []
    for h in range(heads):
        q = jnp.where(lo if h % 2 == 0 else ~lo, q_ref[0, :, slab(h // 2)], jnp.zeros((), BF16))
        gate_t = _dot_nt(km_ref[0, :, slab(h)].astype(BF16), q)
        top = _rank_below(jnp.where(past, gate_t, NEG), n_blk, MOBA_TOPK)
        bias = _block_bias((top & past).astype(F32), tq)
        q_own.append(q)
        q_ext.append(jnp.concatenate([q, bias], axis=1))

    m_sc[...] = jnp.full(m_sc.shape, NEG, F32)
    acc_sc[...] = jnp.zeros(acc_sc.shape, F32)

    own = pl.ds(q0, tq)
    causal = (lax.broadcasted_iota(jnp.int32, (tq, tq), 1) <= lax.broadcasted_iota(jnp.int32, (tq, tq), 0))
    for h in range(heads):
        kv = kv_ref[0, own, slab(h)]
        s = jnp.where(causal, _dot_nt(q_own[h], kv), NEG)
        _softmax_step(s, jnp.concatenate([kv, oh_ref[own, :]], axis=1), m_sc.at[h], acc_sc.at[h])

    def body(c, carry):
        ks = pl.ds(pl.multiple_of(c * (2 * tq), 2 * tq), 2 * tq)
        oh = oh_ref[ks, :]
        for h in range(heads):
            kvx = jnp.concatenate([kv_ref[0, ks, slab(h)], oh], axis=1)
            _softmax_step(_dot_nt(q_ext[h], kvx), kvx, m_sc.at[h], acc_sc.at[h])
        return carry

    lax.fori_loop(0, (qi + 1) // 2, body, 0)

    for pr in range(heads // 2):
        even = _softmax_finish(acc_sc[2 * pr])
        odd = _softmax_finish(acc_sc[2 * pr + 1])
        o_ref[0, :, slab(pr)] = jnp.where(lo, odd, even).astype(BF16)


def _moba(mq, mkv, km):
    b, seq = mq.shape[:2]
    tq = MOBA_BLOCK
    n_blk = seq // tq
    heads = MOBA_HEADS_PER_STEP
    return pl.pallas_call(
        _moba_kernel,
        grid=(b, MOBA_HEADS // heads, seq // tq),
        in_specs=[
            pl.BlockSpec((1, tq, heads // 2 * LANES), lambda i, j, k: (i, k, j)),
            pl.BlockSpec((1, seq, heads * LANES), lambda i, j, k: (i, 0, j)),
            pl.BlockSpec((1, n_blk, heads * LANES), lambda i, j, k: (i, 0, j)),
            pl.BlockSpec((seq, LANES), lambda i, j, k: (0, 0)),
        ],
        out_specs=pl.BlockSpec((1, tq, heads // 2 * LANES), lambda i, j, k: (i, k, j)),
        out_shape=jax.ShapeDtypeStruct((b, seq, MOBA_HEADS * HEAD_DIM), BF16),
        scratch_shapes=[pltpu.VMEM((heads, tq, 1), F32), pltpu.VMEM((heads, tq, 2 * LANES), F32)],
        compiler_params=_params(("parallel", "parallel", "arbitrary")),
        name="moba",
    )(mq, mkv, km, _block_onehot(seq, MOBA_BLOCK))


def _outproj_kernel(x_ref, a_ref, m_ref, wa_ref, wm_ref, o_ref):
    o_ref[...] = x_ref[...] + _dot(a_ref[...], wa_ref[...]) + _dot(m_ref[...], wm_ref[...])


def _outproj(x2, o_nsa, o_moba, wa, wm):
    n = x2.shape[0]
    half = NSA_Q_COLS

    def rows(width):
        return pl.BlockSpec((PROJ_ROWS, width), lambda i: (i, 0))

    return pl.pallas_call(
        _outproj_kernel,
        grid=(n // PROJ_ROWS,),
        in_specs=[rows(D_MODEL), rows(half), rows(half), _resident((half, D_MODEL)), _resident((half, D_MODEL))],
        out_specs=rows(D_MODEL),
        out_shape=jax.ShapeDtypeStruct((n, D_MODEL), F32),
        compiler_params=_params(("parallel",)),
        name="outproj",
    )(x2, o_nsa, o_moba, wa, wm)


def _in_perm():
    d = HEAD_DIM
    perm = np.full((IN_COLS_P,), -1, np.int64)
    perm[SEC_Q:SEC_Q + NSA_Q_COLS] = np.arange(NSA_Q_COLS)
    kv0 = NSA_Q_COLS
    gate0 = kv0 + NSA_KV_COLS
    moba0 = gate0 + NSA_GATE_COLS
    per_group = NSA_GROUP * 3
    for g in range(NSA_KV_HEADS):
        perm[SEC_GATE + g * LANES:SEC_GATE + g * LANES + per_group] = gate0 + g * per_group + np.arange(per_group)

    def kv_cols(which, g):
        return kv0 + (which * NSA_KV_HEADS + g) * d + np.arange(d)

    for g in range(NSA_KV_HEADS):
        base = SEC_SW + g * 2 * LANES
        perm[base:base + d] = kv_cols(2, g)
        perm[base + d:base + 2 * d] = kv_cols(3, g)
        perm[base + 2 * d:base + 3 * d] = kv_cols(4, g)
        perm[base + 3 * d:base + 4 * d] = kv_cols(5, g)
        base = SEC_CMP + g * LANES
        perm[base:base + d] = kv_cols(0, g)
        perm[base + d:base + 2 * d] = kv_cols(1, g)
    perm[SEC_MQ:SEC_MQ + MOBA_HEADS * d] = moba0 + np.arange(MOBA_HEADS * d)
    for h in range(MOBA_HEADS):
        k_cols = moba0 + (MOBA_HEADS + h) * d + np.arange(d)
        v_cols = moba0 + (2 * MOBA_HEADS + h) * d + np.arange(d)
        base = SEC_MKV + h * LANES
        first, second = (k_cols, v_cols) if h % 2 == 0 else (v_cols, k_cols)
        perm[base:base + d] = first
        perm[base + d:base + 2 * d] = second
    return perm


def _moba_out_rows():
    d = HEAD_DIM
    rows = []
    for pr in range(MOBA_HEADS // 2):
        rows.append(NSA_Q_COLS + (2 * pr + 1) * d + np.arange(d))
        rows.append(NSA_Q_COLS + (2 * pr) * d + np.arange(d))
    return np.concatenate(rows)


def _overlap_table(seq):
    n_pad = seq // CMP_STRIDE
    n = np.arange(n_pad)[:, None]
    j = np.arange(LANES)[None, :]
    starts = n * CMP_STRIDE
    sb = j * SLC_BLOCK
    ov = (starts < sb + SLC_BLOCK) & (starts + CMP_LEN > sb) & (n < n_pad - 1) & (j < seq // SLC_BLOCK)
    return ov.astype(np.float32)


def _block_onehot(seq, block):
    onehot = np.arange(seq)[:, None] // block == np.arange(LANES)[None, :]
    return jnp.asarray(onehot.astype(np.float32), BF16)


def _rope_lane_tables(seq):
    half = ROT_DIM // 2
    inv_freq = jnp.power(ROPE_THETA, -(jnp.arange(0, ROT_DIM, 2, dtype=F32) / ROT_DIM))
    ang = jnp.arange(seq, dtype=F32)[:, None] * inv_freq[None, :]
    cos, sin = jnp.cos(ang), jnp.sin(ang)
    rest = HEAD_DIM - ROT_DIM
    zeros_h = jnp.zeros((seq, half), F32)
    c = jnp.concatenate([cos, cos, jnp.ones((seq, rest), F32)], axis=1)
    sl = jnp.concatenate([-sin, zeros_h, jnp.zeros((seq, rest), F32)], axis=1)
    sr = jnp.concatenate([zeros_h, sin, jnp.zeros((seq, rest), F32)], axis=1)
    return tuple(jnp.tile(t, (1, LANES // HEAD_DIM)) for t in (c, sl, sr))


def _mixer(x2, b, seq, g_mix, w_in, pos_ck, w_ck1, w_ck2, pos_cv, w_cv1, w_cv2, w_out, tabs, ov):
    perm = _in_perm()
    w_p = jnp.where(perm[None, :] >= 0, w_in[:, np.maximum(perm, 0)], 0.0).astype(BF16)
    qn, gt, sw, cmp_in, mq, mkv, km = _inproj(x2, g_mix, w_p, tabs, seq)

    n_half = seq // CMP_STRIDE
    hcmp = cmp_in.reshape(b, n_half, CMP_STRIDE, NSA_KV_HEADS, 2, HEAD_DIM)
    hcmp = hcmp.transpose(0, 3, 4, 1, 2, 5).reshape(b, NSA_KV_HEADS, 2, n_half, CMP_STRIDE * HEAD_DIM)
    pos = jnp.stack([pos_ck.reshape(2, -1), pos_cv.reshape(2, -1)])
    w1 = jnp.stack([w_ck1, w_cv1]).astype(BF16)
    zeros = jnp.zeros_like(w_ck2)
    w2p = jnp.stack([jnp.concatenate([w_ck2, zeros], axis=1),
                     jnp.concatenate([zeros, w_cv2], axis=1)]).astype(BF16)
    kvc = _compress(hcmp, pos, w1, w2p)

    o_nsa = _nsa(qn.reshape(b, seq, -1), gt.reshape(b, seq, -1), tabs, kvc, sw.reshape(b, seq, -1), ov)
    o_moba = _moba(mq.reshape(b, seq, -1), mkv.reshape(b, seq, -1), km.reshape(b, seq // MOBA_BLOCK, -1))

    wa = w_out[:NSA_Q_COLS].astype(BF16)
    wm = w_out[_moba_out_rows()].astype(BF16)
    n = b * seq
    return _outproj(x2, o_nsa.reshape(n, -1), o_moba.reshape(n, -1), wa, wm)


def kernel(x, norm_ffn1, w_ffn1_gate, w_ffn1_up, w_ffn1_down, norm_mix, w_in, pos_ck, w_ck1, w_ck2,
           pos_cv, w_cv1, w_cv2, w_out, norm_ffn2, w_ffn2_gate, w_ffn2_up, w_ffn2_down, norm_final):
    b, seq, d = x.shape
    assert d == D_MODEL and seq % PROJ_ROWS == 0 and SLC_TOPN <= seq // SLC_BLOCK <= LANES
    depth = norm_ffn1.shape[0]
    tabs = _rope_lane_tables(seq)
    ov = jnp.asarray(_overlap_table(seq), BF16)
    x2 = x.reshape(b * seq, d)
    for l in range(depth):
        x2 = _ffn(x2, norm_ffn1[l], w_ffn1_gate[l].astype(BF16), w_ffn1_up[l].astype(BF16),
                  w_ffn1_down[l].astype(BF16))
        x2 = _mixer(x2, b, seq, norm_mix[l], w_in[l], pos_ck[l], w_ck1[l], w_ck2[l],
                    pos_cv[l], w_cv1[l], w_cv2[l], w_out[l], tabs, ov)
        x2 = _ffn(x2, norm_ffn2[l], w_ffn2_gate[l].astype(BF16), w_ffn2_up[l].astype(BF16),
                  w_ffn2_down[l].astype(BF16), norm_final if l == depth - 1 else None)
    return x2.reshape(b, seq, d)
```

```python
import functools

import numpy as np
import jax
import jax.numpy as jnp
from jax import lax
from jax.experimental import pallas as pl
from jax.experimental.pallas import tpu as pltpu

D_MODEL = 1024
HEAD_DIM = 64
NSA_HEADS = 8
NSA_KV_HEADS = 2
NSA_GROUP = NSA_HEADS // NSA_KV_HEADS
MOBA_HEADS = 8
ROT_DIM = HEAD_DIM // 4
ROPE_THETA = 500000.0
CMP_LEN = 32
CMP_STRIDE = 16
CMP_HIDDEN = 256
SLC_BLOCK = 64
SLC_TOPN = 16
WINDOW = 512
MOBA_BLOCK = 256
MOBA_TOPK = 3
D_FF = 2816
EPS = 1e-6
NEG = -1e30
FORCE_BONUS = 1e4
SCALE = HEAD_DIM ** -0.5

NSA_Q_COLS = NSA_HEADS * HEAD_DIM
NSA_KV_COLS = 3 * 2 * NSA_KV_HEADS * HEAD_DIM
NSA_GATE_COLS = NSA_HEADS * 3
MOBA_COLS = 3 * MOBA_HEADS * HEAD_DIM

LANES = 128
VMEM_LIMIT = 56 * 1024 * 1024

SEC_Q = 0
SEC_GATE = 512
SEC_SW = 768
SEC_CMP = 1280
SEC_MQ = 1536
SEC_MKV = 2048
IN_COLS_P = 3072

F32 = jnp.float32
BF16 = jnp.bfloat16

FFN_ROWS = 512
FFN_CHUNK = 256
PROJ_ROWS = 512
NSA_TQ = 128
NSA_KC = 512
SUBLANES = 8
MOBA_HEADS_PER_STEP = 4
MASK_BIAS = 2.0 ** 100


def _dot(a, b):
    return jnp.dot(a, b, preferred_element_type=F32)


def _dot_nt(a, b):
    return lax.dot_general(a, b, (((1,), (1,)), ((), ())), preferred_element_type=F32)


def _rms(x, g):
    return x * lax.rsqrt(jnp.mean(x * x, axis=-1, keepdims=True) + EPS) * g


def _params(sem):
    return pltpu.CompilerParams(dimension_semantics=sem, vmem_limit_bytes=VMEM_LIMIT)


def _resident(shape):
    nd = len(shape)
    return pl.BlockSpec(shape, lambda *_: (0,) * nd, pipeline_mode=pl.Buffered(1))


def _ffn_kernel(x_ref, g_ref, wg_ref, wu_ref, wd_ref, *rest, final_norm):
    if final_norm:
        gf_ref, o_ref, h_ref = rest
    else:
        o_ref, h_ref = rest
    x = x_ref[...]
    xb = _rms(x, g_ref[...]).astype(BF16)
    for c in range(D_FF // FFN_CHUNK):
        sl = slice(c * FFN_CHUNK, (c + 1) * FFN_CHUNK)
        gate = _dot(xb, wg_ref[:, sl])
        up = _dot(xb, wu_ref[:, sl])
        h_ref[:, sl] = (jax.nn.silu(gate) * up).astype(BF16)
    y = x + 0.5 * _dot(h_ref[...], wd_ref[...])
    if final_norm:
        y = _rms(y, gf_ref[...])
    o_ref[...] = y


def _ffn(x2, g, wg, wu, wd, g_final=None):
    n = x2.shape[0]
    final_norm = g_final is not None
    row = pl.BlockSpec((FFN_ROWS, D_MODEL), lambda i: (i, 0))
    in_specs = [row, _resident((1, D_MODEL)), _resident((D_MODEL, D_FF)), _resident((D_MODEL, D_FF)),
                _resident((D_FF, D_MODEL))]
    args = [x2, g.reshape(1, D_MODEL), wg, wu, wd]
    if final_norm:
        in_specs.append(_resident((1, D_MODEL)))
        args.append(g_final.reshape(1, D_MODEL))
    return pl.pallas_call(
        functools.partial(_ffn_kernel, final_norm=final_norm),
        grid=(n // FFN_ROWS,),
        in_specs=in_specs,
        out_specs=row,
        out_shape=jax.ShapeDtypeStruct((n, D_MODEL), F32),
        scratch_shapes=[pltpu.VMEM((FFN_ROWS, D_FF), BF16)],
        compiler_params=_params(("parallel",)),
        name="ffn_final" if final_norm else "ffn",
    )(*args)


def _rope_tables(c, sl, sr, mode):
    if mode == "both":
        return c, sl, sr
    lane = lax.broadcasted_iota(jnp.int32, c.shape, 1)
    keep = lane < HEAD_DIM if mode == "lo" else lane >= HEAD_DIM
    return jnp.where(keep, c, 1.0), jnp.where(keep, sl, 0.0), jnp.where(keep, sr, 0.0)


def _rope(v, tabs):
    c, sl, sr = tabs
    return v * c + pltpu.roll(v, LANES - ROT_DIM // 2, 1) * sl + pltpu.roll(v, ROT_DIM // 2, 1) * sr


def _inproj_kernel(x_ref, g_ref, w_ref, c_ref, sl_ref, sr_ref,
                   qn_ref, gt_ref, sw_ref, cmp_ref, mq_ref, mkv_ref, km_ref):
    xb = _rms(x_ref[...], g_ref[...]).astype(BF16)
    base = (c_ref[...], sl_ref[...], sr_ref[...])
    tabs = {m: _rope_tables(*base, m) for m in ("both", "lo", "hi")}

    qn_ref[...] = _dot(xb, w_ref[:, SEC_Q:SEC_GATE])
    gt_ref[...] = _dot(xb, w_ref[:, SEC_GATE:SEC_SW])
    cmp_ref[...] = _dot(xb, w_ref[:, SEC_CMP:SEC_MQ])

    sw = _dot(xb, w_ref[:, SEC_SW:SEC_CMP])
    for i in range(4):
        sl = slice(i * LANES, (i + 1) * LANES)
        sw_ref[:, sl] = _rope(sw[:, sl], tabs["lo"]).astype(BF16)

    mq = _dot(xb, w_ref[:, SEC_MQ:SEC_MKV])
    for i in range(4):
        sl = slice(i * LANES, (i + 1) * LANES)
        mq_ref[:, sl] = (_rope(mq[:, sl], tabs["both"]) * SCALE).astype(BF16)

    mkv = _dot(xb, w_ref[:, SEC_MKV:IN_COLS_P])
    nblk = PROJ_ROWS // MOBA_BLOCK
    for h in range(MOBA_HEADS):
        sl = slice(h * LANES, (h + 1) * LANES)
        r = _rope(mkv[:, sl], tabs["lo" if h % 2 == 0 else "hi"])
        mkv_ref[:, sl] = r.astype(BF16)
        km_ref[:, 0, sl] = jnp.mean(r.reshape(nblk, MOBA_BLOCK, LANES), axis=1)


def _inproj(x2, g, w_p, tabs, seq):
    n = x2.shape[0]
    tiles_per_seq = seq // PROJ_ROWS
    nblk = PROJ_ROWS // MOBA_BLOCK

    def rows(width):
        return pl.BlockSpec((PROJ_ROWS, width), lambda i: (i, 0))

    tab = pl.BlockSpec((PROJ_ROWS, LANES), lambda i: (i % tiles_per_seq, 0))
    out_shapes = (
        jax.ShapeDtypeStruct((n, 512), F32),
        jax.ShapeDtypeStruct((n, 256), F32),
        jax.ShapeDtypeStruct((n, 512), BF16),
        jax.ShapeDtypeStruct((n, 256), F32),
        jax.ShapeDtypeStruct((n, 512), BF16),
        jax.ShapeDtypeStruct((n, 1024), BF16),
        jax.ShapeDtypeStruct((n // MOBA_BLOCK, 1, 1024), F32),
    )
    out_specs = (rows(512), rows(256), rows(512), rows(256), rows(512), rows(1024),
                 pl.BlockSpec((nblk, 1, 1024), lambda i: (i, 0, 0)))
    return pl.pallas_call(
        _inproj_kernel,
        grid=(n // PROJ_ROWS,),
        in_specs=[rows(D_MODEL), _resident((1, D_MODEL)), _resident((D_MODEL, IN_COLS_P)), tab, tab, tab],
        out_specs=out_specs,
        out_shape=out_shapes,
        compiler_params=_params(("parallel",)),
        name="inproj",
    )(x2, g.reshape(1, D_MODEL), w_p, *tabs)


def _compress_kernel(h_ref, pos_ref, w1_ref, w2_ref, o_ref):
    half = CMP_STRIDE * HEAD_DIM
    n_pad = h_ref.shape[3]
    out = jnp.zeros((n_pad, LANES), F32)
    for kv in range(2):
        h = h_ref[0, 0, kv]
        first = (h + pos_ref[kv, 0:1, :]).astype(BF16)
        second = (h + pos_ref[kv, 1:2, :]).astype(BF16)
        ya = _dot(first, w1_ref[kv, :half, :])
        zb = _dot(second, w1_ref[kv, half:, :])
        pre = ya + pltpu.roll(zb, n_pad - 1, 0)
        out = out + _dot(jax.nn.gelu(pre).astype(BF16), w2_ref[kv])
    row = lax.broadcasted_iota(jnp.int32, out.shape, 0)
    o_ref[0, 0] = jnp.where(row < n_pad - 1, out, 0.0).astype(BF16)


def _compress(hcmp, pos, w1, w2p):
    b, g = hcmp.shape[:2]
    n_pad = hcmp.shape[3]
    return pl.pallas_call(
        _compress_kernel,
        grid=(b, g),
        in_specs=[pl.BlockSpec((1, 1, 2, n_pad, CMP_STRIDE * HEAD_DIM), lambda i, j: (i, j, 0, 0, 0)),
                  _resident(pos.shape), _resident(w1.shape), _resident(w2p.shape)],
        out_specs=pl.BlockSpec((1, 1, n_pad, LANES), lambda i, j: (i, j, 0, 0)),
        out_shape=jax.ShapeDtypeStruct((b, g, n_pad, LANES), BF16),
        compiler_params=_params(("parallel", "parallel")),
        name="compress",
    )(hcmp, pos, w1, w2p)


def _rank_below(s_t, n_rows, limit):
    t = s_t.shape[1]
    n_grp = n_rows // SUBLANES
    grp = [s_t[g * SUBLANES:(g + 1) * SUBLANES] for g in range(n_grp)]
    cnt = [jnp.zeros((SUBLANES, t), F32) for _ in range(n_grp)]
    sub = lax.broadcasted_iota(jnp.int32, (SUBLANES, t), 0)
    for i in range(n_rows):
        ri = s_t[i:i + 1, :]
        for g in range(n_grp):
            if g > i // SUBLANES:
                beats = ri >= grp[g]
            elif g < i // SUBLANES:
                beats = ri > grp[g]
            else:
                beats = (ri > grp[g]) | ((ri == grp[g]) & (sub > i % SUBLANES))
            cnt[g] = cnt[g] + jnp.where(beats, 1.0, 0.0)
    return jnp.concatenate(cnt, axis=0) < limit


def _block_bias(sel_t, t):
    pad = jnp.zeros((LANES - sel_t.shape[0], t), F32)
    sel = jnp.concatenate([sel_t, pad], axis=0).T
    return ((sel - 1.0) * MASK_BIAS).astype(BF16)


def _with_ones(kv):
    return jnp.concatenate([kv, jnp.ones(kv.shape, kv.dtype)], axis=1)


def _softmax_step(s, kv, m_ref, acc_ref):
    m_old = m_ref[...]
    m_new = jnp.maximum(m_old, jnp.max(s, axis=-1, keepdims=True))
    alpha = jnp.exp(m_old - m_new)
    p = jnp.exp(s - m_new).astype(BF16)
    acc_ref[...] = alpha * acc_ref[...] + _dot(p, _with_ones(kv))
    m_ref[...] = m_new


def _attend(q_ext, keys, n_past, causal, m_ref, acc_ref, s_a, s_b):
    def put_scores(c, s_ref):
        kvs, onehot = keys(c)
        for h, (q, kv) in enumerate(zip(q_ext, kvs)):
            s_ref[h] = _dot_nt(q, jnp.concatenate([kv, onehot], axis=1))

    def softmax(c, s_ref, mask=None):
        for h, kv in enumerate(keys(c)[0]):
            s = s_ref[h] if mask is None else jnp.where(mask, s_ref[h], -MASK_BIAS)
            _softmax_step(s, kv, m_ref.at[h], acc_ref.at[h])

    put_scores(0, s_a)

    def body(i, carry):
        c = 2 * i
        put_scores(c + 1, s_b)
        softmax(c, s_a)
        put_scores(c + 2, s_a)
        softmax(c + 1, s_b)
        return carry

    lax.fori_loop(0, n_past // 2, body, 0)

    @pl.when(n_past % 2 == 1)
    def _():
        put_scores(n_past, s_b)
        softmax(n_past - 1, s_a)
        softmax(n_past, s_b, causal)

    @pl.when(n_past % 2 == 0)
    def _():
        softmax(n_past, s_a, causal)


def _softmax_finish(acc):
    return acc[:, :LANES] / jnp.maximum(acc[:, LANES:], 1.0)


def _nsa_kernel(qn_ref, gt_ref, c_ref, sl_ref, sr_ref, kvc_ref, kvs_ref, kvw_ref, ov_ref, oh_ref,
                o_ref, m_sc, acc_sc, sa_sc, sb_sc):
    tq, r_heads = NSA_TQ, NSA_GROUP
    rows = r_heads * tq
    q0 = pl.program_id(2) * tq
    tabs = (c_ref[...], sl_ref[...], sr_ref[...])
    lo = lax.broadcasted_iota(jnp.int32, (tq, LANES), 1) < HEAD_DIM

    def stack(rotate):
        parts = []
        for r in range(r_heads):
            pair = qn_ref[0, :, (r // 2) * LANES:(r // 2 + 1) * LANES]
            if rotate:
                pair = _rope(pair, tabs)
            if r % 2:
                pair = pltpu.roll(pair, HEAD_DIM, 1)
            parts.append(jnp.where(lo, pair * SCALE, 0.0).astype(BF16))
        return jnp.concatenate(parts, axis=0)

    q_plain = stack(False)
    q_rot = stack(True)

    w0 = pl.multiple_of(jnp.maximum(q0 - WINDOW, 0), tq)
    kvw = kvw_ref[0, pl.ds(w0, WINDOW + tq), :]
    d = (q0 - w0) + (lax.broadcasted_iota(jnp.int32, (tq, WINDOW + tq), 0)
                     - lax.broadcasted_iota(jnp.int32, (tq, WINDOW + tq), 1))
    valid_w = ((d >= 0) & (d < WINDOW))[None]
    s = jnp.where(valid_w, _dot_nt(q_rot, kvw).reshape(r_heads, tq, WINDOW + tq), NEG)
    p = jnp.exp(s - jnp.max(s, axis=-1, keepdims=True)).reshape(rows, WINDOW + tq).astype(BF16)
    o_win = _softmax_finish(_dot(p, _with_ones(kvw)))

    kvc = kvc_ref[0, 0]
    n_pad = kvc.shape[0]
    n_idx = lax.broadcasted_iota(jnp.int32, (tq, n_pad), 1)
    t_idx = q0 + lax.broadcasted_iota(jnp.int32, (tq, n_pad), 0)
    valid_c = (n_idx * CMP_STRIDE + CMP_LEN - 1 <= t_idx)[None]
    s = jnp.where(valid_c, _dot_nt(q_plain, kvc).reshape(r_heads, tq, n_pad), NEG)
    p = jnp.where(valid_c, jnp.exp(s - jnp.max(s, axis=-1, keepdims=True)), 0.0)
    p = p / jnp.maximum(jnp.sum(p, axis=-1, keepdims=True), 1.0)
    o_cmp = _dot(p.reshape(rows, n_pad).astype(BF16), kvc)

    p_sum = p[0] + p[1] + p[2] + p[3]
    ov = ov_ref[...]
    hi = p_sum.astype(BF16)
    r1 = p_sum - hi.astype(F32)
    mid = r1.astype(BF16)
    low = (r1 - mid.astype(F32)).astype(BF16)
    imp = _dot(hi, ov) + _dot(mid, ov) + _dot(low, ov)

    j_idx = lax.broadcasted_iota(jnp.int32, (tq, LANES), 1)
    t_row = q0 + lax.broadcasted_iota(jnp.int32, (tq, LANES), 0)
    blk_shift = SLC_BLOCK.bit_length() - 1
    qblk = jnp.right_shift(t_row, blk_shift)
    forced = (j_idx == 0) | (j_idx == qblk) | (j_idx == qblk - 1)
    visible = j_idx * SLC_BLOCK <= t_row
    score = jnp.where(visible, jnp.where(forced, FORCE_BONUS, imp), NEG)
    n_slc = kvs_ref.shape[1] // SLC_BLOCK
    sel_t = _rank_below(score.T[:n_slc], n_slc, SLC_TOPN).astype(F32)
    bias = _block_bias(sel_t, tq)

    gate = jax.nn.sigmoid(gt_ref[0])

    def gate_of(r, branch):
        return jnp.broadcast_to(gate[:, 3 * r + branch:3 * r + branch + 1], (tq, LANES))

    def head(o, r):
        return o[r * tq:(r + 1) * tq]

    partial = [gate_of(r, 0) * head(o_cmp, r) + gate_of(r, 2) * head(o_win, r) for r in range(r_heads)]
    gate_slc = [gate_of(r, 1) for r in range(r_heads)]

    q_ext = jnp.concatenate([q_rot, jnp.concatenate([bias] * r_heads, axis=0)], axis=1)
    m_sc[...] = jnp.full(m_sc.shape, NEG, F32)
    acc_sc[...] = jnp.zeros(acc_sc.shape, F32)

    def slc_keys(c):
        ks = pl.ds(pl.multiple_of(c * NSA_KC, NSA_KC), NSA_KC)
        return (kvs_ref[0, ks, :],), oh_ref[ks, :]

    c_last = q0 // NSA_KC
    kpos = c_last * NSA_KC + lax.broadcasted_iota(jnp.int32, (rows, NSA_KC), 1)
    tpos = q0 + (lax.broadcasted_iota(jnp.int32, (rows, NSA_KC), 0) & (tq - 1))
    _attend((q_ext,), slc_keys, c_last, kpos <= tpos, m_sc, acc_sc, sa_sc, sb_sc)
    o_slc = _softmax_finish(acc_sc[0])

    outs = [partial[r] + gate_slc[r] * head(o_slc, r) for r in range(r_heads)]
    for pr in range(r_heads // 2):
        pair = jnp.where(lo, pltpu.roll(outs[2 * pr], HEAD_DIM, 1), outs[2 * pr + 1])
        o_ref[0, :, pr * LANES:(pr + 1) * LANES] = pair.astype(BF16)


def _nsa(qn, gt, tabs, kvc, sw, ov):
    b, seq = qn.shape[:2]
    g = NSA_KV_HEADS
    tq = NSA_TQ
    rows = NSA_GROUP * tq
    n_pad = kvc.shape[2]
    tab = pl.BlockSpec((tq, LANES), lambda i, j, k: (k, 0))
    return pl.pallas_call(
        _nsa_kernel,
        grid=(b, g, seq // tq),
        in_specs=[
            pl.BlockSpec((1, tq, 2 * LANES), lambda i, j, k: (i, k, j)),
            pl.BlockSpec((1, tq, LANES), lambda i, j, k: (i, k, j)),
            tab, tab, tab,
            pl.BlockSpec((1, 1, n_pad, LANES), lambda i, j, k: (i, j, 0, 0)),
            pl.BlockSpec((1, seq, LANES), lambda i, j, k: (i, 0, 2 * j)),
            pl.BlockSpec((1, seq, LANES), lambda i, j, k: (i, 0, 2 * j + 1)),
            pl.BlockSpec((n_pad, LANES), lambda i, j, k: (0, 0)),
            pl.BlockSpec((seq, LANES), lambda i, j, k: (0, 0)),
        ],
        out_specs=pl.BlockSpec((1, tq, 2 * LANES), lambda i, j, k: (i, k, j)),
        out_shape=jax.ShapeDtypeStruct((b, seq, NSA_Q_COLS), BF16),
        scratch_shapes=[pltpu.VMEM((1, rows, 1), F32), pltpu.VMEM((1, rows, 2 * LANES), F32),
                        pltpu.VMEM((1, rows, NSA_KC), F32), pltpu.VMEM((1, rows, NSA_KC), F32)],
        compiler_params=_params(("parallel", "parallel", "arbitrary")),
        name="nsa",
    )(qn, gt, *tabs, kvc, sw, sw, ov, _block_onehot(seq, SLC_BLOCK))


def _moba_kernel(q_ref, kv_ref, km_ref, oh_ref, o_ref, m_sc, acc_sc, sa_sc, sb_sc):
    tq = MOBA_BLOCK
    heads = MOBA_HEADS_PER_STEP
    qi = pl.program_id(2)
    q0 = pl.multiple_of(qi * tq, tq)
    lo = lax.broadcasted_iota(jnp.int32, (tq, LANES), 1) < HEAD_DIM
    n_blk = km_ref.shape[1]
    blk = lax.broadcasted_iota(jnp.int32, (n_blk, tq), 0)
    past = blk < qi

    def slab(h):
        return slice(h * LANES, (h + 1) * LANES)

    q_ext = []
    for h in range(heads):
        q = jnp.where(lo if h % 2 == 0 else ~lo, q_ref[0, :, slab(h // 2)], jnp.zeros((), BF16))
        gate_t = _dot_nt(km_ref[0, :, slab(h)].astype(BF16), q)
        top = _rank_below(jnp.where(past, gate_t, NEG), n_blk, MOBA_TOPK)
        bias = _block_bias(((top & past) | (blk == qi)).astype(F32), tq)
        q_ext.append(jnp.concatenate([q, bias], axis=1))

    m_sc[...] = jnp.full(m_sc.shape, NEG, F32)
    acc_sc[...] = jnp.zeros(acc_sc.shape, F32)

    kc = 2 * tq

    def keys(c):
        ks = pl.ds(pl.multiple_of(c * kc, kc), kc)
        return tuple(kv_ref[0, ks, slab(h)] for h in range(heads)), oh_ref[ks, :]

    c_last = qi // 2
    kpos = c_last * kc + lax.broadcasted_iota(jnp.int32, (tq, kc), 1)
    tpos = q0 + lax.broadcasted_iota(jnp.int32, (tq, kc), 0)
    _attend(q_ext, keys, c_last, kpos <= tpos, m_sc, acc_sc, sa_sc, sb_sc)

    for pr in range(heads // 2):
        even = _softmax_finish(acc_sc[2 * pr])
        odd = _softmax_finish(acc_sc[2 * pr + 1])
        o_ref[0, :, slab(pr)] = jnp.where(lo, odd, even).astype(BF16)


def _moba(mq, mkv, km):
    b, seq = mq.shape[:2]
    tq = MOBA_BLOCK
    n_blk = seq // tq
    heads = MOBA_HEADS_PER_STEP
    return pl.pallas_call(
        _moba_kernel,
        grid=(b, MOBA_HEADS // heads, seq // tq),
        in_specs=[
            pl.BlockSpec((1, tq, heads // 2 * LANES), lambda i, j, k: (i, k, j)),
            pl.BlockSpec((1, seq, heads * LANES), lambda i, j, k: (i, 0, j)),
            pl.BlockSpec((1, n_blk, heads * LANES), lambda i, j, k: (i, 0, j)),
            pl.BlockSpec((seq, LANES), lambda i, j, k: (0, 0)),
        ],
        out_specs=pl.BlockSpec((1, tq, heads // 2 * LANES), lambda i, j, k: (i, k, j)),
        out_shape=jax.ShapeDtypeStruct((b, seq, MOBA_HEADS * HEAD_DIM), BF16),
        scratch_shapes=[pltpu.VMEM((heads, tq, 1), F32), pltpu.VMEM((heads, tq, 2 * LANES), F32),
                        pltpu.VMEM((heads, tq, 2 * tq), F32), pltpu.VMEM((heads, tq, 2 * tq), F32)],
        compiler_params=_params(("parallel", "parallel", "arbitrary")),
        name="moba",
    )(mq, mkv, km, _block_onehot(seq, MOBA_BLOCK))


def _outproj_kernel(x_ref, a_ref, m_ref, wa_ref, wm_ref, o_ref):
    o_ref[...] = x_ref[...] + _dot(a_ref[...], wa_ref[...]) + _dot(m_ref[...], wm_ref[...])


def _outproj(x2, o_nsa, o_moba, wa, wm):
    n = x2.shape[0]
    half = NSA_Q_COLS

    def rows(width):
        return pl.BlockSpec((PROJ_ROWS, width), lambda i: (i, 0))

    return pl.pallas_call(
        _outproj_kernel,
        grid=(n // PROJ_ROWS,),
        in_specs=[rows(D_MODEL), rows(half), rows(half), _resident((half, D_MODEL)), _resident((half, D_MODEL))],
        out_specs=rows(D_MODEL),
        out_shape=jax.ShapeDtypeStruct((n, D_MODEL), F32),
        compiler_params=_params(("parallel",)),
        name="outproj",
    )(x2, o_nsa, o_moba, wa, wm)


def _in_perm():
    d = HEAD_DIM
    perm = np.full((IN_COLS_P,), -1, np.int64)
    perm[SEC_Q:SEC_Q + NSA_Q_COLS] = np.arange(NSA_Q_COLS)
    kv0 = NSA_Q_COLS
    gate0 = kv0 + NSA_KV_COLS
    moba0 = gate0 + NSA_GATE_COLS
    per_group = NSA_GROUP * 3
    for g in range(NSA_KV_HEADS):
        perm[SEC_GATE + g * LANES:SEC_GATE + g * LANES + per_group] = gate0 + g * per_group + np.arange(per_group)

    def kv_cols(which, g):
        return kv0 + (which * NSA_KV_HEADS + g) * d + np.arange(d)

    for g in range(NSA_KV_HEADS):
        base = SEC_SW + g * 2 * LANES
        perm[base:base + d] = kv_cols(2, g)
        perm[base + d:base + 2 * d] = kv_cols(3, g)
        perm[base + 2 * d:base + 3 * d] = kv_cols(4, g)
        perm[base + 3 * d:base + 4 * d] = kv_cols(5, g)
        base = SEC_CMP + g * LANES
        perm[base:base + d] = kv_cols(0, g)
        perm[base + d:base + 2 * d] = kv_cols(1, g)
    perm[SEC_MQ:SEC_MQ + MOBA_HEADS * d] = moba0 + np.arange(MOBA_HEADS * d)
    for h in range(MOBA_HEADS):
        k_cols = moba0 + (MOBA_HEADS + h) * d + np.arange(d)
        v_cols = moba0 + (2 * MOBA_HEADS + h) * d + np.arange(d)
        base = SEC_MKV + h * LANES
        first, second = (k_cols, v_cols) if h % 2 == 0 else (v_cols, k_cols)
        perm[base:base + d] = first
        perm[base + d:base + 2 * d] = second
    return perm


def _moba_out_rows():
    d = HEAD_DIM
    rows = []
    for pr in range(MOBA_HEADS // 2):
        rows.append(NSA_Q_COLS + (2 * pr + 1) * d + np.arange(d))
        rows.append(NSA_Q_COLS + (2 * pr) * d + np.arange(d))
    return np.concatenate(rows)


def _overlap_table(seq):
    n_pad = seq // CMP_STRIDE
    n = np.arange(n_pad)[:, None]
    j = np.arange(LANES)[None, :]
    starts = n * CMP_STRIDE
    sb = j * SLC_BLOCK
    ov = (starts < sb + SLC_BLOCK) & (starts + CMP_LEN > sb) & (n < n_pad - 1) & (j < seq // SLC_BLOCK)
    return ov.astype(np.float32)


def _block_onehot(seq, block):
    onehot = np.arange(seq)[:, None] // block == np.arange(LANES)[None, :]
    return jnp.asarray(onehot.astype(np.float32), BF16)


def _rope_lane_tables(seq):
    half = ROT_DIM // 2
    inv_freq = jnp.power(ROPE_THETA, -(jnp.arange(0, ROT_DIM, 2, dtype=F32) / ROT_DIM))
    ang = jnp.arange(seq, dtype=F32)[:, None] * inv_freq[None, :]
    cos, sin = jnp.cos(ang), jnp.sin(ang)
    rest = HEAD_DIM - ROT_DIM
    zeros_h = jnp.zeros((seq, half), F32)
    c = jnp.concatenate([cos, cos, jnp.ones((seq, rest), F32)], axis=1)
    sl = jnp.concatenate([-sin, zeros_h, jnp.zeros((seq, rest), F32)], axis=1)
    sr = jnp.concatenate([zeros_h, sin, jnp.zeros((seq, rest), F32)], axis=1)
    return tuple(jnp.tile(t, (1, LANES // HEAD_DIM)) for t in (c, sl, sr))


def _mixer(x2, b, seq, g_mix, w_in, pos_ck, w_ck1, w_ck2, pos_cv, w_cv1, w_cv2, w_out, tabs, ov):
    perm = _in_perm()
    w_p = jnp.where(perm[None, :] >= 0, w_in[:, np.maximum(perm, 0)], 0.0).astype(BF16)
    qn, gt, sw, cmp_in, mq, mkv, km = _inproj(x2, g_mix, w_p, tabs, seq)

    n_half = seq // CMP_STRIDE
    hcmp = cmp_in.reshape(b, n_half, CMP_STRIDE, NSA_KV_HEADS, 2, HEAD_DIM)
    hcmp = hcmp.transpose(0, 3, 4, 1, 2, 5).reshape(b, NSA_KV_HEADS, 2, n_half, CMP_STRIDE * HEAD_DIM)
    pos = jnp.stack([pos_ck.reshape(2, -1), pos_cv.reshape(2, -1)])
    w1 = jnp.stack([w_ck1, w_cv1]).astype(BF16)
    zeros = jnp.zeros_like(w_ck2)
    w2p = jnp.stack([jnp.concatenate([w_ck2, zeros], axis=1),
                     jnp.concatenate([zeros, w_cv2], axis=1)]).astype(BF16)
    kvc = _compress(hcmp, pos, w1, w2p)

    o_nsa = _nsa(qn.reshape(b, seq, -1), gt.reshape(b, seq, -1), tabs, kvc, sw.reshape(b, seq, -1), ov)
    o_moba = _moba(mq.reshape(b, seq, -1), mkv.reshape(b, seq, -1), km.reshape(b, seq // MOBA_BLOCK, -1))

    wa = w_out[:NSA_Q_COLS].astype(BF16)
    wm = w_out[_moba_out_rows()].astype(BF16)
    n = b * seq
    return _outproj(x2, o_nsa.reshape(n, -1), o_moba.reshape(n, -1), wa, wm)


def kernel(x, norm_ffn1, w_ffn1_gate, w_ffn1_up, w_ffn1_down, norm_mix, w_in, pos_ck, w_ck1, w_ck2,
           pos_cv, w_cv1, w_cv2, w_out, norm_ffn2, w_ffn2_gate, w_ffn2_up, w_ffn2_down, norm_final):
    b, seq, d = x.shape
    assert d == D_MODEL and seq % PROJ_ROWS == 0 and SLC_TOPN <= seq // SLC_BLOCK <= LANES
    depth = norm_ffn1.shape[0]
    tabs = _rope_lane_tables(seq)
    ov = jnp.asarray(_overlap_table(seq), BF16)
    x2 = x.reshape(b * seq, d)
    for l in range(depth):
        x2 = _ffn(x2, norm_ffn1[l], w_ffn1_gate[l].astype(BF16), w_ffn1_up[l].astype(BF16),
                  w_ffn1_down[l].astype(BF16))
        x2 = _mixer(x2, b, seq, norm_mix[l], w_in[l], pos_ck[l], w_ck1[l], w_ck2[l],
                    pos_cv[l], w_cv1[l], w_cv2[l], w_out[l], tabs, ov)
        x2 = _ffn(x2, norm_ffn2[l], w_ffn2_gate[l].astype(BF16), w_ffn2_up[l].astype(BF16),
                  w_ffn2_down[l].astype(BF16), norm_final if l == depth - 1 else None)
    return x2.reshape(b, seq, d)
```

```python
import functools

import numpy as np
import jax
import jax.numpy as jnp
from jax import lax
from jax.experimental import pallas as pl
from jax.experimental.pallas import tpu as pltpu

D_MODEL = 1024
HEAD_DIM = 64
NSA_HEADS = 8
NSA_KV_HEADS = 2
NSA_GROUP = NSA_HEADS // NSA_KV_HEADS
MOBA_HEADS = 8
ROT_DIM = HEAD_DIM // 4
ROPE_THETA = 500000.0
CMP_LEN = 32
CMP_STRIDE = 16
CMP_HIDDEN = 256
SLC_BLOCK = 64
SLC_TOPN = 16
WINDOW = 512
MOBA_BLOCK = 256
MOBA_TOPK = 3
D_FF = 2816
EPS = 1e-6
NEG = -1e30
FORCE_BONUS = 1e4
SCALE = HEAD_DIM ** -0.5

NSA_Q_COLS = NSA_HEADS * HEAD_DIM
NSA_KV_COLS = 3 * 2 * NSA_KV_HEADS * HEAD_DIM
NSA_GATE_COLS = NSA_HEADS * 3
MOBA_COLS = 3 * MOBA_HEADS * HEAD_DIM

LANES = 128
SUBLANES = 8
VMEM_LIMIT = 56 * 1024 * 1024

SEC_Q = 0
SEC_GATE = 512
SEC_SW = 768
SEC_CMP = 1280
SEC_MQ = 1536
SEC_MKV = 2048
IN_COLS_P = 3072

F32 = jnp.float32
BF16 = jnp.bfloat16

FFN_ROWS = 512
FFN_CHUNK = 256
PROJ_ROWS = 512
NSA_TQ = 256
NSA_KC = 512
MOBA_HEADS_PER_STEP = 4
MOBA_TQ = 2 * MOBA_BLOCK
MASK_BIAS = 2.0 ** 100


def _dot(a, b):
    return jnp.dot(a, b, preferred_element_type=F32)


def _dot_nt(a, b):
    return lax.dot_general(a, b, (((1,), (1,)), ((), ())), preferred_element_type=F32)


def _rms(x, g):
    return x * lax.rsqrt(jnp.mean(x * x, axis=-1, keepdims=True) + EPS) * g


def _params(sem):
    return pltpu.CompilerParams(dimension_semantics=sem, vmem_limit_bytes=VMEM_LIMIT)


def _resident(shape, layer=None):
    nd = len(shape)
    if layer is None:
        return pl.BlockSpec(shape, lambda *_: (0,) * nd, pipeline_mode=pl.Buffered(1))
    return pl.BlockSpec((None,) + tuple(shape), lambda *_: (layer,) + (0,) * nd,
                        pipeline_mode=pl.Buffered(1))


def _ffn_kernel(x_ref, g_ref, wg_ref, wu_ref, wd_ref, *rest, final_norm):
    if final_norm:
        gf_ref, o_ref, h_ref = rest
    else:
        o_ref, h_ref = rest
    x = x_ref[...]
    xb = _rms(x, g_ref[...]).astype(BF16)
    for c in range(D_FF // FFN_CHUNK):
        sl = slice(c * FFN_CHUNK, (c + 1) * FFN_CHUNK)
        gate = _dot(xb, wg_ref[:, sl])
        up = _dot(xb, wu_ref[:, sl])
        h_ref[:, sl] = (jax.nn.silu(gate) * up).astype(BF16)
    y = x + 0.5 * _dot(h_ref[...], wd_ref[...])
    if final_norm:
        y = _rms(y, gf_ref[...])
    o_ref[...] = y


def _ffn(x2, g, wg, wu, wd, layer, g_final=None):
    n = x2.shape[0]
    final_norm = g_final is not None
    row = pl.BlockSpec((FFN_ROWS, D_MODEL), lambda i: (i, 0))
    in_specs = [row, _resident((1, D_MODEL)), _resident((D_MODEL, D_FF), layer),
                _resident((D_MODEL, D_FF), layer), _resident((D_FF, D_MODEL), layer)]
    args = [x2, g.reshape(1, D_MODEL), wg, wu, wd]
    if final_norm:
        in_specs.append(_resident((1, D_MODEL)))
        args.append(g_final.reshape(1, D_MODEL))
    return pl.pallas_call(
        functools.partial(_ffn_kernel, final_norm=final_norm),
        grid=(n // FFN_ROWS,),
        in_specs=in_specs,
        out_specs=row,
        out_shape=jax.ShapeDtypeStruct((n, D_MODEL), F32),
        scratch_shapes=[pltpu.VMEM((FFN_ROWS, D_FF), BF16)],
        compiler_params=_params(("parallel",)),
        name="ffn_final" if final_norm else "ffn",
    )(*args)


def _rope_tables(c, sl, sr, mode):
    if mode == "both":
        return c, sl, sr
    lane = lax.broadcasted_iota(jnp.int32, c.shape, 1)
    keep = lane < HEAD_DIM if mode == "lo" else lane >= HEAD_DIM
    return jnp.where(keep, c, 1.0), jnp.where(keep, sl, 0.0), jnp.where(keep, sr, 0.0)


def _rope(v, tabs):
    c, sl, sr = tabs
    return v * c + pltpu.roll(v, LANES - ROT_DIM // 2, 1) * sl + pltpu.roll(v, ROT_DIM // 2, 1) * sr


def _inproj_kernel(x_ref, g_ref, w_ref, c_ref, sl_ref, sr_ref,
                   qn_ref, gt_ref, sw_ref, cmp_ref, mq_ref, mkv_ref, km_ref):
    xb = _rms(x_ref[...], g_ref[...]).astype(BF16)
    base = (c_ref[...], sl_ref[...], sr_ref[...])
    tabs = {m: _rope_tables(*base, m) for m in ("both", "lo", "hi")}

    qn_ref[...] = _dot(xb, w_ref[:, SEC_Q:SEC_GATE])
    gt_ref[...] = _dot(xb, w_ref[:, SEC_GATE:SEC_SW])
    cmp_ref[...] = _dot(xb, w_ref[:, SEC_CMP:SEC_MQ])

    sw = _dot(xb, w_ref[:, SEC_SW:SEC_CMP])
    for i in range(4):
        sl = slice(i * LANES, (i + 1) * LANES)
        sw_ref[:, sl] = _rope(sw[:, sl], tabs["lo"]).astype(BF16)

    mq = _dot(xb, w_ref[:, SEC_MQ:SEC_MKV])
    for i in range(4):
        sl = slice(i * LANES, (i + 1) * LANES)
        mq_ref[:, sl] = (_rope(mq[:, sl], tabs["both"]) * SCALE).astype(BF16)

    mkv = _dot(xb, w_ref[:, SEC_MKV:IN_COLS_P])
    nblk = PROJ_ROWS // MOBA_BLOCK
    for h in range(MOBA_HEADS):
        sl = slice(h * LANES, (h + 1) * LANES)
        r = _rope(mkv[:, sl], tabs["lo" if h % 2 == 0 else "hi"])
        mkv_ref[:, sl] = r.astype(BF16)
        km_ref[:, 0, sl] = jnp.mean(r.reshape(nblk, MOBA_BLOCK, LANES), axis=1)


def _inproj(x2, g, w_p, layer, tabs, seq):
    n = x2.shape[0]
    tiles_per_seq = seq // PROJ_ROWS
    nblk = PROJ_ROWS // MOBA_BLOCK

    def rows(width):
        return pl.BlockSpec((PROJ_ROWS, width), lambda i: (i, 0))

    tab = pl.BlockSpec((PROJ_ROWS, LANES), lambda i: (i % tiles_per_seq, 0))
    out_shapes = (
        jax.ShapeDtypeStruct((n, 512), F32),
        jax.ShapeDtypeStruct((n, 256), F32),
        jax.ShapeDtypeStruct((n, 512), BF16),
        jax.ShapeDtypeStruct((n, 256), F32),
        jax.ShapeDtypeStruct((n, 512), BF16),
        jax.ShapeDtypeStruct((n, 1024), BF16),
        jax.ShapeDtypeStruct((n // MOBA_BLOCK, 1, 1024), F32),
    )
    out_specs = (rows(512), rows(256), rows(512), rows(256), rows(512), rows(1024),
                 pl.BlockSpec((nblk, 1, 1024), lambda i: (i, 0, 0)))
    return pl.pallas_call(
        _inproj_kernel,
        grid=(n // PROJ_ROWS,),
        in_specs=[rows(D_MODEL), _resident((1, D_MODEL)), _resident((D_MODEL, IN_COLS_P), layer),
                  tab, tab, tab],
        out_specs=out_specs,
        out_shape=out_shapes,
        compiler_params=_params(("parallel",)),
        name="inproj",
    )(x2, g.reshape(1, D_MODEL), w_p, *tabs)


def _compress_kernel(x_ref, pos_ref, wa_ref, wb_ref, w2_ref, o_ref):
    n_pad = x_ref.shape[1] // CMP_STRIDE
    width = 2 * CMP_HIDDEN
    ya = jnp.zeros((n_pad, width), F32)
    zb = jnp.zeros((n_pad, width), F32)
    for l in range(CMP_STRIDE):
        h = x_ref[0, pl.ds(l, n_pad, stride=CMP_STRIDE), :]
        ya = ya + _dot((h + pos_ref[0, l:l + 1, :]).astype(BF16), wa_ref[l])
        zb = zb + _dot((h + pos_ref[1, l:l + 1, :]).astype(BF16), wb_ref[l])
    pre = ya + pltpu.roll(zb, n_pad - 1, 0)
    out = _dot(jax.nn.gelu(pre).astype(BF16), w2_ref[...])
    row = lax.broadcasted_iota(jnp.int32, out.shape, 0)
    o_ref[0, 0] = jnp.where(row < n_pad - 1, out, 0.0).astype(BF16)


def _compress(cmp_in, pos, wa, wb, w2):
    b, seq = cmp_in.shape[:2]
    g = NSA_KV_HEADS
    n_pad = seq // CMP_STRIDE
    return pl.pallas_call(
        _compress_kernel,
        grid=(b, g),
        in_specs=[pl.BlockSpec((1, seq, LANES), lambda i, j: (i, 0, j)),
                  _resident(pos.shape), _resident(wa.shape), _resident(wb.shape), _resident(w2.shape)],
        out_specs=pl.BlockSpec((1, 1, n_pad, LANES), lambda i, j: (i, j, 0, 0)),
        out_shape=jax.ShapeDtypeStruct((b, g, n_pad, LANES), BF16),
        compiler_params=_params(("parallel", "parallel")),
        name="compress",
    )(cmp_in, pos, wa, wb, w2)


def _rank_below(s_t, n_rows, limit):
    t = s_t.shape[1]
    n_grp = n_rows // SUBLANES
    grp = [s_t[g * SUBLANES:(g + 1) * SUBLANES] for g in range(n_grp)]
    cnt = [jnp.zeros((SUBLANES, t), F32) for _ in range(n_grp)]
    sub = lax.broadcasted_iota(jnp.int32, (SUBLANES, t), 0)
    for i in range(n_rows):
        ri = s_t[i:i + 1, :]
        for g in range(n_grp):
            if g > i // SUBLANES:
                beats = ri >= grp[g]
            elif g < i // SUBLANES:
                beats = ri > grp[g]
            else:
                beats = (ri > grp[g]) | ((ri == grp[g]) & (sub > i % SUBLANES))
            cnt[g] = cnt[g] + jnp.where(beats, 1.0, 0.0)
    return jnp.concatenate(cnt, axis=0) < limit


def _block_bias(sel_t, t):
    pad = jnp.zeros((LANES - sel_t.shape[0], t), F32)
    sel = jnp.concatenate([sel_t, pad], axis=0).T
    return ((sel - 1.0) * MASK_BIAS).astype(BF16)


def _with_ones(kv):
    return jnp.concatenate([kv, jnp.ones(kv.shape, kv.dtype)], axis=1)


def _softmax_step(s, kv, m_ref, acc_ref):
    m_old = m_ref[...]
    m_new = jnp.maximum(m_old, jnp.max(s, axis=-1, keepdims=True))
    alpha = jnp.exp(m_old - m_new)
    p = jnp.exp(s - m_new).astype(BF16)
    acc_ref[...] = alpha * acc_ref[...] + _dot(p, _with_ones(kv))
    m_ref[...] = m_new


def _attend(q_ext, keys, n_past, causal, m_ref, acc_ref, s_a, s_b):
    def put_scores(c, s_ref):
        kvs, onehot = keys(c)
        for h, (q, kv) in enumerate(zip(q_ext, kvs)):
            s_ref[h] = _dot_nt(q, jnp.concatenate([kv, onehot], axis=1))

    def softmax(c, s_ref, mask=None):
        for h, kv in enumerate(keys(c)[0]):
            s = s_ref[h] if mask is None else jnp.where(mask, s_ref[h], -MASK_BIAS)
            _softmax_step(s, kv, m_ref.at[h], acc_ref.at[h])

    put_scores(0, s_a)

    def body(i, carry):
        c = 2 * i
        put_scores(c + 1, s_b)
        softmax(c, s_a)
        put_scores(c + 2, s_a)
        softmax(c + 1, s_b)
        return carry

    lax.fori_loop(0, n_past // 2, body, 0)

    @pl.when(n_past % 2 == 1)
    def _():
        put_scores(n_past, s_b)
        softmax(n_past - 1, s_a)
        softmax(n_past, s_b, causal)

    @pl.when(n_past % 2 == 0)
    def _():
        softmax(n_past, s_a, causal)


def _softmax_finish(acc):
    return acc[:, :LANES] / jnp.maximum(acc[:, LANES:], 1.0)


def _nsa_kernel(qn_ref, gt_ref, c_ref, sl_ref, sr_ref, kvc_ref, sw_ref, ov_ref, oh_ref,
                o_ref, m_sc, acc_sc, sa_sc, sb_sc):
    tq, r_heads, groups = NSA_TQ, NSA_GROUP, NSA_KV_HEADS
    rows = r_heads * tq
    q0 = pl.program_id(1) * tq
    tabs = (c_ref[...], sl_ref[...], sr_ref[...])
    lo = lax.broadcasted_iota(jnp.int32, (tq, LANES), 1) < HEAD_DIM
    n_slc = sw_ref.shape[1] // SLC_BLOCK
    blk_shift = SLC_BLOCK.bit_length() - 1

    def slab(i):
        return slice(i * LANES, (i + 1) * LANES)

    def head(o, r):
        return o[r * tq:(r + 1) * tq]

    def front(g):
        def stack(rotate):
            parts = []
            for r in range(r_heads):
                pair = qn_ref[0, :, slab(2 * g + r // 2)]
                if rotate:
                    pair = _rope(pair, tabs)
                if r % 2:
                    pair = pltpu.roll(pair, HEAD_DIM, 1)
                parts.append(jnp.where(lo, pair * SCALE, 0.0).astype(BF16))
            return jnp.concatenate(parts, axis=0)

        q_plain = stack(False)
        q_rot = stack(True)

        w0 = pl.multiple_of(jnp.maximum(q0 - WINDOW, 0), tq)
        kvw = sw_ref[0, pl.ds(w0, WINDOW + tq), slab(2 * g + 1)]
        d = (q0 - w0) + (lax.broadcasted_iota(jnp.int32, (tq, WINDOW + tq), 0)
                         - lax.broadcasted_iota(jnp.int32, (tq, WINDOW + tq), 1))
        valid_w = ((d >= 0) & (d < WINDOW))[None]
        s = jnp.where(valid_w, _dot_nt(q_rot, kvw).reshape(r_heads, tq, WINDOW + tq), NEG)
        p = jnp.exp(s - jnp.max(s, axis=-1, keepdims=True)).reshape(rows, WINDOW + tq).astype(BF16)
        o_win = _softmax_finish(_dot(p, _with_ones(kvw)))

        kvc = kvc_ref[0, g]
        n_pad = kvc.shape[0]
        n_idx = lax.broadcasted_iota(jnp.int32, (tq, n_pad), 1)
        t_idx = q0 + lax.broadcasted_iota(jnp.int32, (tq, n_pad), 0)
        valid_c = (n_idx * CMP_STRIDE + CMP_LEN - 1 <= t_idx)[None]
        s = jnp.where(valid_c, _dot_nt(q_plain, kvc).reshape(r_heads, tq, n_pad), NEG)
        p = jnp.where(valid_c, jnp.exp(s - jnp.max(s, axis=-1, keepdims=True)), 0.0)
        p = p / jnp.maximum(jnp.sum(p, axis=-1, keepdims=True), 1.0)
        o_cmp = _dot(p.reshape(rows, n_pad).astype(BF16), kvc)

        p_sum = p[0] + p[1] + p[2] + p[3]
        ov = ov_ref[...]
        hi = p_sum.astype(BF16)
        r1 = p_sum - hi.astype(F32)
        mid = r1.astype(BF16)
        low = (r1 - mid.astype(F32)).astype(BF16)
        imp = _dot(hi, ov) + _dot(mid, ov) + _dot(low, ov)

        j_idx = lax.broadcasted_iota(jnp.int32, (tq, LANES), 1)
        t_row = q0 + lax.broadcasted_iota(jnp.int32, (tq, LANES), 0)
        qblk = jnp.right_shift(t_row, blk_shift)
        forced = (j_idx == 0) | (j_idx == qblk) | (j_idx == qblk - 1)
        visible = j_idx * SLC_BLOCK <= t_row
        score = jnp.where(visible, jnp.where(forced, FORCE_BONUS, imp), NEG)
        sel_t = _rank_below(score.T[:n_slc], n_slc, SLC_TOPN).astype(F32)
        bias = _block_bias(sel_t, tq)

        gate = jax.nn.sigmoid(gt_ref[0, :, slab(g)])

        def gate_of(r, branch):
            return jnp.broadcast_to(gate[:, 3 * r + branch:3 * r + branch + 1], (tq, LANES))

        partial = [gate_of(r, 0) * head(o_cmp, r) + gate_of(r, 2) * head(o_win, r) for r in range(r_heads)]
        gate_slc = [gate_of(r, 1) for r in range(r_heads)]
        q_ext = jnp.concatenate([q_rot, jnp.concatenate([bias] * r_heads, axis=0)], axis=1)
        return q_ext, partial, gate_slc

    fronts = [front(g) for g in range(groups)]

    m_sc[...] = jnp.full(m_sc.shape, NEG, F32)
    acc_sc[...] = jnp.zeros(acc_sc.shape, F32)

    def slc_keys(c):
        ks = pl.ds(pl.multiple_of(c * NSA_KC, NSA_KC), NSA_KC)
        return tuple(sw_ref[0, ks, slab(2 * g)] for g in range(groups)), oh_ref[ks, :]

    c_last = q0 // NSA_KC
    kpos = c_last * NSA_KC + lax.broadcasted_iota(jnp.int32, (rows, NSA_KC), 1)
    tpos = q0 + (lax.broadcasted_iota(jnp.int32, (rows, NSA_KC), 0) & (tq - 1))
    _attend([f[0] for f in fronts], slc_keys, c_last, kpos <= tpos, m_sc, acc_sc, sa_sc, sb_sc)

    for g, (_, partial, gate_slc) in enumerate(fronts):
        o_slc = _softmax_finish(acc_sc[g])
        outs = [partial[r] + gate_slc[r] * head(o_slc, r) for r in range(r_heads)]
        for pr in range(r_heads // 2):
            pair = jnp.where(lo, pltpu.roll(outs[2 * pr], HEAD_DIM, 1), outs[2 * pr + 1])
            o_ref[0, :, slab(2 * g + pr)] = pair.astype(BF16)


def _nsa(qn, gt, tabs, kvc, sw, ov):
    b, seq = qn.shape[:2]
    groups = NSA_KV_HEADS
    tq = NSA_TQ
    rows = NSA_GROUP * tq
    n_pad = kvc.shape[2]
    tab = pl.BlockSpec((tq, LANES), lambda i, k: (k, 0))
    return pl.pallas_call(
        _nsa_kernel,
        grid=(b, seq // tq),
        in_specs=[
            pl.BlockSpec((1, tq, NSA_Q_COLS), lambda i, k: (i, k, 0)),
            pl.BlockSpec((1, tq, groups * LANES), lambda i, k: (i, k, 0)),
            tab, tab, tab,
            pl.BlockSpec((1, groups, n_pad, LANES), lambda i, k: (i, 0, 0, 0)),
            pl.BlockSpec((1, seq, 2 * groups * LANES), lambda i, k: (i, 0, 0)),
            pl.BlockSpec((n_pad, LANES), lambda i, k: (0, 0)),
            pl.BlockSpec((seq, LANES), lambda i, k: (0, 0)),
        ],
        out_specs=pl.BlockSpec((1, tq, NSA_Q_COLS), lambda i, k: (i, k, 0)),
        out_shape=jax.ShapeDtypeStruct((b, seq, NSA_Q_COLS), BF16),
        scratch_shapes=[pltpu.VMEM((groups, rows, 1), F32), pltpu.VMEM((groups, rows, 2 * LANES), F32),
                        pltpu.VMEM((groups, rows, NSA_KC), F32), pltpu.VMEM((groups, rows, NSA_KC), F32)],
        compiler_params=_params(("parallel", "arbitrary")),
        name="nsa",
    )(qn, gt, *tabs, kvc, sw, ov, _block_onehot(seq, SLC_BLOCK))


def _moba_kernel(q_ref, kv_ref, km_ref, oh_ref, o_ref, m_sc, acc_sc, sa_sc, sb_sc):
    tq = MOBA_TQ
    heads = MOBA_HEADS_PER_STEP
    q0 = pl.multiple_of(pl.program_id(2) * tq, tq)
    lo = lax.broadcasted_iota(jnp.int32, (tq, LANES), 1) < HEAD_DIM
    n_blk = km_ref.shape[1]
    blk_shift = MOBA_BLOCK.bit_length() - 1
    blk = lax.broadcasted_iota(jnp.int32, (n_blk, tq), 0)
    qblk = jnp.right_shift(q0 + lax.broadcasted_iota(jnp.int32, (n_blk, tq), 1), blk_shift)
    past = blk < qblk

    def slab(h):
        return slice(h * LANES, (h + 1) * LANES)

    q_ext = []
    for h in range(heads):
        q = jnp.where(lo if h % 2 == 0 else ~lo, q_ref[0, :, slab(h // 2)], jnp.zeros((), BF16))
        gate_t = _dot_nt(km_ref[0, :, slab(h)].astype(BF16), q)
        top = _rank_below(jnp.where(past, gate_t, NEG), n_blk, MOBA_TOPK)
        bias = _block_bias(((top & past) | (blk == qblk)).astype(F32), tq)
        q_ext.append(jnp.concatenate([q, bias], axis=1))

    m_sc[...] = jnp.full(m_sc.shape, NEG, F32)
    acc_sc[...] = jnp.zeros(acc_sc.shape, F32)

    def keys(c):
        ks = pl.ds(pl.multiple_of(c * tq, tq), tq)
        return tuple(kv_ref[0, ks, slab(h)] for h in range(heads)), oh_ref[ks, :]

    causal = lax.broadcasted_iota(jnp.int32, (tq, tq), 1) <= lax.broadcasted_iota(jnp.int32, (tq, tq), 0)
    _attend(q_ext, keys, pl.program_id(2), causal, m_sc, acc_sc, sa_sc, sb_sc)

    for pr in range(heads // 2):
        even = _softmax_finish(acc_sc[2 * pr])
        odd = _softmax_finish(acc_sc[2 * pr + 1])
        o_ref[0, :, slab(pr)] = jnp.where(lo, odd, even).astype(BF16)


def _moba(mq, mkv, km):
    b, seq = mq.shape[:2]
    tq = MOBA_TQ
    n_blk = seq // MOBA_BLOCK
    heads = MOBA_HEADS_PER_STEP
    return pl.pallas_call(
        _moba_kernel,
        grid=(b, MOBA_HEADS // heads, seq // tq),
        in_specs=[
            pl.BlockSpec((1, tq, heads // 2 * LANES), lambda i, j, k: (i, k, j)),
            pl.BlockSpec((1, seq, heads * LANES), lambda i, j, k: (i, 0, j)),
            pl.BlockSpec((1, n_blk, heads * LANES), lambda i, j, k: (i, 0, j)),
            pl.BlockSpec((seq, LANES), lambda i, j, k: (0, 0)),
        ],
        out_specs=pl.BlockSpec((1, tq, heads // 2 * LANES), lambda i, j, k: (i, k, j)),
        out_shape=jax.ShapeDtypeStruct((b, seq, MOBA_HEADS * HEAD_DIM), BF16),
        scratch_shapes=[pltpu.VMEM((heads, tq, 1), F32), pltpu.VMEM((heads, tq, 2 * LANES), F32),
                        pltpu.VMEM((heads, tq, tq), F32), pltpu.VMEM((heads, tq, tq), F32)],
        compiler_params=_params(("parallel", "parallel", "arbitrary")),
        name="moba",
    )(mq, mkv, km, _block_onehot(seq, MOBA_BLOCK))


def _outproj_kernel(x_ref, a_ref, m_ref, wa_ref, wm_ref, o_ref):
    o_ref[...] = x_ref[...] + _dot(a_ref[...], wa_ref[...]) + _dot(m_ref[...], wm_ref[...])


def _outproj(x2, o_nsa, o_moba, wa, wm):
    n = x2.shape[0]
    half = NSA_Q_COLS

    def rows(width):
        return pl.BlockSpec((PROJ_ROWS, width), lambda i: (i, 0))

    return pl.pallas_call(
        _outproj_kernel,
        grid=(n // PROJ_ROWS,),
        in_specs=[rows(D_MODEL), rows(half), rows(half), _resident((half, D_MODEL)), _resident((half, D_MODEL))],
        out_specs=rows(D_MODEL),
        out_shape=jax.ShapeDtypeStruct((n, D_MODEL), F32),
        compiler_params=_params(("parallel",)),
        name="outproj",
    )(x2, o_nsa, o_moba, wa, wm)


def _in_perm():
    d = HEAD_DIM
    perm = np.full((IN_COLS_P,), -1, np.int64)
    perm[SEC_Q:SEC_Q + NSA_Q_COLS] = np.arange(NSA_Q_COLS)
    kv0 = NSA_Q_COLS
    gate0 = kv0 + NSA_KV_COLS
    moba0 = gate0 + NSA_GATE_COLS
    per_group = NSA_GROUP * 3
    for g in range(NSA_KV_HEADS):
        perm[SEC_GATE + g * LANES:SEC_GATE + g * LANES + per_group] = gate0 + g * per_group + np.arange(per_group)

    def kv_cols(which, g):
        return kv0 + (which * NSA_KV_HEADS + g) * d + np.arange(d)

    for g in range(NSA_KV_HEADS):
        base = SEC_SW + g * 2 * LANES
        perm[base:base + d] = kv_cols(2, g)
        perm[base + d:base + 2 * d] = kv_cols(3, g)
        perm[base + 2 * d:base + 3 * d] = kv_cols(4, g)
        perm[base + 3 * d:base + 4 * d] = kv_cols(5, g)
        base = SEC_CMP + g * LANES
        perm[base:base + d] = kv_cols(0, g)
        perm[base + d:base + 2 * d] = kv_cols(1, g)
    perm[SEC_MQ:SEC_MQ + MOBA_HEADS * d] = moba0 + np.arange(MOBA_HEADS * d)
    for h in range(MOBA_HEADS):
        k_cols = moba0 + (MOBA_HEADS + h) * d + np.arange(d)
        v_cols = moba0 + (2 * MOBA_HEADS + h) * d + np.arange(d)
        base = SEC_MKV + h * LANES
        first, second = (k_cols, v_cols) if h % 2 == 0 else (v_cols, k_cols)
        perm[base:base + d] = first
        perm[base + d:base + 2 * d] = second
    return perm


def _moba_out_rows():
    d = HEAD_DIM
    rows = []
    for pr in range(MOBA_HEADS // 2):
        rows.append(NSA_Q_COLS + (2 * pr + 1) * d + np.arange(d))
        rows.append(NSA_Q_COLS + (2 * pr) * d + np.arange(d))
    return np.concatenate(rows)


def _overlap_table(seq):
    n_pad = seq // CMP_STRIDE
    n = np.arange(n_pad)[:, None]
    j = np.arange(LANES)[None, :]
    starts = n * CMP_STRIDE
    sb = j * SLC_BLOCK
    ov = (starts < sb + SLC_BLOCK) & (starts + CMP_LEN > sb) & (n < n_pad - 1) & (j < seq // SLC_BLOCK)
    return ov.astype(np.float32)


def _block_onehot(seq, block):
    onehot = np.arange(seq)[:, None] // block == np.arange(LANES)[None, :]
    return jnp.asarray(onehot.astype(np.float32), BF16)


def _rope_lane_tables(seq):
    half = ROT_DIM // 2
    inv_freq = jnp.power(ROPE_THETA, -(jnp.arange(0, ROT_DIM, 2, dtype=F32) / ROT_DIM))
    ang = jnp.arange(seq, dtype=F32)[:, None] * inv_freq[None, :]
    cos, sin = jnp.cos(ang), jnp.sin(ang)
    rest = HEAD_DIM - ROT_DIM
    zeros_h = jnp.zeros((seq, half), F32)
    c = jnp.concatenate([cos, cos, jnp.ones((seq, rest), F32)], axis=1)
    sl = jnp.concatenate([-sin, zeros_h, jnp.zeros((seq, rest), F32)], axis=1)
    sr = jnp.concatenate([zeros_h, sin, jnp.zeros((seq, rest), F32)], axis=1)
    return tuple(jnp.tile(t, (1, LANES // HEAD_DIM)) for t in (c, sl, sr))


def _mixer(x2, b, seq, g_mix, w_p, layer, pos_ck, w_ck1, w_ck2, pos_cv, w_cv1, w_cv2, w_out, tabs, ov):
    qn, gt, sw, cmp_in, mq, mkv, km = _inproj(x2, g_mix, w_p, layer, tabs, seq)

    def halves(w_k, w_v):
        wk = w_k.reshape(2, CMP_STRIDE, HEAD_DIM, CMP_HIDDEN)
        wv = w_v.reshape(2, CMP_STRIDE, HEAD_DIM, CMP_HIDDEN)
        z = jnp.zeros_like(wk)
        return jnp.concatenate([jnp.concatenate([wk, z], axis=3), jnp.concatenate([z, wv], axis=3)],
                               axis=2).astype(BF16)

    w1 = halves(w_ck1, w_cv1)
    pos = jnp.concatenate([pos_ck, pos_cv], axis=1).reshape(2, CMP_STRIDE, LANES)
    z2 = jnp.zeros_like(w_ck2)
    w2 = jnp.concatenate([jnp.concatenate([w_ck2, z2], axis=1),
                          jnp.concatenate([z2, w_cv2], axis=1)], axis=0).astype(BF16)
    kvc = _compress(cmp_in.reshape(b, seq, -1), pos, w1[0], w1[1], w2)

    o_nsa = _nsa(qn.reshape(b, seq, -1), gt.reshape(b, seq, -1), tabs, kvc, sw.reshape(b, seq, -1), ov)
    o_moba = _moba(mq.reshape(b, seq, -1), mkv.reshape(b, seq, -1), km.reshape(b, seq // MOBA_BLOCK, -1))

    wa = w_out[:NSA_Q_COLS].astype(BF16)
    wm = w_out[_moba_out_rows()].astype(BF16)
    n = b * seq
    return _outproj(x2, o_nsa.reshape(n, -1), o_moba.reshape(n, -1), wa, wm)


def kernel(x, norm_ffn1, w_ffn1_gate, w_ffn1_up, w_ffn1_down, norm_mix, w_in, pos_ck, w_ck1, w_ck2,
           pos_cv, w_cv1, w_cv2, w_out, norm_ffn2, w_ffn2_gate, w_ffn2_up, w_ffn2_down, norm_final):
    b, seq, d = x.shape
    assert d == D_MODEL and seq % PROJ_ROWS == 0 and SLC_TOPN <= seq // SLC_BLOCK <= LANES
    depth = norm_ffn1.shape[0]
    tabs = _rope_lane_tables(seq)
    ov = jnp.asarray(_overlap_table(seq), BF16)
    x2 = x.reshape(b * seq, d)
    perm = _in_perm()
    w_p = jnp.where(perm >= 0, w_in[:, :, np.maximum(perm, 0)], 0.0).astype(BF16)
    wg1, wu1, wd1, wg2, wu2, wd2 = (w.astype(BF16) for w in (w_ffn1_gate, w_ffn1_up, w_ffn1_down,
                                                            w_ffn2_gate, w_ffn2_up, w_ffn2_down))
    for l in range(depth):
        x2 = _ffn(x2, norm_ffn1[l], wg1, wu1, wd1, l)
        x2 = _mixer(x2, b, seq, norm_mix[l], w_p, l, pos_ck[l], w_ck1[l], w_ck2[l],
                    pos_cv[l], w_cv1[l], w_cv2[l], w_out[l], tabs, ov)
        x2 = _ffn(x2, norm_ffn2[l], wg2, wu2, wd2, l, norm_final if l == depth - 1 else None)
    return x2.reshape(b, seq, d)
```

```python
import functools

import numpy as np
import jax
import jax.numpy as jnp
from jax import lax
from jax.experimental import pallas as pl
from jax.experimental.pallas import tpu as pltpu

D_MODEL = 1024
HEAD_DIM = 64
NSA_HEADS = 8
NSA_KV_HEADS = 2
NSA_GROUP = NSA_HEADS // NSA_KV_HEADS
MOBA_HEADS = 8
ROT_DIM = HEAD_DIM // 4
ROPE_THETA = 500000.0
CMP_LEN = 32
CMP_STRIDE = 16
CMP_HIDDEN = 256
SLC_BLOCK = 64
SLC_TOPN = 16
WINDOW = 512
MOBA_BLOCK = 256
MOBA_TOPK = 3
D_FF = 2816
EPS = 1e-6
NEG = -1e30
FORCE_BONUS = 1e4
SCALE = HEAD_DIM ** -0.5

NSA_Q_COLS = NSA_HEADS * HEAD_DIM
NSA_KV_COLS = 3 * 2 * NSA_KV_HEADS * HEAD_DIM
NSA_GATE_COLS = NSA_HEADS * 3
MOBA_COLS = 3 * MOBA_HEADS * HEAD_DIM

LANES = 128
SUBLANES = 8
VMEM_LIMIT = 56 * 1024 * 1024

SEC_Q = 0
SEC_GATE = 512
SEC_SW = 768
SEC_CMP = 1280
SEC_MQ = 1536
SEC_MKV = 2048
IN_COLS_P = 3072

F32 = jnp.float32
BF16 = jnp.bfloat16

FFN_ROWS = 512
FFN_CHUNK = 256
PROJ_ROWS = 512
NSA_TQ = 256
NSA_KC = 512
MOBA_HEADS_PER_STEP = 4
MOBA_TQ = 2 * MOBA_BLOCK
MASK_BIAS = 2.0 ** 100


def _dot(a, b):
    return jnp.dot(a, b, preferred_element_type=F32)


def _dot_nt(a, b):
    return lax.dot_general(a, b, (((1,), (1,)), ((), ())), preferred_element_type=F32)


def _rms(x, g):
    return x * lax.rsqrt(jnp.mean(x * x, axis=-1, keepdims=True) + EPS) * g


def _params(sem):
    return pltpu.CompilerParams(dimension_semantics=sem, vmem_limit_bytes=VMEM_LIMIT)


def _resident(shape, layer=None):
    nd = len(shape)
    if layer is None:
        return pl.BlockSpec(shape, lambda *_: (0,) * nd, pipeline_mode=pl.Buffered(1))
    return pl.BlockSpec((None,) + tuple(shape), lambda *_: (layer,) + (0,) * nd,
                        pipeline_mode=pl.Buffered(1))


def _ffn_kernel(*refs, mix, final_norm):
    refs = list(refs)
    x_ref = refs.pop(0)
    if mix:
        a_ref, mo_ref, wa_ref, wm_ref = (refs.pop(0) for _ in range(4))
    g_ref, wg_ref, wu_ref, wd_ref = (refs.pop(0) for _ in range(4))
    gf_ref = refs.pop(0) if final_norm else None
    o_ref, h_ref = refs
    x = x_ref[...]
    if mix:
        x = x + _dot(a_ref[...], wa_ref[...]) + _dot(mo_ref[...], wm_ref[...])
    xb = _rms(x, g_ref[...]).astype(BF16)
    for c in range(D_FF // FFN_CHUNK):
        sl = slice(c * FFN_CHUNK, (c + 1) * FFN_CHUNK)
        gate = _dot(xb, wg_ref[:, sl])
        up = _dot(xb, wu_ref[:, sl])
        h_ref[:, sl] = (jax.nn.silu(gate) * up).astype(BF16)
    y = x + 0.5 * _dot(h_ref[...], wd_ref[...])
    if final_norm:
        y = _rms(y, gf_ref[...])
    o_ref[...] = y


def _ffn(x2, g, wg, wu, wd, layer, mix=None, g_final=None):
    n = x2.shape[0]
    final_norm = g_final is not None

    def rows(width):
        return pl.BlockSpec((FFN_ROWS, width), lambda i: (i, 0))

    in_specs, args = [rows(D_MODEL)], [x2]
    if mix is not None:
        o_nsa, o_moba, wa, wm = mix
        in_specs += [rows(o_nsa.shape[1]), rows(o_moba.shape[1]), _resident(wa.shape), _resident(wm.shape)]
        args += [o_nsa, o_moba, wa, wm]
    in_specs += [_resident((1, D_MODEL)), _resident((D_MODEL, D_FF), layer),
                 _resident((D_MODEL, D_FF), layer), _resident((D_FF, D_MODEL), layer)]
    args += [g.reshape(1, D_MODEL), wg, wu, wd]
    if final_norm:
        in_specs.append(_resident((1, D_MODEL)))
        args.append(g_final.reshape(1, D_MODEL))
    return pl.pallas_call(
        functools.partial(_ffn_kernel, mix=mix is not None, final_norm=final_norm),
        grid=(n // FFN_ROWS,),
        in_specs=in_specs,
        out_specs=rows(D_MODEL),
        out_shape=jax.ShapeDtypeStruct((n, D_MODEL), F32),
        scratch_shapes=[pltpu.VMEM((FFN_ROWS, D_FF), BF16)],
        compiler_params=_params(("parallel",)),
        name="ffn" + ("_mix" if mix is not None else "") + ("_final" if final_norm else ""),
    )(*args)


def _rope_tables(c, sl, sr, mode):
    if mode == "both":
        return c, sl, sr
    lane = lax.broadcasted_iota(jnp.int32, c.shape, 1)
    keep = lane < HEAD_DIM if mode == "lo" else lane >= HEAD_DIM
    return jnp.where(keep, c, 1.0), jnp.where(keep, sl, 0.0), jnp.where(keep, sr, 0.0)


def _rope(v, tabs):
    c, sl, sr = tabs
    return v * c + pltpu.roll(v, LANES - ROT_DIM // 2, 1) * sl + pltpu.roll(v, ROT_DIM // 2, 1) * sr


def _inproj_kernel(x_ref, g_ref, w_ref, c_ref, sl_ref, sr_ref,
                   qn_ref, gt_ref, sw_ref, cmp_ref, mq_ref, mkv_ref, km_ref):
    xb = _rms(x_ref[...], g_ref[...]).astype(BF16)
    base = (c_ref[...], sl_ref[...], sr_ref[...])
    tabs = {m: _rope_tables(*base, m) for m in ("both", "lo", "hi")}

    qn_ref[...] = _dot(xb, w_ref[:, SEC_Q:SEC_GATE])
    gt_ref[...] = _dot(xb, w_ref[:, SEC_GATE:SEC_SW])
    cmp_ref[...] = _dot(xb, w_ref[:, SEC_CMP:SEC_MQ])

    sw = _dot(xb, w_ref[:, SEC_SW:SEC_CMP])
    for i in range(4):
        sl = slice(i * LANES, (i + 1) * LANES)
        sw_ref[:, sl] = _rope(sw[:, sl], tabs["lo"]).astype(BF16)

    mq = _dot(xb, w_ref[:, SEC_MQ:SEC_MKV])
    for i in range(4):
        sl = slice(i * LANES, (i + 1) * LANES)
        mq_ref[:, sl] = (_rope(mq[:, sl], tabs["both"]) * SCALE).astype(BF16)

    mkv = _dot(xb, w_ref[:, SEC_MKV:IN_COLS_P])
    nblk = PROJ_ROWS // MOBA_BLOCK
    for h in range(MOBA_HEADS):
        sl = slice(h * LANES, (h + 1) * LANES)
        r = _rope(mkv[:, sl], tabs["lo" if h % 2 == 0 else "hi"])
        mkv_ref[:, sl] = r.astype(BF16)
        km_ref[:, 0, sl] = jnp.mean(r.reshape(nblk, MOBA_BLOCK, LANES), axis=1)


def _inproj(x2, g, w_p, layer, tabs, seq):
    n = x2.shape[0]
    tiles_per_seq = seq // PROJ_ROWS
    nblk = PROJ_ROWS // MOBA_BLOCK

    def rows(width):
        return pl.BlockSpec((PROJ_ROWS, width), lambda i: (i, 0))

    tab = pl.BlockSpec((PROJ_ROWS, LANES), lambda i: (i % tiles_per_seq, 0))
    out_shapes = (
        jax.ShapeDtypeStruct((n, 512), F32),
        jax.ShapeDtypeStruct((n, 256), F32),
        jax.ShapeDtypeStruct((n, 512), BF16),
        jax.ShapeDtypeStruct((n, 256), F32),
        jax.ShapeDtypeStruct((n, 512), BF16),
        jax.ShapeDtypeStruct((n, 1024), BF16),
        jax.ShapeDtypeStruct((n // MOBA_BLOCK, 1, 1024), F32),
    )
    out_specs = (rows(512), rows(256), rows(512), rows(256), rows(512), rows(1024),
                 pl.BlockSpec((nblk, 1, 1024), lambda i: (i, 0, 0)))
    return pl.pallas_call(
        _inproj_kernel,
        grid=(n // PROJ_ROWS,),
        in_specs=[rows(D_MODEL), _resident((1, D_MODEL)), _resident((D_MODEL, IN_COLS_P), layer),
                  tab, tab, tab],
        out_specs=out_specs,
        out_shape=out_shapes,
        compiler_params=_params(("parallel",)),
        name="inproj",
    )(x2, g.reshape(1, D_MODEL), w_p, *tabs)


def _compress_kernel(x_ref, pos_ref, wa_ref, wb_ref, w2_ref, o_ref):
    n_pad = x_ref.shape[1] // CMP_STRIDE
    width = 2 * CMP_HIDDEN
    ya = jnp.zeros((n_pad, width), F32)
    zb = jnp.zeros((n_pad, width), F32)
    for l in range(CMP_STRIDE):
        h = x_ref[0, pl.ds(l, n_pad, stride=CMP_STRIDE), :]
        ya = ya + _dot((h + pos_ref[0, l:l + 1, :]).astype(BF16), wa_ref[l])
        zb = zb + _dot((h + pos_ref[1, l:l + 1, :]).astype(BF16), wb_ref[l])
    pre = ya + pltpu.roll(zb, n_pad - 1, 0)
    out = _dot(jax.nn.gelu(pre).astype(BF16), w2_ref[...])
    row = lax.broadcasted_iota(jnp.int32, out.shape, 0)
    o_ref[0, 0] = jnp.where(row < n_pad - 1, out, 0.0).astype(BF16)


def _compress(cmp_in, pos, wa, wb, w2):
    b, seq = cmp_in.shape[:2]
    g = NSA_KV_HEADS
    n_pad = seq // CMP_STRIDE
    return pl.pallas_call(
        _compress_kernel,
        grid=(b, g),
        in_specs=[pl.BlockSpec((1, seq, LANES), lambda i, j: (i, 0, j)),
                  _resident(pos.shape), _resident(wa.shape), _resident(wb.shape), _resident(w2.shape)],
        out_specs=pl.BlockSpec((1, 1, n_pad, LANES), lambda i, j: (i, j, 0, 0)),
        out_shape=jax.ShapeDtypeStruct((b, g, n_pad, LANES), BF16),
        compiler_params=_params(("parallel", "parallel")),
        name="compress",
    )(cmp_in, pos, wa, wb, w2)


def _rank_below(s_t, n_rows, limit):
    t = s_t.shape[1]
    n_grp = n_rows // SUBLANES
    grp = [s_t[g * SUBLANES:(g + 1) * SUBLANES] for g in range(n_grp)]
    cnt = [jnp.zeros((SUBLANES, t), F32) for _ in range(n_grp)]
    sub = lax.broadcasted_iota(jnp.int32, (SUBLANES, t), 0)
    for i in range(n_rows):
        ri = s_t[i:i + 1, :]
        for g in range(n_grp):
            if g > i // SUBLANES:
                beats = ri >= grp[g]
            elif g < i // SUBLANES:
                beats = ri > grp[g]
            else:
                beats = (ri > grp[g]) | ((ri == grp[g]) & (sub > i % SUBLANES))
            cnt[g] = cnt[g] + jnp.where(beats, 1.0, 0.0)
    return jnp.concatenate(cnt, axis=0) < limit


def _block_bias(sel_t, t):
    pad = jnp.zeros((LANES - sel_t.shape[0], t), F32)
    sel = jnp.concatenate([sel_t, pad], axis=0).T
    return ((sel - 1.0) * MASK_BIAS).astype(BF16)


def _with_ones(kv):
    return jnp.concatenate([kv, jnp.ones(kv.shape, kv.dtype)], axis=1)


def _softmax_step(s, kv, m_ref, acc_ref):
    m_old = m_ref[...]
    m_new = jnp.maximum(m_old, jnp.broadcast_to(jnp.max(s, axis=-1, keepdims=True), m_old.shape))
    alpha = jnp.exp(m_old - m_new)
    p = jnp.exp(s - jnp.concatenate([m_new] * (s.shape[1] // LANES), axis=1)).astype(BF16)
    acc_ref[...] = jnp.concatenate([alpha, alpha], axis=1) * acc_ref[...] + _dot(p, _with_ones(kv))
    m_ref[...] = m_new


def _attend(q_ext, keys, n_past, causal, m_ref, acc_ref, s_a, s_b):
    def put_scores(c, s_ref):
        kvs, onehot = keys(c)
        for h, (q, kv) in enumerate(zip(q_ext, kvs)):
            s_ref[h] = _dot_nt(q, jnp.concatenate([kv, onehot], axis=1))

    def softmax(c, s_ref, mask=None):
        for h, kv in enumerate(keys(c)[0]):
            s = s_ref[h] if mask is None else jnp.where(mask, s_ref[h], -MASK_BIAS)
            _softmax_step(s, kv, m_ref.at[h], acc_ref.at[h])

    put_scores(0, s_a)

    def body(i, carry):
        c = 2 * i
        put_scores(c + 1, s_b)
        softmax(c, s_a)
        put_scores(c + 2, s_a)
        softmax(c + 1, s_b)
        return carry

    lax.fori_loop(0, n_past // 2, body, 0)

    @pl.when(n_past % 2 == 1)
    def _():
        put_scores(n_past, s_b)
        softmax(n_past - 1, s_a)
        softmax(n_past, s_b, causal)

    @pl.when(n_past % 2 == 0)
    def _():
        softmax(n_past, s_a, causal)


def _softmax_finish(acc):
    return acc[:, :LANES] / jnp.maximum(acc[:, LANES:], 1.0)


def _nsa_kernel(qn_ref, gt_ref, c_ref, sl_ref, sr_ref, kvc_ref, sw_ref, ov_ref, oh_ref,
                o_ref, m_sc, acc_sc, sa_sc, sb_sc):
    tq, r_heads, groups = NSA_TQ, NSA_GROUP, NSA_KV_HEADS
    rows = r_heads * tq
    q0 = pl.program_id(1) * tq
    tabs = (c_ref[...], sl_ref[...], sr_ref[...])
    lo = lax.broadcasted_iota(jnp.int32, (tq, LANES), 1) < HEAD_DIM
    n_slc = sw_ref.shape[1] // SLC_BLOCK
    blk_shift = SLC_BLOCK.bit_length() - 1

    def slab(i):
        return slice(i * LANES, (i + 1) * LANES)

    def head(o, r):
        return o[r * tq:(r + 1) * tq]

    def front(g):
        def stack(rotate):
            parts = []
            for r in range(r_heads):
                pair = qn_ref[0, :, slab(2 * g + r // 2)]
                if rotate:
                    pair = _rope(pair, tabs)
                if r % 2:
                    pair = pltpu.roll(pair, HEAD_DIM, 1)
                parts.append(jnp.where(lo, pair * SCALE, 0.0).astype(BF16))
            return jnp.concatenate(parts, axis=0)

        q_plain = stack(False)
        q_rot = stack(True)

        w0 = pl.multiple_of(jnp.maximum(q0 - WINDOW, 0), tq)
        kvw = sw_ref[0, pl.ds(w0, WINDOW + tq), slab(2 * g + 1)]
        d = (q0 - w0) + (lax.broadcasted_iota(jnp.int32, (tq, WINDOW + tq), 0)
                         - lax.broadcasted_iota(jnp.int32, (tq, WINDOW + tq), 1))
        valid_w = ((d >= 0) & (d < WINDOW))[None]
        s = jnp.where(valid_w, _dot_nt(q_rot, kvw).reshape(r_heads, tq, WINDOW + tq), NEG)
        p = jnp.exp(s - jnp.max(s, axis=-1, keepdims=True)).reshape(rows, WINDOW + tq).astype(BF16)
        o_win = _softmax_finish(_dot(p, _with_ones(kvw)))

        kvc = kvc_ref[0, g]
        n_pad = kvc.shape[0]
        n_idx = lax.broadcasted_iota(jnp.int32, (tq, n_pad), 1)
        t_idx = q0 + lax.broadcasted_iota(jnp.int32, (tq, n_pad), 0)
        valid_c = (n_idx * CMP_STRIDE + CMP_LEN - 1 <= t_idx)[None]
        s = jnp.where(valid_c, _dot_nt(q_plain, kvc).reshape(r_heads, tq, n_pad), NEG)
        p = jnp.where(valid_c, jnp.exp(s - jnp.max(s, axis=-1, keepdims=True)), 0.0)
        p = p / jnp.maximum(jnp.sum(p, axis=-1, keepdims=True), 1.0)
        o_cmp = _dot(p.reshape(rows, n_pad).astype(BF16), kvc)

        p_sum = p[0] + p[1] + p[2] + p[3]
        ov = ov_ref[...]
        hi = p_sum.astype(BF16)
        r1 = p_sum - hi.astype(F32)
        mid = r1.astype(BF16)
        low = (r1 - mid.astype(F32)).astype(BF16)
        imp = _dot(hi, ov) + _dot(mid, ov) + _dot(low, ov)

        j_idx = lax.broadcasted_iota(jnp.int32, (tq, LANES), 1)
        t_row = q0 + lax.broadcasted_iota(jnp.int32, (tq, LANES), 0)
        qblk = jnp.right_shift(t_row, blk_shift)
        forced = (j_idx == 0) | (j_idx == qblk) | (j_idx == qblk - 1)
        visible = j_idx * SLC_BLOCK <= t_row
        score = jnp.where(visible, jnp.where(forced, FORCE_BONUS, imp), NEG)
        sel_t = _rank_below(score.T[:n_slc], n_slc, SLC_TOPN).astype(F32)
        bias = _block_bias(sel_t, tq)

        gate = jax.nn.sigmoid(gt_ref[0, :, slab(g)])

        def gate_of(r, branch):
            return jnp.broadcast_to(gate[:, 3 * r + branch:3 * r + branch + 1], (tq, LANES))

        partial = [gate_of(r, 0) * head(o_cmp, r) + gate_of(r, 2) * head(o_win, r) for r in range(r_heads)]
        gate_slc = [gate_of(r, 1) for r in range(r_heads)]
        q_ext = jnp.concatenate([q_rot, jnp.concatenate([bias] * r_heads, axis=0)], axis=1)
        return q_ext, partial, gate_slc

    fronts = [front(g) for g in range(groups)]

    m_sc[...] = jnp.full(m_sc.shape, NEG, F32)
    acc_sc[...] = jnp.zeros(acc_sc.shape, F32)

    def slc_keys(c):
        ks = pl.ds(pl.multiple_of(c * NSA_KC, NSA_KC), NSA_KC)
        return tuple(sw_ref[0, ks, slab(2 * g)] for g in range(groups)), oh_ref[ks, :]

    c_last = q0 // NSA_KC
    kpos = c_last * NSA_KC + lax.broadcasted_iota(jnp.int32, (rows, NSA_KC), 1)
    tpos = q0 + (lax.broadcasted_iota(jnp.int32, (rows, NSA_KC), 0) & (tq - 1))
    _attend([f[0] for f in fronts], slc_keys, c_last, kpos <= tpos, m_sc, acc_sc, sa_sc, sb_sc)

    for g, (_, partial, gate_slc) in enumerate(fronts):
        o_slc = _softmax_finish(acc_sc[g])
        outs = [partial[r] + gate_slc[r] * head(o_slc, r) for r in range(r_heads)]
        for pr in range(r_heads // 2):
            pair = jnp.where(lo, pltpu.roll(outs[2 * pr], HEAD_DIM, 1), outs[2 * pr + 1])
            o_ref[0, :, slab(2 * g + pr)] = pair.astype(BF16)


def _nsa(qn, gt, tabs, kvc, sw, ov):
    b, seq = qn.shape[:2]
    groups = NSA_KV_HEADS
    tq = NSA_TQ
    rows = NSA_GROUP * tq
    n_pad = kvc.shape[2]
    tab = pl.BlockSpec((tq, LANES), lambda i, k: (k, 0))
    return pl.pallas_call(
        _nsa_kernel,
        grid=(b, seq // tq),
        in_specs=[
            pl.BlockSpec((1, tq, NSA_Q_COLS), lambda i, k: (i, k, 0)),
            pl.BlockSpec((1, tq, groups * LANES), lambda i, k: (i, k, 0)),
            tab, tab, tab,
            pl.BlockSpec((1, groups, n_pad, LANES), lambda i, k: (i, 0, 0, 0)),
            pl.BlockSpec((1, seq, 2 * groups * LANES), lambda i, k: (i, 0, 0)),
            pl.BlockSpec((n_pad, LANES), lambda i, k: (0, 0)),
            pl.BlockSpec((seq, LANES), lambda i, k: (0, 0)),
        ],
        out_specs=pl.BlockSpec((1, tq, NSA_Q_COLS), lambda i, k: (i, k, 0)),
        out_shape=jax.ShapeDtypeStruct((b, seq, NSA_Q_COLS), BF16),
        scratch_shapes=[pltpu.VMEM((groups, rows, LANES), F32), pltpu.VMEM((groups, rows, 2 * LANES), F32),
                        pltpu.VMEM((groups, rows, NSA_KC), F32), pltpu.VMEM((groups, rows, NSA_KC), F32)],
        compiler_params=_params(("parallel", "arbitrary")),
        name="nsa",
    )(qn, gt, *tabs, kvc, sw, ov, _block_onehot(seq, SLC_BLOCK))


def _moba_kernel(q_ref, kv_ref, km_ref, oh_ref, o_ref, m_sc, acc_sc, sa_sc, sb_sc):
    tq = MOBA_TQ
    heads = MOBA_HEADS_PER_STEP
    q0 = pl.multiple_of(pl.program_id(2) * tq, tq)
    lo = lax.broadcasted_iota(jnp.int32, (tq, LANES), 1) < HEAD_DIM
    n_blk = km_ref.shape[1]
    blk_shift = MOBA_BLOCK.bit_length() - 1
    blk = lax.broadcasted_iota(jnp.int32, (n_blk, tq), 0)
    qblk = jnp.right_shift(q0 + lax.broadcasted_iota(jnp.int32, (n_blk, tq), 1), blk_shift)
    past = blk < qblk

    def slab(h):
        return slice(h * LANES, (h + 1) * LANES)

    q_ext = []
    for h in range(heads):
        q = jnp.where(lo if h % 2 == 0 else ~lo, q_ref[0, :, slab(h // 2)], jnp.zeros((), BF16))
        gate_t = _dot_nt(km_ref[0, :, slab(h)].astype(BF16), q)
        top = _rank_below(jnp.where(past, gate_t, NEG), n_blk, MOBA_TOPK)
        bias = _block_bias(((top & past) | (blk == qblk)).astype(F32), tq)
        q_ext.append(jnp.concatenate([q, bias], axis=1))

    m_sc[...] = jnp.full(m_sc.shape, NEG, F32)
    acc_sc[...] = jnp.zeros(acc_sc.shape, F32)

    def keys(c):
        ks = pl.ds(pl.multiple_of(c * tq, tq), tq)
        return tuple(kv_ref[0, ks, slab(h)] for h in range(heads)), oh_ref[ks, :]

    causal = lax.broadcasted_iota(jnp.int32, (tq, tq), 1) <= lax.broadcasted_iota(jnp.int32, (tq, tq), 0)
    _attend(q_ext, keys, pl.program_id(2), causal, m_sc, acc_sc, sa_sc, sb_sc)

    for pr in range(heads // 2):
        even = _softmax_finish(acc_sc[2 * pr])
        odd = _softmax_finish(acc_sc[2 * pr + 1])
        o_ref[0, :, slab(pr)] = jnp.where(lo, odd, even).astype(BF16)


def _moba(mq, mkv, km):
    b, seq = mq.shape[:2]
    tq = MOBA_TQ
    n_blk = seq // MOBA_BLOCK
    heads = MOBA_HEADS_PER_STEP
    return pl.pallas_call(
        _moba_kernel,
        grid=(b, MOBA_HEADS // heads, seq // tq),
        in_specs=[
            pl.BlockSpec((1, tq, heads // 2 * LANES), lambda i, j, k: (i, k, j)),
            pl.BlockSpec((1, seq, heads * LANES), lambda i, j, k: (i, 0, j)),
            pl.BlockSpec((1, n_blk, heads * LANES), lambda i, j, k: (i, 0, j)),
            pl.BlockSpec((seq, LANES), lambda i, j, k: (0, 0)),
        ],
        out_specs=pl.BlockSpec((1, tq, heads // 2 * LANES), lambda i, j, k: (i, k, j)),
        out_shape=jax.ShapeDtypeStruct((b, seq, MOBA_HEADS * HEAD_DIM), BF16),
        scratch_shapes=[pltpu.VMEM((heads, tq, LANES), F32), pltpu.VMEM((heads, tq, 2 * LANES), F32),
                        pltpu.VMEM((heads, tq, tq), F32), pltpu.VMEM((heads, tq, tq), F32)],
        compiler_params=_params(("parallel", "parallel", "arbitrary")),
        name="moba",
    )(mq, mkv, km, _block_onehot(seq, MOBA_BLOCK))


def _in_perm():
    d = HEAD_DIM
    perm = np.full((IN_COLS_P,), -1, np.int64)
    perm[SEC_Q:SEC_Q + NSA_Q_COLS] = np.arange(NSA_Q_COLS)
    kv0 = NSA_Q_COLS
    gate0 = kv0 + NSA_KV_COLS
    moba0 = gate0 + NSA_GATE_COLS
    per_group = NSA_GROUP * 3
    for g in range(NSA_KV_HEADS):
        perm[SEC_GATE + g * LANES:SEC_GATE + g * LANES + per_group] = gate0 + g * per_group + np.arange(per_group)

    def kv_cols(which, g):
        return kv0 + (which * NSA_KV_HEADS + g) * d + np.arange(d)

    for g in range(NSA_KV_HEADS):
        base = SEC_SW + g * 2 * LANES
        perm[base:base + d] = kv_cols(2, g)
        perm[base + d:base + 2 * d] = kv_cols(3, g)
        perm[base + 2 * d:base + 3 * d] = kv_cols(4, g)
        perm[base + 3 * d:base + 4 * d] = kv_cols(5, g)
        base = SEC_CMP + g * LANES
        perm[base:base + d] = kv_cols(0, g)
        perm[base + d:base + 2 * d] = kv_cols(1, g)
    perm[SEC_MQ:SEC_MQ + MOBA_HEADS * d] = moba0 + np.arange(MOBA_HEADS * d)
    for h in range(MOBA_HEADS):
        k_cols = moba0 + (MOBA_HEADS + h) * d + np.arange(d)
        v_cols = moba0 + (2 * MOBA_HEADS + h) * d + np.arange(d)
        base = SEC_MKV + h * LANES
        first, second = (k_cols, v_cols) if h % 2 == 0 else (v_cols, k_cols)
        perm[base:base + d] = first
        perm[base + d:base + 2 * d] = second
    return perm


def _take_runs(w, idx, axis):
    parts, i = [], 0
    while i < len(idx):
        j = i + 1
        if idx[i] < 0:
            while j < len(idx) and idx[j] < 0:
                j += 1
            shape = list(w.shape)
            shape[axis] = j - i
            parts.append(jnp.zeros(shape, w.dtype))
        else:
            while j < len(idx) and idx[j] == idx[j - 1] + 1:
                j += 1
            parts.append(lax.slice_in_dim(w, int(idx[i]), int(idx[j - 1]) + 1, axis=axis))
        i = j
    return jnp.concatenate(parts, axis=axis)


def _moba_out_rows():
    d = HEAD_DIM
    rows = []
    for pr in range(MOBA_HEADS // 2):
        rows.append(NSA_Q_COLS + (2 * pr + 1) * d + np.arange(d))
        rows.append(NSA_Q_COLS + (2 * pr) * d + np.arange(d))
    return np.concatenate(rows)


def _overlap_table(seq):
    n_pad = seq // CMP_STRIDE
    n = np.arange(n_pad)[:, None]
    j = np.arange(LANES)[None, :]
    starts = n * CMP_STRIDE
    sb = j * SLC_BLOCK
    ov = (starts < sb + SLC_BLOCK) & (starts + CMP_LEN > sb) & (n < n_pad - 1) & (j < seq // SLC_BLOCK)
    return ov.astype(np.float32)


def _block_onehot(seq, block):
    onehot = np.arange(seq)[:, None] // block == np.arange(LANES)[None, :]
    return jnp.asarray(onehot.astype(np.float32), BF16)


def _rope_lane_tables(seq):
    half = ROT_DIM // 2
    inv_freq = jnp.power(ROPE_THETA, -(jnp.arange(0, ROT_DIM, 2, dtype=F32) / ROT_DIM))
    ang = jnp.arange(seq, dtype=F32)[:, None] * inv_freq[None, :]
    cos, sin = jnp.cos(ang), jnp.sin(ang)
    rest = HEAD_DIM - ROT_DIM
    zeros_h = jnp.zeros((seq, half), F32)
    c = jnp.concatenate([cos, cos, jnp.ones((seq, rest), F32)], axis=1)
    sl = jnp.concatenate([-sin, zeros_h, jnp.zeros((seq, rest), F32)], axis=1)
    sr = jnp.concatenate([zeros_h, sin, jnp.zeros((seq, rest), F32)], axis=1)
    return tuple(jnp.tile(t, (1, LANES // HEAD_DIM)) for t in (c, sl, sr))


def _mixer(x2, b, seq, g_mix, w_p, layer, pos_ck, w_ck1, w_ck2, pos_cv, w_cv1, w_cv2, tabs, ov):
    qn, gt, sw, cmp_in, mq, mkv, km = _inproj(x2, g_mix, w_p, layer, tabs, seq)

    def halves(w_k, w_v):
        wk = w_k.reshape(2, CMP_STRIDE, HEAD_DIM, CMP_HIDDEN)
        wv = w_v.reshape(2, CMP_STRIDE, HEAD_DIM, CMP_HIDDEN)
        z = jnp.zeros_like(wk)
        return jnp.concatenate([jnp.concatenate([wk, z], axis=3), jnp.concatenate([z, wv], axis=3)],
                               axis=2).astype(BF16)

    w1 = halves(w_ck1, w_cv1)
    pos = jnp.concatenate([pos_ck, pos_cv], axis=1).reshape(2, CMP_STRIDE, LANES)
    z2 = jnp.zeros_like(w_ck2)
    w2 = jnp.concatenate([jnp.concatenate([w_ck2, z2], axis=1),
                          jnp.concatenate([z2, w_cv2], axis=1)], axis=0).astype(BF16)
    kvc = _compress(cmp_in.reshape(b, seq, -1), pos, w1[0], w1[1], w2)

    o_nsa = _nsa(qn.reshape(b, seq, -1), gt.reshape(b, seq, -1), tabs, kvc, sw.reshape(b, seq, -1), ov)
    o_moba = _moba(mq.reshape(b, seq, -1), mkv.reshape(b, seq, -1), km.reshape(b, seq // MOBA_BLOCK, -1))

    n = b * seq
    return o_nsa.reshape(n, -1), o_moba.reshape(n, -1)


def kernel(x, norm_ffn1, w_ffn1_gate, w_ffn1_up, w_ffn1_down, norm_mix, w_in, pos_ck, w_ck1, w_ck2,
           pos_cv, w_cv1, w_cv2, w_out, norm_ffn2, w_ffn2_gate, w_ffn2_up, w_ffn2_down, norm_final):
    b, seq, d = x.shape
    assert d == D_MODEL and seq % PROJ_ROWS == 0 and SLC_TOPN <= seq // SLC_BLOCK <= LANES
    depth = norm_ffn1.shape[0]
    tabs = _rope_lane_tables(seq)
    ov = jnp.asarray(_overlap_table(seq), BF16)
    x2 = x.reshape(b * seq, d)
    w_p = _take_runs(w_in.astype(BF16), _in_perm(), axis=2)
    w_out_b = w_out.astype(BF16)
    wa = w_out_b[:, :NSA_Q_COLS]
    wm = _take_runs(w_out_b, _moba_out_rows(), axis=1)
    wg1, wu1, wd1, wg2, wu2, wd2 = (w.astype(BF16) for w in (w_ffn1_gate, w_ffn1_up, w_ffn1_down,
                                                            w_ffn2_gate, w_ffn2_up, w_ffn2_down))
    for l in range(depth):
        x2 = _ffn(x2, norm_ffn1[l], wg1, wu1, wd1, l)
        o_nsa, o_moba = _mixer(x2, b, seq, norm_mix[l], w_p, l, pos_ck[l], w_ck1[l], w_ck2[l],
                               pos_cv[l], w_cv1[l], w_cv2[l], tabs, ov)
        x2 = _ffn(x2, norm_ffn2[l], wg2, wu2, wd2, l, mix=(o_nsa, o_moba, wa[l], wm[l]),
                  g_final=norm_final if l == depth - 1 else None)
    return x2.reshape(b, seq, d)
```

```python
import functools

import numpy as np
import jax
import jax.numpy as jnp
from jax import lax
from jax.experimental import pallas as pl
from jax.experimental.pallas import tpu as pltpu

D_MODEL = 1024
HEAD_DIM = 64
NSA_HEADS = 8
NSA_KV_HEADS = 2
NSA_GROUP = NSA_HEADS // NSA_KV_HEADS
MOBA_HEADS = 8
ROT_DIM = HEAD_DIM // 4
ROPE_THETA = 500000.0
CMP_LEN = 32
CMP_STRIDE = 16
CMP_HIDDEN = 256
SLC_BLOCK = 64
SLC_TOPN = 16
WINDOW = 512
MOBA_BLOCK = 256
MOBA_TOPK = 3
D_FF = 2816
EPS = 1e-6
NEG = -1e30
FORCE_BONUS = 1e4
SCALE = HEAD_DIM ** -0.5

NSA_Q_COLS = NSA_HEADS * HEAD_DIM
NSA_KV_COLS = 3 * 2 * NSA_KV_HEADS * HEAD_DIM
NSA_GATE_COLS = NSA_HEADS * 3
MOBA_COLS = 3 * MOBA_HEADS * HEAD_DIM

LANES = 128
SUBLANES = 8
VMEM_LIMIT = 56 * 1024 * 1024

SEC_Q = 0
SEC_GATE = 512
SEC_SW = 768
SEC_CMP = 1280
SEC_MQ = 1536
SEC_MKV = 2048
IN_COLS_P = 3072

F32 = jnp.float32
BF16 = jnp.bfloat16

FFN_ROWS = 512
FFN_CHUNK = 256
PROJ_ROWS = 512
NSA_TQ = 256
NSA_KC = 512
MOBA_HEADS_PER_STEP = 4
MOBA_TQ = 2 * MOBA_BLOCK
MASK_BIAS = 2.0 ** 100


def _dot(a, b):
    return jnp.dot(a, b, preferred_element_type=F32)


def _dot_nt(a, b):
    return lax.dot_general(a, b, (((1,), (1,)), ((), ())), preferred_element_type=F32)


def _rms(x, g):
    return x * lax.rsqrt(jnp.mean(x * x, axis=-1, keepdims=True) + EPS) * g


def _params(sem):
    return pltpu.CompilerParams(dimension_semantics=sem, vmem_limit_bytes=VMEM_LIMIT)


def _resident(shape, layer=None):
    nd = len(shape)
    if layer is None:
        return pl.BlockSpec(shape, lambda *_: (0,) * nd, pipeline_mode=pl.Buffered(1))
    return pl.BlockSpec((None,) + tuple(shape), lambda *_: (layer,) + (0,) * nd,
                        pipeline_mode=pl.Buffered(1))


def _ffn_kernel(*refs, mix, final_norm):
    refs = list(refs)
    x_ref = refs.pop(0)
    if mix:
        a_ref, mo_ref, wa_ref, wm_ref = (refs.pop(0) for _ in range(4))
    g_ref, wg_ref, wu_ref, wd_ref = (refs.pop(0) for _ in range(4))
    gf_ref = refs.pop(0) if final_norm else None
    o_ref, h_ref = refs
    x = x_ref[...]
    if mix:
        x = x + _dot(a_ref[...], wa_ref[...]) + _dot(mo_ref[...], wm_ref[...])
    xb = _rms(x, g_ref[...]).astype(BF16)
    for c in range(D_FF // FFN_CHUNK):
        sl = slice(c * FFN_CHUNK, (c + 1) * FFN_CHUNK)
        gate = _dot(xb, wg_ref[:, sl])
        up = _dot(xb, wu_ref[:, sl])
        h_ref[:, sl] = (jax.nn.silu(gate) * up).astype(BF16)
    y = x + 0.5 * _dot(h_ref[...], wd_ref[...])
    if final_norm:
        y = _rms(y, gf_ref[...])
    o_ref[...] = y


def _ffn(x2, g, wg, wu, wd, layer, mix=None, g_final=None):
    n = x2.shape[0]
    final_norm = g_final is not None

    def rows(width):
        return pl.BlockSpec((FFN_ROWS, width), lambda i: (i, 0))

    in_specs, args = [rows(D_MODEL)], [x2]
    if mix is not None:
        o_nsa, o_moba, wa, wm = mix
        in_specs += [rows(o_nsa.shape[1]), rows(o_moba.shape[1]), _resident(wa.shape), _resident(wm.shape)]
        args += [o_nsa, o_moba, wa, wm]
    in_specs += [_resident((1, D_MODEL)), _resident((D_MODEL, D_FF), layer),
                 _resident((D_MODEL, D_FF), layer), _resident((D_FF, D_MODEL), layer)]
    args += [g.reshape(1, D_MODEL), wg, wu, wd]
    if final_norm:
        in_specs.append(_resident((1, D_MODEL)))
        args.append(g_final.reshape(1, D_MODEL))
    return pl.pallas_call(
        functools.partial(_ffn_kernel, mix=mix is not None, final_norm=final_norm),
        grid=(n // FFN_ROWS,),
        in_specs=in_specs,
        out_specs=rows(D_MODEL),
        out_shape=jax.ShapeDtypeStruct((n, D_MODEL), F32),
        scratch_shapes=[pltpu.VMEM((FFN_ROWS, D_FF), BF16)],
        compiler_params=_params(("parallel",)),
        name="ffn" + ("_mix" if mix is not None else "") + ("_final" if final_norm else ""),
    )(*args)


def _rope_tables(c, sl, sr, mode):
    if mode == "both":
        return c, sl, sr
    lane = lax.broadcasted_iota(jnp.int32, c.shape, 1)
    keep = lane < HEAD_DIM if mode == "lo" else lane >= HEAD_DIM
    return jnp.where(keep, c, 1.0), jnp.where(keep, sl, 0.0), jnp.where(keep, sr, 0.0)


def _rope(v, tabs):
    c, sl, sr = tabs
    return v * c + pltpu.roll(v, LANES - ROT_DIM // 2, 1) * sl + pltpu.roll(v, ROT_DIM // 2, 1) * sr


def _inproj_kernel(x_ref, g_ref, w_ref, c_ref, sl_ref, sr_ref,
                   qn_ref, gt_ref, sw_ref, cmp_ref, mq_ref, mkv_ref, km_ref):
    xb = _rms(x_ref[...], g_ref[...]).astype(BF16)
    base = (c_ref[...], sl_ref[...], sr_ref[...])
    tabs = {m: _rope_tables(*base, m) for m in ("both", "lo", "hi")}

    qn_ref[...] = _dot(xb, w_ref[:, SEC_Q:SEC_GATE])
    gt_ref[...] = _dot(xb, w_ref[:, SEC_GATE:SEC_SW])
    cmp_ref[...] = _dot(xb, w_ref[:, SEC_CMP:SEC_MQ])

    sw = _dot(xb, w_ref[:, SEC_SW:SEC_CMP])
    for i in range(4):
        sl = slice(i * LANES, (i + 1) * LANES)
        sw_ref[:, sl] = _rope(sw[:, sl], tabs["lo"]).astype(BF16)

    mq = _dot(xb, w_ref[:, SEC_MQ:SEC_MKV])
    for i in range(4):
        sl = slice(i * LANES, (i + 1) * LANES)
        mq_ref[:, sl] = (_rope(mq[:, sl], tabs["both"]) * SCALE).astype(BF16)

    mkv = _dot(xb, w_ref[:, SEC_MKV:IN_COLS_P])
    nblk = PROJ_ROWS // MOBA_BLOCK
    for h in range(MOBA_HEADS):
        sl = slice(h * LANES, (h + 1) * LANES)
        r = _rope(mkv[:, sl], tabs["lo" if h % 2 == 0 else "hi"])
        mkv_ref[:, sl] = r.astype(BF16)
        km_ref[:, 0, sl] = jnp.mean(r.reshape(nblk, MOBA_BLOCK, LANES), axis=1)


def _inproj(x2, g, w_p, layer, tabs, seq):
    n = x2.shape[0]
    tiles_per_seq = seq // PROJ_ROWS
    nblk = PROJ_ROWS // MOBA_BLOCK

    def rows(width):
        return pl.BlockSpec((PROJ_ROWS, width), lambda i: (i, 0))

    tab = pl.BlockSpec((PROJ_ROWS, LANES), lambda i: (i % tiles_per_seq, 0))
    out_shapes = (
        jax.ShapeDtypeStruct((n, 512), F32),
        jax.ShapeDtypeStruct((n, 256), F32),
        jax.ShapeDtypeStruct((n, 512), BF16),
        jax.ShapeDtypeStruct((n, 256), F32),
        jax.ShapeDtypeStruct((n, 512), BF16),
        jax.ShapeDtypeStruct((n, 1024), BF16),
        jax.ShapeDtypeStruct((n // MOBA_BLOCK, 1, 1024), F32),
    )
    out_specs = (rows(512), rows(256), rows(512), rows(256), rows(512), rows(1024),
                 pl.BlockSpec((nblk, 1, 1024), lambda i: (i, 0, 0)))
    return pl.pallas_call(
        _inproj_kernel,
        grid=(n // PROJ_ROWS,),
        in_specs=[rows(D_MODEL), _resident((1, D_MODEL)), _resident((D_MODEL, IN_COLS_P), layer),
                  tab, tab, tab],
        out_specs=out_specs,
        out_shape=out_shapes,
        compiler_params=_params(("parallel",)),
        name="inproj",
    )(x2, g.reshape(1, D_MODEL), w_p, *tabs)


def _compress_kernel(x_ref, pos_ref, wa_ref, wb_ref, w2_ref, o_ref):
    n_pad = x_ref.shape[1] // CMP_STRIDE
    width = 2 * CMP_HIDDEN
    ya = jnp.zeros((n_pad, width), F32)
    zb = jnp.zeros((n_pad, width), F32)
    for l in range(CMP_STRIDE):
        h = x_ref[0, pl.ds(l, n_pad, stride=CMP_STRIDE), :]
        ya = ya + _dot((h + pos_ref[0, l:l + 1, :]).astype(BF16), wa_ref[l])
        zb = zb + _dot((h + pos_ref[1, l:l + 1, :]).astype(BF16), wb_ref[l])
    pre = ya + pltpu.roll(zb, n_pad - 1, 0)
    out = _dot(jax.nn.gelu(pre).astype(BF16), w2_ref[...])
    row = lax.broadcasted_iota(jnp.int32, out.shape, 0)
    o_ref[0, 0] = jnp.where(row < n_pad - 1, out, 0.0).astype(BF16)


def _compress(cmp_in, pos, wa, wb, w2):
    b, seq = cmp_in.shape[:2]
    g = NSA_KV_HEADS
    n_pad = seq // CMP_STRIDE
    return pl.pallas_call(
        _compress_kernel,
        grid=(b, g),
        in_specs=[pl.BlockSpec((1, seq, LANES), lambda i, j: (i, 0, j)),
                  _resident(pos.shape), _resident(wa.shape), _resident(wb.shape), _resident(w2.shape)],
        out_specs=pl.BlockSpec((1, 1, n_pad, LANES), lambda i, j: (i, j, 0, 0)),
        out_shape=jax.ShapeDtypeStruct((b, g, n_pad, LANES), BF16),
        compiler_params=_params(("parallel", "parallel")),
        name="compress",
    )(cmp_in, pos, wa, wb, w2)


def _rank_below(s_t, n_rows, limit):
    t = s_t.shape[1]
    n_grp = n_rows // SUBLANES
    grp = [s_t[g * SUBLANES:(g + 1) * SUBLANES] for g in range(n_grp)]
    cnt = [jnp.zeros((SUBLANES, t), F32) for _ in range(n_grp)]
    sub = lax.broadcasted_iota(jnp.int32, (SUBLANES, t), 0)
    for i in range(n_rows):
        ri = s_t[i:i + 1, :]
        for g in range(n_grp):
            if g > i // SUBLANES:
                beats = ri >= grp[g]
            elif g < i // SUBLANES:
                beats = ri > grp[g]
            else:
                beats = (ri > grp[g]) | ((ri == grp[g]) & (sub > i % SUBLANES))
            cnt[g] = cnt[g] + jnp.where(beats, 1.0, 0.0)
    return jnp.concatenate(cnt, axis=0) < limit


def _block_bias(sel_t, t):
    pad = jnp.zeros((LANES - sel_t.shape[0], t), F32)
    sel = jnp.concatenate([sel_t, pad], axis=0).T
    return ((sel - 1.0) * MASK_BIAS).astype(BF16)


def _with_ones(kv):
    return jnp.concatenate([kv, jnp.ones(kv.shape, kv.dtype)], axis=1)


def _softmax_step(read_s, kv, m_ref, acc_ref):
    m_old = m_ref[...]
    m_new = jnp.maximum(m_old, jnp.broadcast_to(jnp.max(read_s(), axis=-1, keepdims=True), m_old.shape))
    m_ref[...] = m_new
    alpha = jnp.exp(m_old - m_new)
    s = read_s()
    p = jnp.exp(s - jnp.concatenate([m_new] * (s.shape[1] // LANES), axis=1)).astype(BF16)
    acc_ref[...] = jnp.concatenate([alpha, alpha], axis=1) * acc_ref[...] + _dot(p, _with_ones(kv))


def _attend(q_ext, keys, n_past, causal, m_ref, acc_ref, s_a, s_b):
    def put_scores(c, s_ref):
        kvs, onehot = keys(c)
        for h, (q, kv) in enumerate(zip(q_ext, kvs)):
            s_ref[h] = _dot_nt(q, jnp.concatenate([kv, onehot], axis=1))

    def softmax(c, s_ref, mask=None):
        for h, kv in enumerate(keys(c)[0]):
            if mask is None:
                read_s = functools.partial(lambda h: s_ref[h], h)
            else:
                read_s = functools.partial(lambda s: s, jnp.where(mask, s_ref[h], -MASK_BIAS))
            _softmax_step(read_s, kv, m_ref.at[h], acc_ref.at[h])

    put_scores(0, s_a)

    def body(i, carry):
        c = 2 * i
        put_scores(c + 1, s_b)
        softmax(c, s_a)
        put_scores(c + 2, s_a)
        softmax(c + 1, s_b)
        return carry

    lax.fori_loop(0, n_past // 2, body, 0)

    @pl.when(n_past % 2 == 1)
    def _():
        put_scores(n_past, s_b)
        softmax(n_past - 1, s_a)
        softmax(n_past, s_b, causal)

    @pl.when(n_past % 2 == 0)
    def _():
        softmax(n_past, s_a, causal)


def _softmax_finish(acc):
    return acc[:, :LANES] / jnp.maximum(acc[:, LANES:], 1.0)


def _nsa_kernel(qn_ref, gt_ref, c_ref, sl_ref, sr_ref, kvc_ref, sw_ref, ov_ref, oh_ref,
                o_ref, m_sc, acc_sc, sa_sc, sb_sc):
    tq, r_heads, groups = NSA_TQ, NSA_GROUP, NSA_KV_HEADS
    rows = r_heads * tq
    q0 = pl.program_id(1) * tq
    tabs = (c_ref[...], sl_ref[...], sr_ref[...])
    lo = lax.broadcasted_iota(jnp.int32, (tq, LANES), 1) < HEAD_DIM
    n_slc = sw_ref.shape[1] // SLC_BLOCK
    blk_shift = SLC_BLOCK.bit_length() - 1

    def slab(i):
        return slice(i * LANES, (i + 1) * LANES)

    def head(o, r):
        return o[r * tq:(r + 1) * tq]

    def front(g):
        def stack(rotate):
            parts = []
            for r in range(r_heads):
                pair = qn_ref[0, :, slab(2 * g + r // 2)]
                if rotate:
                    pair = _rope(pair, tabs)
                if r % 2:
                    pair = pltpu.roll(pair, HEAD_DIM, 1)
                parts.append(jnp.where(lo, pair * SCALE, 0.0).astype(BF16))
            return jnp.concatenate(parts, axis=0)

        q_plain = stack(False)
        q_rot = stack(True)

        w0 = pl.multiple_of(jnp.maximum(q0 - WINDOW, 0), tq)
        kvw = sw_ref[0, pl.ds(w0, WINDOW + tq), slab(2 * g + 1)]
        d = (q0 - w0) + (lax.broadcasted_iota(jnp.int32, (tq, WINDOW + tq), 0)
                         - lax.broadcasted_iota(jnp.int32, (tq, WINDOW + tq), 1))
        valid_w = ((d >= 0) & (d < WINDOW))[None]
        s = jnp.where(valid_w, _dot_nt(q_rot, kvw).reshape(r_heads, tq, WINDOW + tq), NEG)
        p = jnp.exp(s - jnp.max(s, axis=-1, keepdims=True)).reshape(rows, WINDOW + tq).astype(BF16)
        o_win = _softmax_finish(_dot(p, _with_ones(kvw)))

        kvc = kvc_ref[0, g]
        n_pad = kvc.shape[0]
        n_idx = lax.broadcasted_iota(jnp.int32, (tq, n_pad), 1)
        t_idx = q0 + lax.broadcasted_iota(jnp.int32, (tq, n_pad), 0)
        valid_c = (n_idx * CMP_STRIDE + CMP_LEN - 1 <= t_idx)[None]
        s = jnp.where(valid_c, _dot_nt(q_plain, kvc).reshape(r_heads, tq, n_pad), NEG)
        m = jnp.maximum(jnp.max(s, axis=-1, keepdims=True), NEG / 2)
        p = jnp.exp(s - m)
        p = p / jnp.maximum(jnp.sum(p, axis=-1, keepdims=True), 1.0)
        o_cmp = _dot(p.reshape(rows, n_pad).astype(BF16), kvc)

        p_sum = p[0] + p[1] + p[2] + p[3]
        ov = ov_ref[...]
        hi = p_sum.astype(BF16)
        r1 = p_sum - hi.astype(F32)
        mid = r1.astype(BF16)
        low = (r1 - mid.astype(F32)).astype(BF16)
        imp = _dot(hi, ov) + _dot(mid, ov) + _dot(low, ov)

        j_idx = lax.broadcasted_iota(jnp.int32, (tq, LANES), 1)
        t_row = q0 + lax.broadcasted_iota(jnp.int32, (tq, LANES), 0)
        qblk = jnp.right_shift(t_row, blk_shift)
        forced = (j_idx == 0) | (j_idx == qblk) | (j_idx == qblk - 1)
        visible = j_idx * SLC_BLOCK <= t_row
        score = jnp.where(visible, jnp.where(forced, FORCE_BONUS, imp), NEG)
        sel_t = _rank_below(score.T[:n_slc], n_slc, SLC_TOPN).astype(F32)
        bias = _block_bias(sel_t, tq)

        gate = jax.nn.sigmoid(gt_ref[0, :, slab(g)])

        def gate_of(r, branch):
            return jnp.broadcast_to(gate[:, 3 * r + branch:3 * r + branch + 1], (tq, LANES))

        partial = [gate_of(r, 0) * head(o_cmp, r) + gate_of(r, 2) * head(o_win, r) for r in range(r_heads)]
        gate_slc = [gate_of(r, 1) for r in range(r_heads)]
        q_ext = jnp.concatenate([q_rot, jnp.concatenate([bias] * r_heads, axis=0)], axis=1)
        return q_ext, partial, gate_slc

    fronts = [front(g) for g in range(groups)]

    m_sc[...] = jnp.full(m_sc.shape, NEG, F32)
    acc_sc[...] = jnp.zeros(acc_sc.shape, F32)

    def slc_keys(c):
        ks = pl.ds(pl.multiple_of(c * NSA_KC, NSA_KC), NSA_KC)
        return tuple(sw_ref[0, ks, slab(2 * g)] for g in range(groups)), oh_ref[ks, :]

    c_last = q0 // NSA_KC
    kpos = c_last * NSA_KC + lax.broadcasted_iota(jnp.int32, (rows, NSA_KC), 1)
    tpos = q0 + (lax.broadcasted_iota(jnp.int32, (rows, NSA_KC), 0) & (tq - 1))
    _attend([f[0] for f in fronts], slc_keys, c_last, kpos <= tpos, m_sc, acc_sc, sa_sc, sb_sc)

    for g, (_, partial, gate_slc) in enumerate(fronts):
        o_slc = _softmax_finish(acc_sc[g])
        outs = [partial[r] + gate_slc[r] * head(o_slc, r) for r in range(r_heads)]
        for pr in range(r_heads // 2):
            pair = jnp.where(lo, pltpu.roll(outs[2 * pr], HEAD_DIM, 1), outs[2 * pr + 1])
            o_ref[0, :, slab(2 * g + pr)] = pair.astype(BF16)


def _nsa(qn, gt, tabs, kvc, sw, ov):
    b, seq = qn.shape[:2]
    groups = NSA_KV_HEADS
    tq = NSA_TQ
    rows = NSA_GROUP * tq
    n_pad = kvc.shape[2]
    tab = pl.BlockSpec((tq, LANES), lambda i, k: (k, 0))
    return pl.pallas_call(
        _nsa_kernel,
        grid=(b, seq // tq),
        in_specs=[
            pl.BlockSpec((1, tq, NSA_Q_COLS), lambda i, k: (i, k, 0)),
            pl.BlockSpec((1, tq, groups * LANES), lambda i, k: (i, k, 0)),
            tab, tab, tab,
            pl.BlockSpec((1, groups, n_pad, LANES), lambda i, k: (i, 0, 0, 0)),
            pl.BlockSpec((1, seq, 2 * groups * LANES), lambda i, k: (i, 0, 0)),
            pl.BlockSpec((n_pad, LANES), lambda i, k: (0, 0)),
            pl.BlockSpec((seq, LANES), lambda i, k: (0, 0)),
        ],
        out_specs=pl.BlockSpec((1, tq, NSA_Q_COLS), lambda i, k: (i, k, 0)),
        out_shape=jax.ShapeDtypeStruct((b, seq, NSA_Q_COLS), BF16),
        scratch_shapes=[pltpu.VMEM((groups, rows, LANES), F32), pltpu.VMEM((groups, rows, 2 * LANES), F32),
                        pltpu.VMEM((groups, rows, NSA_KC), F32), pltpu.VMEM((groups, rows, NSA_KC), F32)],
        compiler_params=_params(("parallel", "arbitrary")),
        name="nsa",
    )(qn, gt, *tabs, kvc, sw, ov, _block_onehot(seq, SLC_BLOCK))


def _moba_kernel(q_ref, kv_ref, km_ref, oh_ref, o_ref, m_sc, acc_sc, sa_sc, sb_sc):
    tq = MOBA_TQ
    heads = MOBA_HEADS_PER_STEP
    q0 = pl.multiple_of(pl.program_id(2) * tq, tq)
    lo = lax.broadcasted_iota(jnp.int32, (tq, LANES), 1) < HEAD_DIM
    n_blk = km_ref.shape[1]
    blk_shift = MOBA_BLOCK.bit_length() - 1
    blk = lax.broadcasted_iota(jnp.int32, (n_blk, tq), 0)
    qblk = jnp.right_shift(q0 + lax.broadcasted_iota(jnp.int32, (n_blk, tq), 1), blk_shift)
    past = blk < qblk

    def slab(h):
        return slice(h * LANES, (h + 1) * LANES)

    q_ext = []
    for h in range(heads):
        q = jnp.where(lo if h % 2 == 0 else ~lo, q_ref[0, :, slab(h // 2)], jnp.zeros((), BF16))
        gate_t = _dot_nt(km_ref[0, :, slab(h)].astype(BF16), q)
        top = _rank_below(jnp.where(past, gate_t, NEG), n_blk, MOBA_TOPK)
        bias = _block_bias(((top & past) | (blk == qblk)).astype(F32), tq)
        q_ext.append(jnp.concatenate([q, bias], axis=1))

    m_sc[...] = jnp.full(m_sc.shape, NEG, F32)
    acc_sc[...] = jnp.zeros(acc_sc.shape, F32)

    def keys(c):
        ks = pl.ds(pl.multiple_of(c * tq, tq), tq)
        return tuple(kv_ref[0, ks, slab(h)] for h in range(heads)), oh_ref[ks, :]

    causal = lax.broadcasted_iota(jnp.int32, (tq, tq), 1) <= lax.broadcasted_iota(jnp.int32, (tq, tq), 0)
    _attend(q_ext, keys, pl.program_id(2), causal, m_sc, acc_sc, sa_sc, sb_sc)

    for pr in range(heads // 2):
        even = _softmax_finish(acc_sc[2 * pr])
        odd = _softmax_finish(acc_sc[2 * pr + 1])
        o_ref[0, :, slab(pr)] = jnp.where(lo, odd, even).astype(BF16)


def _moba(mq, mkv, km):
    b, seq = mq.shape[:2]
    tq = MOBA_TQ
    n_blk = seq // MOBA_BLOCK
    heads = MOBA_HEADS_PER_STEP
    return pl.pallas_call(
        _moba_kernel,
        grid=(b, MOBA_HEADS // heads, seq // tq),
        in_specs=[
            pl.BlockSpec((1, tq, heads // 2 * LANES), lambda i, j, k: (i, k, j)),
            pl.BlockSpec((1, seq, heads * LANES), lambda i, j, k: (i, 0, j)),
            pl.BlockSpec((1, n_blk, heads * LANES), lambda i, j, k: (i, 0, j)),
            pl.BlockSpec((seq, LANES), lambda i, j, k: (0, 0)),
        ],
        out_specs=pl.BlockSpec((1, tq, heads // 2 * LANES), lambda i, j, k: (i, k, j)),
        out_shape=jax.ShapeDtypeStruct((b, seq, MOBA_HEADS * HEAD_DIM), BF16),
        scratch_shapes=[pltpu.VMEM((heads, tq, LANES), F32), pltpu.VMEM((heads, tq, 2 * LANES), F32),
                        pltpu.VMEM((heads, tq, tq), F32), pltpu.VMEM((heads, tq, tq), F32)],
        compiler_params=_params(("parallel", "parallel", "arbitrary")),
        name="moba",
    )(mq, mkv, km, _block_onehot(seq, MOBA_BLOCK))


def _in_perm():
    d = HEAD_DIM
    perm = np.full((IN_COLS_P,), -1, np.int64)
    perm[SEC_Q:SEC_Q + NSA_Q_COLS] = np.arange(NSA_Q_COLS)
    kv0 = NSA_Q_COLS
    gate0 = kv0 + NSA_KV_COLS
    moba0 = gate0 + NSA_GATE_COLS
    per_group = NSA_GROUP * 3
    for g in range(NSA_KV_HEADS):
        perm[SEC_GATE + g * LANES:SEC_GATE + g * LANES + per_group] = gate0 + g * per_group + np.arange(per_group)

    def kv_cols(which, g):
        return kv0 + (which * NSA_KV_HEADS + g) * d + np.arange(d)

    for g in range(NSA_KV_HEADS):
        base = SEC_SW + g * 2 * LANES
        perm[base:base + d] = kv_cols(2, g)
        perm[base + d:base + 2 * d] = kv_cols(3, g)
        perm[base + 2 * d:base + 3 * d] = kv_cols(4, g)
        perm[base + 3 * d:base + 4 * d] = kv_cols(5, g)
        base = SEC_CMP + g * LANES
        perm[base:base + d] = kv_cols(0, g)
        perm[base + d:base + 2 * d] = kv_cols(1, g)
    perm[SEC_MQ:SEC_MQ + MOBA_HEADS * d] = moba0 + np.arange(MOBA_HEADS * d)
    for h in range(MOBA_HEADS):
        k_cols = moba0 + (MOBA_HEADS + h) * d + np.arange(d)
        v_cols = moba0 + (2 * MOBA_HEADS + h) * d + np.arange(d)
        base = SEC_MKV + h * LANES
        first, second = (k_cols, v_cols) if h % 2 == 0 else (v_cols, k_cols)
        perm[base:base + d] = first
        perm[base + d:base + 2 * d] = second
    return perm


def _take_runs(w, idx, axis):
    parts, i = [], 0
    while i < len(idx):
        j = i + 1
        if idx[i] < 0:
            while j < len(idx) and idx[j] < 0:
                j += 1
            shape = list(w.shape)
            shape[axis] = j - i
            parts.append(jnp.zeros(shape, w.dtype))
        else:
            while j < len(idx) and idx[j] == idx[j - 1] + 1:
                j += 1
            parts.append(lax.slice_in_dim(w, int(idx[i]), int(idx[j - 1]) + 1, axis=axis))
        i = j
    return jnp.concatenate(parts, axis=axis)


def _moba_out_rows():
    d = HEAD_DIM
    rows = []
    for pr in range(MOBA_HEADS // 2):
        rows.append(NSA_Q_COLS + (2 * pr + 1) * d + np.arange(d))
        rows.append(NSA_Q_COLS + (2 * pr) * d + np.arange(d))
    return np.concatenate(rows)


def _overlap_table(seq):
    n_pad = seq // CMP_STRIDE
    n = np.arange(n_pad)[:, None]
    j = np.arange(LANES)[None, :]
    starts = n * CMP_STRIDE
    sb = j * SLC_BLOCK
    ov = (starts < sb + SLC_BLOCK) & (starts + CMP_LEN > sb) & (n < n_pad - 1) & (j < seq // SLC_BLOCK)
    return ov.astype(np.float32)


def _block_onehot(seq, block):
    onehot = np.arange(seq)[:, None] // block == np.arange(LANES)[None, :]
    return jnp.asarray(onehot.astype(np.float32), BF16)


def _rope_lane_tables(seq):
    half = ROT_DIM // 2
    inv_freq = jnp.power(ROPE_THETA, -(jnp.arange(0, ROT_DIM, 2, dtype=F32) / ROT_DIM))
    ang = jnp.arange(seq, dtype=F32)[:, None] * inv_freq[None, :]
    cos, sin = jnp.cos(ang), jnp.sin(ang)
    rest = HEAD_DIM - ROT_DIM
    zeros_h = jnp.zeros((seq, half), F32)
    c = jnp.concatenate([cos, cos, jnp.ones((seq, rest), F32)], axis=1)
    sl = jnp.concatenate([-sin, zeros_h, jnp.zeros((seq, rest), F32)], axis=1)
    sr = jnp.concatenate([zeros_h, sin, jnp.zeros((seq, rest), F32)], axis=1)
    return tuple(jnp.tile(t, (1, LANES // HEAD_DIM)) for t in (c, sl, sr))


def _mixer(x2, b, seq, g_mix, w_p, layer, pos_ck, w_ck1, w_ck2, pos_cv, w_cv1, w_cv2, tabs, ov):
    qn, gt, sw, cmp_in, mq, mkv, km = _inproj(x2, g_mix, w_p, layer, tabs, seq)

    def halves(w_k, w_v):
        wk = w_k.reshape(2, CMP_STRIDE, HEAD_DIM, CMP_HIDDEN)
        wv = w_v.reshape(2, CMP_STRIDE, HEAD_DIM, CMP_HIDDEN)
        z = jnp.zeros_like(wk)
        return jnp.concatenate([jnp.concatenate([wk, z], axis=3), jnp.concatenate([z, wv], axis=3)],
                               axis=2).astype(BF16)

    w1 = halves(w_ck1, w_cv1)
    pos = jnp.concatenate([pos_ck, pos_cv], axis=1).reshape(2, CMP_STRIDE, LANES)
    z2 = jnp.zeros_like(w_ck2)
    w2 = jnp.concatenate([jnp.concatenate([w_ck2, z2], axis=1),
                          jnp.concatenate([z2, w_cv2], axis=1)], axis=0).astype(BF16)
    kvc = _compress(cmp_in.reshape(b, seq, -1), pos, w1[0], w1[1], w2)

    o_nsa = _nsa(qn.reshape(b, seq, -1), gt.reshape(b, seq, -1), tabs, kvc, sw.reshape(b, seq, -1), ov)
    o_moba = _moba(mq.reshape(b, seq, -1), mkv.reshape(b, seq, -1), km.reshape(b, seq // MOBA_BLOCK, -1))

    n = b * seq
    return o_nsa.reshape(n, -1), o_moba.reshape(n, -1)


def kernel(x, norm_ffn1, w_ffn1_gate, w_ffn1_up, w_ffn1_down, norm_mix, w_in, pos_ck, w_ck1, w_ck2,
           pos_cv, w_cv1, w_cv2, w_out, norm_ffn2, w_ffn2_gate, w_ffn2_up, w_ffn2_down, norm_final):
    b, seq, d = x.shape
    assert d == D_MODEL and seq % PROJ_ROWS == 0 and SLC_TOPN <= seq // SLC_BLOCK <= LANES
    depth = norm_ffn1.shape[0]
    tabs = _rope_lane_tables(seq)
    ov = jnp.asarray(_overlap_table(seq), BF16)
    x2 = x.reshape(b * seq, d)
    w_p = _take_runs(w_in.astype(BF16), _in_perm(), axis=2)
    w_out_b = w_out.astype(BF16)
    wa = w_out_b[:, :NSA_Q_COLS]
    wm = _take_runs(w_out_b, _moba_out_rows(), axis=1)
    wg1, wu1, wd1, wg2, wu2, wd2 = (w.astype(BF16) for w in (w_ffn1_gate, w_ffn1_up, w_ffn1_down,
                                                            w_ffn2_gate, w_ffn2_up, w_ffn2_down))
    for l in range(depth):
        x2 = _ffn(x2, norm_ffn1[l], wg1, wu1, wd1, l)
        o_nsa, o_moba = _mixer(x2, b, seq, norm_mix[l], w_p, l, pos_ck[l], w_ck1[l], w_ck2[l],
                               pos_cv[l], w_cv1[l], w_cv2[l], tabs, ov)
        x2 = _ffn(x2, norm_ffn2[l], wg2, wu2, wd2, l, mix=(o_nsa, o_moba, wa[l], wm[l]),
                  g_final=norm_final if l == depth - 1 else None)
    return x2.reshape(b, seq, d)
```

```python
import functools

import numpy as np
import jax
import jax.numpy as jnp
from jax import lax
from jax.experimental import pallas as pl
from jax.experimental.pallas import tpu as pltpu

D_MODEL = 1024
HEAD_DIM = 64
NSA_HEADS = 8
NSA_KV_HEADS = 2
NSA_GROUP = NSA_HEADS // NSA_KV_HEADS
MOBA_HEADS = 8
ROT_DIM = HEAD_DIM // 4
ROPE_THETA = 500000.0
CMP_LEN = 32
CMP_STRIDE = 16
CMP_HIDDEN = 256
SLC_BLOCK = 64
SLC_TOPN = 16
WINDOW = 512
MOBA_BLOCK = 256
MOBA_TOPK = 3
D_FF = 2816
EPS = 1e-6
NEG = -1e30
FORCE_BONUS = 1e4
SCALE = HEAD_DIM ** -0.5

NSA_Q_COLS = NSA_HEADS * HEAD_DIM
NSA_KV_COLS = 3 * 2 * NSA_KV_HEADS * HEAD_DIM
NSA_GATE_COLS = NSA_HEADS * 3
MOBA_COLS = 3 * MOBA_HEADS * HEAD_DIM

LANES = 128
SUBLANES = 8
VMEM_LIMIT = 56 * 1024 * 1024

SEC_Q = 0
SEC_GATE = 512
SEC_SW = 768
SEC_CMP = 1280
SEC_MQ = 1536
SEC_MKV = 2048
IN_COLS_P = 3072

F32 = jnp.float32
BF16 = jnp.bfloat16

FFN_ROWS = 512
FFN_CHUNK = 256
PROJ_ROWS = 512
NSA_TQ = 256
NSA_KC = 512
MOBA_HEADS_PER_STEP = 4
MOBA_TQ = 2 * MOBA_BLOCK
MASK_BIAS = 2.0 ** 100


def _dot(a, b):
    return jnp.dot(a, b, preferred_element_type=F32)


def _dot_nt(a, b):
    return lax.dot_general(a, b, (((1,), (1,)), ((), ())), preferred_element_type=F32)


def _rms(x, g):
    return x * lax.rsqrt(jnp.mean(x * x, axis=-1, keepdims=True) + EPS) * g


def _params(sem):
    return pltpu.CompilerParams(dimension_semantics=sem, vmem_limit_bytes=VMEM_LIMIT)


def _resident(shape, layer=None):
    nd = len(shape)
    if layer is None:
        return pl.BlockSpec(shape, lambda *_: (0,) * nd, pipeline_mode=pl.Buffered(1))
    return pl.BlockSpec((None,) + tuple(shape), lambda *_: (layer,) + (0,) * nd,
                        pipeline_mode=pl.Buffered(1))


def _ffn_kernel(*refs, mix, final_norm):
    refs = list(refs)
    x_ref = refs.pop(0)
    if mix:
        a_ref, mo_ref, wa_ref, wm_ref = (refs.pop(0) for _ in range(4))
    g_ref, wg_ref, wu_ref, wd_ref = (refs.pop(0) for _ in range(4))
    gf_ref = refs.pop(0) if final_norm else None
    o_ref, h_ref = refs
    x = x_ref[...]
    if mix:
        x = x + _dot(a_ref[...], wa_ref[...]) + _dot(mo_ref[...], wm_ref[...])
    xb = _rms(x, g_ref[...]).astype(BF16)
    for c in range(D_FF // FFN_CHUNK):
        sl = slice(c * FFN_CHUNK, (c + 1) * FFN_CHUNK)
        gate = _dot(xb, wg_ref[:, sl])
        up = _dot(xb, wu_ref[:, sl])
        h_ref[:, sl] = (jax.nn.silu(gate) * up).astype(BF16)
    y = x + 0.5 * _dot(h_ref[...], wd_ref[...])
    if final_norm:
        y = _rms(y, gf_ref[...])
    o_ref[...] = y


def _ffn(x2, g, wg, wu, wd, layer, mix=None, g_final=None):
    n = x2.shape[0]
    final_norm = g_final is not None

    def rows(width):
        return pl.BlockSpec((FFN_ROWS, width), lambda i: (i, 0))

    in_specs, args = [rows(D_MODEL)], [x2]
    if mix is not None:
        o_nsa, o_moba, wa, wm = mix
        in_specs += [rows(o_nsa.shape[1]), rows(o_moba.shape[1]), _resident(wa.shape), _resident(wm.shape)]
        args += [o_nsa, o_moba, wa, wm]
    in_specs += [_resident((1, D_MODEL)), _resident((D_MODEL, D_FF), layer),
                 _resident((D_MODEL, D_FF), layer), _resident((D_FF, D_MODEL), layer)]
    args += [g.reshape(1, D_MODEL), wg, wu, wd]
    if final_norm:
        in_specs.append(_resident((1, D_MODEL)))
        args.append(g_final.reshape(1, D_MODEL))
    return pl.pallas_call(
        functools.partial(_ffn_kernel, mix=mix is not None, final_norm=final_norm),
        grid=(n // FFN_ROWS,),
        in_specs=in_specs,
        out_specs=rows(D_MODEL),
        out_shape=jax.ShapeDtypeStruct((n, D_MODEL), F32),
        scratch_shapes=[pltpu.VMEM((FFN_ROWS, D_FF), BF16)],
        compiler_params=_params(("parallel",)),
        name="ffn" + ("_mix" if mix is not None else "") + ("_final" if final_norm else ""),
    )(*args)


def _rope_tables(c, sl, sr, mode):
    if mode == "both":
        return c, sl, sr
    lane = lax.broadcasted_iota(jnp.int32, c.shape, 1)
    keep = lane < HEAD_DIM if mode == "lo" else lane >= HEAD_DIM
    return jnp.where(keep, c, 1.0), jnp.where(keep, sl, 0.0), jnp.where(keep, sr, 0.0)


def _rope(v, tabs):
    c, sl, sr = tabs
    return v * c + pltpu.roll(v, LANES - ROT_DIM // 2, 1) * sl + pltpu.roll(v, ROT_DIM // 2, 1) * sr


def _inproj_kernel(x_ref, g_ref, w_ref, c_ref, sl_ref, sr_ref,
                   qp_ref, qr_ref, gt_ref, sw_ref, cmp_ref, mq_ref, mkv_ref, km_ref):
    xb = _rms(x_ref[...], g_ref[...]).astype(BF16)
    base = (c_ref[...], sl_ref[...], sr_ref[...])
    tabs = {m: _rope_tables(*base, m) for m in ("both", "lo", "hi")}

    qn = _dot(xb, w_ref[:, SEC_Q:SEC_GATE])
    lo = lax.broadcasted_iota(jnp.int32, (PROJ_ROWS, LANES), 1) < HEAD_DIM
    for pr in range(NSA_HEADS // 2):
        pair = qn[:, pr * LANES:(pr + 1) * LANES]
        for out_ref, v in ((qp_ref, pair), (qr_ref, _rope(pair, tabs["both"]))):
            for e in range(2):
                head = v if e == 0 else pltpu.roll(v, HEAD_DIM, 1)
                sl = slice((2 * pr + e) * LANES, (2 * pr + e + 1) * LANES)
                out_ref[:, sl] = jnp.where(lo, head * SCALE, 0.0).astype(BF16)
    gt_ref[...] = jax.nn.sigmoid(_dot(xb, w_ref[:, SEC_GATE:SEC_SW]))
    cmp_ref[...] = _dot(xb, w_ref[:, SEC_CMP:SEC_MQ])

    sw = _dot(xb, w_ref[:, SEC_SW:SEC_CMP])
    for i in range(4):
        sl = slice(i * LANES, (i + 1) * LANES)
        sw_ref[:, sl] = _rope(sw[:, sl], tabs["lo"]).astype(BF16)

    mq = _dot(xb, w_ref[:, SEC_MQ:SEC_MKV])
    for i in range(4):
        sl = slice(i * LANES, (i + 1) * LANES)
        mq_ref[:, sl] = (_rope(mq[:, sl], tabs["both"]) * SCALE).astype(BF16)

    mkv = _dot(xb, w_ref[:, SEC_MKV:IN_COLS_P])
    nblk = PROJ_ROWS // MOBA_BLOCK
    for h in range(MOBA_HEADS):
        sl = slice(h * LANES, (h + 1) * LANES)
        r = _rope(mkv[:, sl], tabs["lo" if h % 2 == 0 else "hi"])
        mkv_ref[:, sl] = r.astype(BF16)
        km_ref[:, 0, sl] = jnp.mean(r.reshape(nblk, MOBA_BLOCK, LANES), axis=1)


def _inproj(x2, g, w_p, layer, tabs, seq):
    n = x2.shape[0]
    tiles_per_seq = seq // PROJ_ROWS
    nblk = PROJ_ROWS // MOBA_BLOCK

    def rows(width):
        return pl.BlockSpec((PROJ_ROWS, width), lambda i: (i, 0))

    tab = pl.BlockSpec((PROJ_ROWS, LANES), lambda i: (i % tiles_per_seq, 0))
    out_shapes = (
        jax.ShapeDtypeStruct((n, 1024), BF16),
        jax.ShapeDtypeStruct((n, 1024), BF16),
        jax.ShapeDtypeStruct((n, 256), F32),
        jax.ShapeDtypeStruct((n, 512), BF16),
        jax.ShapeDtypeStruct((n, 256), F32),
        jax.ShapeDtypeStruct((n, 512), BF16),
        jax.ShapeDtypeStruct((n, 1024), BF16),
        jax.ShapeDtypeStruct((n // MOBA_BLOCK, 1, 1024), F32),
    )
    out_specs = (rows(1024), rows(1024), rows(256), rows(512), rows(256), rows(512), rows(1024),
                 pl.BlockSpec((nblk, 1, 1024), lambda i: (i, 0, 0)))
    return pl.pallas_call(
        _inproj_kernel,
        grid=(n // PROJ_ROWS,),
        in_specs=[rows(D_MODEL), _resident((1, D_MODEL)), _resident((D_MODEL, IN_COLS_P), layer),
                  tab, tab, tab],
        out_specs=out_specs,
        out_shape=out_shapes,
        compiler_params=_params(("parallel",)),
        name="inproj",
    )(x2, g.reshape(1, D_MODEL), w_p, *tabs)


def _compress_kernel(x_ref, pos_ref, wa_ref, wb_ref, w2_ref, o_ref):
    n_pad = x_ref.shape[1] // CMP_STRIDE
    width = 2 * CMP_HIDDEN
    ya = jnp.zeros((n_pad, width), F32)
    zb = jnp.zeros((n_pad, width), F32)
    for l in range(CMP_STRIDE):
        h = x_ref[0, pl.ds(l, n_pad, stride=CMP_STRIDE), :]
        ya = ya + _dot((h + pos_ref[0, l:l + 1, :]).astype(BF16), wa_ref[l])
        zb = zb + _dot((h + pos_ref[1, l:l + 1, :]).astype(BF16), wb_ref[l])
    pre = ya + pltpu.roll(zb, n_pad - 1, 0)
    out = _dot(jax.nn.gelu(pre).astype(BF16), w2_ref[...])
    row = lax.broadcasted_iota(jnp.int32, out.shape, 0)
    o_ref[0, 0] = jnp.where(row < n_pad - 1, out, 0.0).astype(BF16)


def _compress(cmp_in, pos, wa, wb, w2):
    b, seq = cmp_in.shape[:2]
    g = NSA_KV_HEADS
    n_pad = seq // CMP_STRIDE
    return pl.pallas_call(
        _compress_kernel,
        grid=(b, g),
        in_specs=[pl.BlockSpec((1, seq, LANES), lambda i, j: (i, 0, j)),
                  _resident(pos.shape), _resident(wa.shape), _resident(wb.shape), _resident(w2.shape)],
        out_specs=pl.BlockSpec((1, 1, n_pad, LANES), lambda i, j: (i, j, 0, 0)),
        out_shape=jax.ShapeDtypeStruct((b, g, n_pad, LANES), BF16),
        compiler_params=_params(("parallel", "parallel")),
        name="compress",
    )(cmp_in, pos, wa, wb, w2)


def _rank_below(s_t, n_rows, limit):
    t = s_t.shape[1]
    n_grp = n_rows // SUBLANES
    grp = [s_t[g * SUBLANES:(g + 1) * SUBLANES] for g in range(n_grp)]
    cnt = [jnp.zeros((SUBLANES, t), F32) for _ in range(n_grp)]
    sub = lax.broadcasted_iota(jnp.int32, (SUBLANES, t), 0)
    for i in range(n_rows):
        ri = s_t[i:i + 1, :]
        for g in range(n_grp):
            if g > i // SUBLANES:
                beats = ri >= grp[g]
            elif g < i // SUBLANES:
                beats = ri > grp[g]
            else:
                beats = (ri > grp[g]) | ((ri == grp[g]) & (sub > i % SUBLANES))
            cnt[g] = cnt[g] + jnp.where(beats, 1.0, 0.0)
    return jnp.concatenate(cnt, axis=0) < limit


def _block_bias(sel_t, t):
    pad = jnp.zeros((LANES - sel_t.shape[0], t), F32)
    sel = jnp.concatenate([sel_t, pad], axis=0).T
    return ((sel - 1.0) * MASK_BIAS).astype(BF16)


def _with_ones(kv):
    return jnp.concatenate([kv, jnp.ones(kv.shape, kv.dtype)], axis=1)


def _softmax_step(read_s, kv, m_ref, acc_ref):
    m_old = m_ref[...]
    m_new = jnp.maximum(m_old, jnp.broadcast_to(jnp.max(read_s(), axis=-1, keepdims=True), m_old.shape))
    m_ref[...] = m_new
    alpha = jnp.exp(m_old - m_new)
    s = read_s()
    p = jnp.exp(s - jnp.concatenate([m_new] * (s.shape[1] // LANES), axis=1)).astype(BF16)
    acc_ref[...] = jnp.concatenate([alpha, alpha], axis=1) * acc_ref[...] + _dot(p, _with_ones(kv))


def _attend(q_ext, keys, n_past, causal, m_ref, acc_ref, s_a, s_b):
    def put_scores(c, s_ref):
        kvs, onehot = keys(c)
        for h, (q, kv) in enumerate(zip(q_ext, kvs)):
            s_ref[h] = _dot_nt(q, jnp.concatenate([kv, onehot], axis=1))

    def softmax(c, s_ref, mask=None):
        for h, kv in enumerate(keys(c)[0]):
            if mask is None:
                read_s = functools.partial(lambda h: s_ref[h], h)
            else:
                read_s = functools.partial(lambda s: s, jnp.where(mask, s_ref[h], -MASK_BIAS))
            _softmax_step(read_s, kv, m_ref.at[h], acc_ref.at[h])

    put_scores(0, s_a)

    def body(i, carry):
        c = 2 * i
        put_scores(c + 1, s_b)
        softmax(c, s_a)
        put_scores(c + 2, s_a)
        softmax(c + 1, s_b)
        return carry

    lax.fori_loop(0, n_past // 2, body, 0)

    @pl.when(n_past % 2 == 1)
    def _():
        put_scores(n_past, s_b)
        softmax(n_past - 1, s_a)
        softmax(n_past, s_b, causal)

    @pl.when(n_past % 2 == 0)
    def _():
        softmax(n_past, s_a, causal)


def _softmax_finish(acc):
    return acc[:, :LANES] / jnp.maximum(acc[:, LANES:], 1.0)


def _nsa_kernel(qp_ref, qr_ref, gt_ref, wb0_ref, wb1_ref, kvc_ref, sw_ref, ov_ref, oh_ref,
                o_ref, m_sc, acc_sc, sa_sc, sb_sc):
    wb_refs = (wb0_ref, wb1_ref)
    tq, r_heads, groups = NSA_TQ, NSA_GROUP, NSA_KV_HEADS
    rows = r_heads * tq
    q0 = pl.program_id(1) * tq
    lo = lax.broadcasted_iota(jnp.int32, (tq, LANES), 1) < HEAD_DIM
    n_slc = sw_ref.shape[1] // SLC_BLOCK
    blk_shift = SLC_BLOCK.bit_length() - 1

    def slab(i):
        return slice(i * LANES, (i + 1) * LANES)

    def head(o, r):
        return o[r * tq:(r + 1) * tq]

    def front(g):
        def stack(ref):
            return jnp.concatenate([ref[0, :, slab(r_heads * g + r)] for r in range(r_heads)], axis=0)

        q_plain = stack(qp_ref)
        q_rot = stack(qr_ref)

        half = tq // 2
        o_win = []
        for u, wb_ref in enumerate(wb_refs):
            t0 = q0 + u * half
            w0 = pl.multiple_of(jnp.maximum(t0 - WINDOW, 0), half)
            kvw = sw_ref[0, pl.ds(w0, WINDOW + half), slab(2 * g + 1)]
            q_u = jnp.concatenate([qr_ref[0, u * half:(u + 1) * half, slab(r_heads * g + r)]
                                   for r in range(r_heads)], axis=0)
            s = _dot_nt(q_u, kvw).reshape(r_heads, half, WINDOW + half) + wb_ref[...][None]
            p = jnp.exp(s - jnp.max(s, axis=-1, keepdims=True)).reshape(r_heads * half, WINDOW + half)
            o_win.append(_softmax_finish(_dot(p.astype(BF16), _with_ones(kvw))))

        def win_head(r):
            return jnp.concatenate([o[r * half:(r + 1) * half] for o in o_win], axis=0)

        kvc = kvc_ref[0, g]
        n_pad = kvc.shape[0]
        n_idx = lax.broadcasted_iota(jnp.int32, (tq, n_pad), 1)
        t_idx = q0 + lax.broadcasted_iota(jnp.int32, (tq, n_pad), 0)
        valid_c = (n_idx * CMP_STRIDE + CMP_LEN - 1 <= t_idx)[None]
        s = jnp.where(valid_c, _dot_nt(q_plain, kvc).reshape(r_heads, tq, n_pad), NEG)
        m = jnp.maximum(jnp.max(s, axis=-1, keepdims=True), NEG / 2)
        p = jnp.exp(s - m)
        p = p / jnp.maximum(jnp.sum(p, axis=-1, keepdims=True), 1.0)
        o_cmp = _dot(p.reshape(rows, n_pad).astype(BF16), kvc)

        p_sum = p[0] + p[1] + p[2] + p[3]
        ov = ov_ref[...]
        hi = p_sum.astype(BF16)
        r1 = p_sum - hi.astype(F32)
        mid = r1.astype(BF16)
        low = (r1 - mid.astype(F32)).astype(BF16)
        imp = _dot(hi, ov) + _dot(mid, ov) + _dot(low, ov)

        j_idx = lax.broadcasted_iota(jnp.int32, (tq, LANES), 1)
        t_row = q0 + lax.broadcasted_iota(jnp.int32, (tq, LANES), 0)
        qblk = jnp.right_shift(t_row, blk_shift)
        forced = (j_idx == 0) | (j_idx == qblk) | (j_idx == qblk - 1)
        visible = j_idx * SLC_BLOCK <= t_row
        score = jnp.where(visible, jnp.where(forced, FORCE_BONUS, imp), NEG)
        sel_t = _rank_below(score.T[:n_slc], n_slc, SLC_TOPN).astype(F32)
        bias = _block_bias(sel_t, tq)

        gate = gt_ref[0, :, slab(g)]

        def gate_of(r, branch):
            return jnp.broadcast_to(gate[:, 3 * r + branch:3 * r + branch + 1], (tq, LANES))

        partial = [gate_of(r, 0) * head(o_cmp, r) + gate_of(r, 2) * win_head(r) for r in range(r_heads)]
        gate_slc = [gate_of(r, 1) for r in range(r_heads)]
        q_ext = jnp.concatenate([q_rot, jnp.concatenate([bias] * r_heads, axis=0)], axis=1)
        return q_ext, partial, gate_slc

    fronts = [front(g) for g in range(groups)]

    m_sc[...] = jnp.full(m_sc.shape, NEG, F32)
    acc_sc[...] = jnp.zeros(acc_sc.shape, F32)

    def slc_keys(c):
        ks = pl.ds(pl.multiple_of(c * NSA_KC, NSA_KC), NSA_KC)
        return tuple(sw_ref[0, ks, slab(2 * g)] for g in range(groups)), oh_ref[ks, :]

    c_last = q0 // NSA_KC
    kpos = c_last * NSA_KC + lax.broadcasted_iota(jnp.int32, (rows, NSA_KC), 1)
    tpos = q0 + (lax.broadcasted_iota(jnp.int32, (rows, NSA_KC), 0) & (tq - 1))
    _attend([f[0] for f in fronts], slc_keys, c_last, kpos <= tpos, m_sc, acc_sc, sa_sc, sb_sc)

    for g, (_, partial, gate_slc) in enumerate(fronts):
        o_slc = _softmax_finish(acc_sc[g])
        outs = [partial[r] + gate_slc[r] * head(o_slc, r) for r in range(r_heads)]
        for pr in range(r_heads // 2):
            pair = jnp.where(lo, pltpu.roll(outs[2 * pr], HEAD_DIM, 1), outs[2 * pr + 1])
            o_ref[0, :, slab(2 * g + pr)] = pair.astype(BF16)


def _window_bias(seq, half):
    offsets = sorted({t0 - max(t0 - WINDOW, 0) for t0 in range(0, seq, half)})
    i = np.arange(half)[:, None]
    c = np.arange(WINDOW + half)[None, :]
    tabs = []
    for off in offsets:
        d = off + i - c
        tabs.append(np.where((d >= 0) & (d < WINDOW), 0.0, NEG))
    return np.stack(tabs).astype(np.float32)


def _nsa(qp, qr, gt, kvc, sw, ov):
    b, seq = qp.shape[:2]
    groups = NSA_KV_HEADS
    tq = NSA_TQ
    rows = NSA_GROUP * tq
    n_pad = kvc.shape[2]
    half = tq // 2
    wb = jnp.asarray(_window_bias(seq, half))
    last = wb.shape[0] - 1

    def band(u):
        return pl.BlockSpec((None, half, WINDOW + half), lambda i, k: (jnp.minimum(2 * k + u, last), 0, 0))

    return pl.pallas_call(
        _nsa_kernel,
        grid=(b, seq // tq),
        in_specs=[
            pl.BlockSpec((1, tq, NSA_HEADS * LANES), lambda i, k: (i, k, 0)),
            pl.BlockSpec((1, tq, NSA_HEADS * LANES), lambda i, k: (i, k, 0)),
            pl.BlockSpec((1, tq, groups * LANES), lambda i, k: (i, k, 0)),
            band(0), band(1),
            pl.BlockSpec((1, groups, n_pad, LANES), lambda i, k: (i, 0, 0, 0)),
            pl.BlockSpec((1, seq, 2 * groups * LANES), lambda i, k: (i, 0, 0)),
            pl.BlockSpec((n_pad, LANES), lambda i, k: (0, 0)),
            pl.BlockSpec((seq, LANES), lambda i, k: (0, 0)),
        ],
        out_specs=pl.BlockSpec((1, tq, NSA_Q_COLS), lambda i, k: (i, k, 0)),
        out_shape=jax.ShapeDtypeStruct((b, seq, NSA_Q_COLS), BF16),
        scratch_shapes=[pltpu.VMEM((groups, rows, LANES), F32), pltpu.VMEM((groups, rows, 2 * LANES), F32),
                        pltpu.VMEM((groups, rows, NSA_KC), F32), pltpu.VMEM((groups, rows, NSA_KC), F32)],
        compiler_params=_params(("parallel", "arbitrary")),
        name="nsa",
    )(qp, qr, gt, wb, wb, kvc, sw, ov, _block_onehot(seq, SLC_BLOCK))


def _moba_kernel(q_ref, kv_ref, km_ref, oh_ref, o_ref, m_sc, acc_sc, sa_sc, sb_sc):
    tq = MOBA_TQ
    heads = MOBA_HEADS_PER_STEP
    q0 = pl.multiple_of(pl.program_id(2) * tq, tq)
    lo = lax.broadcasted_iota(jnp.int32, (tq, LANES), 1) < HEAD_DIM
    n_blk = km_ref.shape[1]
    blk_shift = MOBA_BLOCK.bit_length() - 1
    blk = lax.broadcasted_iota(jnp.int32, (n_blk, tq), 0)
    qblk = jnp.right_shift(q0 + lax.broadcasted_iota(jnp.int32, (n_blk, tq), 1), blk_shift)
    past = blk < qblk

    def slab(h):
        return slice(h * LANES, (h + 1) * LANES)

    q_ext = []
    for h in range(heads):
        q = jnp.where(lo if h % 2 == 0 else ~lo, q_ref[0, :, slab(h // 2)], jnp.zeros((), BF16))
        gate_t = _dot_nt(km_ref[0, :, slab(h)].astype(BF16), q)
        top = _rank_below(jnp.where(past, gate_t, NEG), n_blk, MOBA_TOPK)
        bias = _block_bias(((top & past) | (blk == qblk)).astype(F32), tq)
        q_ext.append(jnp.concatenate([q, bias], axis=1))

    m_sc[...] = jnp.full(m_sc.shape, NEG, F32)
    acc_sc[...] = jnp.zeros(acc_sc.shape, F32)

    def keys(c):
        ks = pl.ds(pl.multiple_of(c * tq, tq), tq)
        return tuple(kv_ref[0, ks, slab(h)] for h in range(heads)), oh_ref[ks, :]

    causal = lax.broadcasted_iota(jnp.int32, (tq, tq), 1) <= lax.broadcasted_iota(jnp.int32, (tq, tq), 0)
    _attend(q_ext, keys, pl.program_id(2), causal, m_sc, acc_sc, sa_sc, sb_sc)

    for pr in range(heads // 2):
        even = _softmax_finish(acc_sc[2 * pr])
        odd = _softmax_finish(acc_sc[2 * pr + 1])
        o_ref[0, :, slab(pr)] = jnp.where(lo, odd, even).astype(BF16)


def _moba(mq, mkv, km):
    b, seq = mq.shape[:2]
    tq = MOBA_TQ
    n_blk = seq // MOBA_BLOCK
    heads = MOBA_HEADS_PER_STEP
    return pl.pallas_call(
        _moba_kernel,
        grid=(b, MOBA_HEADS // heads, seq // tq),
        in_specs=[
            pl.BlockSpec((1, tq, heads // 2 * LANES), lambda i, j, k: (i, k, j)),
            pl.BlockSpec((1, seq, heads * LANES), lambda i, j, k: (i, 0, j)),
            pl.BlockSpec((1, n_blk, heads * LANES), lambda i, j, k: (i, 0, j)),
            pl.BlockSpec((seq, LANES), lambda i, j, k: (0, 0)),
        ],
        out_specs=pl.BlockSpec((1, tq, heads // 2 * LANES), lambda i, j, k: (i, k, j)),
        out_shape=jax.ShapeDtypeStruct((b, seq, MOBA_HEADS * HEAD_DIM), BF16),
        scratch_shapes=[pltpu.VMEM((heads, tq, LANES), F32), pltpu.VMEM((heads, tq, 2 * LANES), F32),
                        pltpu.VMEM((heads, tq, tq), F32), pltpu.VMEM((heads, tq, tq), F32)],
        compiler_params=_params(("parallel", "parallel", "arbitrary")),
        name="moba",
    )(mq, mkv, km, _block_onehot(seq, MOBA_BLOCK))


def _in_perm():
    d = HEAD_DIM
    perm = np.full((IN_COLS_P,), -1, np.int64)
    perm[SEC_Q:SEC_Q + NSA_Q_COLS] = np.arange(NSA_Q_COLS)
    kv0 = NSA_Q_COLS
    gate0 = kv0 + NSA_KV_COLS
    moba0 = gate0 + NSA_GATE_COLS
    per_group = NSA_GROUP * 3
    for g in range(NSA_KV_HEADS):
        perm[SEC_GATE + g * LANES:SEC_GATE + g * LANES + per_group] = gate0 + g * per_group + np.arange(per_group)

    def kv_cols(which, g):
        return kv0 + (which * NSA_KV_HEADS + g) * d + np.arange(d)

    for g in range(NSA_KV_HEADS):
        base = SEC_SW + g * 2 * LANES
        perm[base:base + d] = kv_cols(2, g)
        perm[base + d:base + 2 * d] = kv_cols(3, g)
        perm[base + 2 * d:base + 3 * d] = kv_cols(4, g)
        perm[base + 3 * d:base + 4 * d] = kv_cols(5, g)
        base = SEC_CMP + g * LANES
        perm[base:base + d] = kv_cols(0, g)
        perm[base + d:base + 2 * d] = kv_cols(1, g)
    perm[SEC_MQ:SEC_MQ + MOBA_HEADS * d] = moba0 + np.arange(MOBA_HEADS * d)
    for h in range(MOBA_HEADS):
        k_cols = moba0 + (MOBA_HEADS + h) * d + np.arange(d)
        v_cols = moba0 + (2 * MOBA_HEADS + h) * d + np.arange(d)
        base = SEC_MKV + h * LANES
        first, second = (k_cols, v_cols) if h % 2 == 0 else (v_cols, k_cols)
        perm[base:base + d] = first
        perm[base + d:base + 2 * d] = second
    return perm


def _take_runs(w, idx, axis):
    parts, i = [], 0
    while i < len(idx):
        j = i + 1
        if idx[i] < 0:
            while j < len(idx) and idx[j] < 0:
                j += 1
            shape = list(w.shape)
            shape[axis] = j - i
            parts.append(jnp.zeros(shape, w.dtype))
        else:
            while j < len(idx) and idx[j] == idx[j - 1] + 1:
                j += 1
            parts.append(lax.slice_in_dim(w, int(idx[i]), int(idx[j - 1]) + 1, axis=axis))
        i = j
    return jnp.concatenate(parts, axis=axis)


def _moba_out_rows():
    d = HEAD_DIM
    rows = []
    for pr in range(MOBA_HEADS // 2):
        rows.append(NSA_Q_COLS + (2 * pr + 1) * d + np.arange(d))
        rows.append(NSA_Q_COLS + (2 * pr) * d + np.arange(d))
    return np.concatenate(rows)


def _overlap_table(seq):
    n_pad = seq // CMP_STRIDE
    n = np.arange(n_pad)[:, None]
    j = np.arange(LANES)[None, :]
    starts = n * CMP_STRIDE
    sb = j * SLC_BLOCK
    ov = (starts < sb + SLC_BLOCK) & (starts + CMP_LEN > sb) & (n < n_pad - 1) & (j < seq // SLC_BLOCK)
    return ov.astype(np.float32)


def _block_onehot(seq, block):
    onehot = np.arange(seq)[:, None] // block == np.arange(LANES)[None, :]
    return jnp.asarray(onehot.astype(np.float32), BF16)


def _rope_lane_tables(seq):
    half = ROT_DIM // 2
    inv_freq = jnp.power(ROPE_THETA, -(jnp.arange(0, ROT_DIM, 2, dtype=F32) / ROT_DIM))
    ang = jnp.arange(seq, dtype=F32)[:, None] * inv_freq[None, :]
    cos, sin = jnp.cos(ang), jnp.sin(ang)
    rest = HEAD_DIM - ROT_DIM
    zeros_h = jnp.zeros((seq, half), F32)
    c = jnp.concatenate([cos, cos, jnp.ones((seq, rest), F32)], axis=1)
    sl = jnp.concatenate([-sin, zeros_h, jnp.zeros((seq, rest), F32)], axis=1)
    sr = jnp.concatenate([zeros_h, sin, jnp.zeros((seq, rest), F32)], axis=1)
    return tuple(jnp.tile(t, (1, LANES // HEAD_DIM)) for t in (c, sl, sr))


def _mixer(x2, b, seq, g_mix, w_p, layer, pos_ck, w_ck1, w_ck2, pos_cv, w_cv1, w_cv2, tabs, ov):
    qp, qr, gt, sw, cmp_in, mq, mkv, km = _inproj(x2, g_mix, w_p, layer, tabs, seq)

    def halves(w_k, w_v):
        wk = w_k.reshape(2, CMP_STRIDE, HEAD_DIM, CMP_HIDDEN)
        wv = w_v.reshape(2, CMP_STRIDE, HEAD_DIM, CMP_HIDDEN)
        z = jnp.zeros_like(wk)
        return jnp.concatenate([jnp.concatenate([wk, z], axis=3), jnp.concatenate([z, wv], axis=3)],
                               axis=2).astype(BF16)

    w1 = halves(w_ck1, w_cv1)
    pos = jnp.concatenate([pos_ck, pos_cv], axis=1).reshape(2, CMP_STRIDE, LANES)
    z2 = jnp.zeros_like(w_ck2)
    w2 = jnp.concatenate([jnp.concatenate([w_ck2, z2], axis=1),
                          jnp.concatenate([z2, w_cv2], axis=1)], axis=0).astype(BF16)
    kvc = _compress(cmp_in.reshape(b, seq, -1), pos, w1[0], w1[1], w2)

    o_nsa = _nsa(qp.reshape(b, seq, -1), qr.reshape(b, seq, -1), gt.reshape(b, seq, -1), kvc,
                 sw.reshape(b, seq, -1), ov)
    o_moba = _moba(mq.reshape(b, seq, -1), mkv.reshape(b, seq, -1), km.reshape(b, seq // MOBA_BLOCK, -1))

    n = b * seq
    return o_nsa.reshape(n, -1), o_moba.reshape(n, -1)


def kernel(x, norm_ffn1, w_ffn1_gate, w_ffn1_up, w_ffn1_down, norm_mix, w_in, pos_ck, w_ck1, w_ck2,
           pos_cv, w_cv1, w_cv2, w_out, norm_ffn2, w_ffn2_gate, w_ffn2_up, w_ffn2_down, norm_final):
    b, seq, d = x.shape
    assert d == D_MODEL and seq % PROJ_ROWS == 0 and SLC_TOPN <= seq // SLC_BLOCK <= LANES
    depth = norm_ffn1.shape[0]
    tabs = _rope_lane_tables(seq)
    ov = jnp.asarray(_overlap_table(seq), BF16)
    x2 = x.reshape(b * seq, d)
    w_p = _take_runs(w_in.astype(BF16), _in_perm(), axis=2)
    w_out_b = w_out.astype(BF16)
    wa = w_out_b[:, :NSA_Q_COLS]
    wm = _take_runs(w_out_b, _moba_out_rows(), axis=1)
    wg1, wu1, wd1, wg2, wu2, wd2 = (w.astype(BF16) for w in (w_ffn1_gate, w_ffn1_up, w_ffn1_down,
                                                            w_ffn2_gate, w_ffn2_up, w_ffn2_down))
    for l in range(depth):
        x2 = _ffn(x2, norm_ffn1[l], wg1, wu1, wd1, l)
        o_nsa, o_moba = _mixer(x2, b, seq, norm_mix[l], w_p, l, pos_ck[l], w_ck1[l], w_ck2[l],
                               pos_cv[l], w_cv1[l], w_cv2[l], tabs, ov)
        x2 = _ffn(x2, norm_ffn2[l], wg2, wu2, wd2, l, mix=(o_nsa, o_moba, wa[l], wm[l]),
                  g_final=norm_final if l == depth - 1 else None)
    return x2.reshape(b, seq, d)
```

```python
import functools

import numpy as np
import jax
import jax.numpy as jnp
from jax import lax
from jax.experimental import pallas as pl
from jax.experimental.pallas import tpu as pltpu

D_MODEL = 1024
HEAD_DIM = 64
NSA_HEADS = 8
NSA_KV_HEADS = 2
NSA_GROUP = NSA_HEADS // NSA_KV_HEADS
MOBA_HEADS = 8
ROT_DIM = HEAD_DIM // 4
ROPE_THETA = 500000.0
CMP_LEN = 32
CMP_STRIDE = 16
CMP_HIDDEN = 256
SLC_BLOCK = 64
SLC_TOPN = 16
WINDOW = 512
MOBA_BLOCK = 256
MOBA_TOPK = 3
D_FF = 2816
EPS = 1e-6
NEG = -1e30
FORCE_BONUS = 1e4
SCALE = HEAD_DIM ** -0.5

NSA_Q_COLS = NSA_HEADS * HEAD_DIM
NSA_KV_COLS = 3 * 2 * NSA_KV_HEADS * HEAD_DIM
NSA_GATE_COLS = NSA_HEADS * 3
MOBA_COLS = 3 * MOBA_HEADS * HEAD_DIM

LANES = 128
SUBLANES = 8
VMEM_LIMIT = 56 * 1024 * 1024

SEC_Q = 0
SEC_GATE = 512
SEC_SW = 768
SEC_CMP = 1280
SEC_MQ = 1536
SEC_MKV = 2048
IN_COLS_P = 3072

F32 = jnp.float32
BF16 = jnp.bfloat16

FFN_ROWS = 512
FFN_CHUNK = 256
PROJ_ROWS = 512
NSA_TQ = 256
NSA_KC = 512
MOBA_HEADS_PER_STEP = 4
MOBA_TQ = 2 * MOBA_BLOCK
MASK_BIAS = 2.0 ** 100


def _dot(a, b):
    return jnp.dot(a, b, preferred_element_type=F32)


def _dot_nt(a, b):
    return lax.dot_general(a, b, (((1,), (1,)), ((), ())), preferred_element_type=F32)


def _rms(x, g):
    return x * lax.rsqrt(jnp.mean(x * x, axis=-1, keepdims=True) + EPS) * g


def _params(sem):
    return pltpu.CompilerParams(dimension_semantics=sem, vmem_limit_bytes=VMEM_LIMIT)


def _resident(shape, layer=None):
    nd = len(shape)
    if layer is None:
        return pl.BlockSpec(shape, lambda *_: (0,) * nd, pipeline_mode=pl.Buffered(1))
    return pl.BlockSpec((None,) + tuple(shape), lambda *_: (layer,) + (0,) * nd,
                        pipeline_mode=pl.Buffered(1))


def _ffn_kernel(*refs, mix, final_norm):
    refs = list(refs)
    x_ref = refs.pop(0)
    if mix:
        a_ref, mo_ref, wa_ref, wm_ref = (refs.pop(0) for _ in range(4))
    g_ref, wg_ref, wu_ref, wd_ref = (refs.pop(0) for _ in range(4))
    gf_ref = refs.pop(0) if final_norm else None
    o_ref, h_ref = refs
    x = x_ref[...]
    if mix:
        x = x + _dot(a_ref[...], wa_ref[...]) + _dot(mo_ref[...], wm_ref[...])
    xb = _rms(x, g_ref[...]).astype(BF16)
    for c in range(D_FF // FFN_CHUNK):
        sl = slice(c * FFN_CHUNK, (c + 1) * FFN_CHUNK)
        gate = _dot(xb, wg_ref[:, sl])
        up = _dot(xb, wu_ref[:, sl])
        h_ref[:, sl] = (jax.nn.silu(gate) * up).astype(BF16)
    y = x + 0.5 * _dot(h_ref[...], wd_ref[...])
    if final_norm:
        y = _rms(y, gf_ref[...])
    o_ref[...] = y


def _ffn(x2, g, wg, wu, wd, layer, mix=None, g_final=None):
    n = x2.shape[0]
    final_norm = g_final is not None

    def rows(width):
        return pl.BlockSpec((FFN_ROWS, width), lambda i: (i, 0))

    in_specs, args = [rows(D_MODEL)], [x2]
    if mix is not None:
        o_nsa, o_moba, wa, wm = mix
        in_specs += [rows(o_nsa.shape[1]), rows(o_moba.shape[1]), _resident(wa.shape), _resident(wm.shape)]
        args += [o_nsa, o_moba, wa, wm]
    in_specs += [_resident((1, D_MODEL)), _resident((D_MODEL, D_FF), layer),
                 _resident((D_MODEL, D_FF), layer), _resident((D_FF, D_MODEL), layer)]
    args += [g.reshape(1, D_MODEL), wg, wu, wd]
    if final_norm:
        in_specs.append(_resident((1, D_MODEL)))
        args.append(g_final.reshape(1, D_MODEL))
    return pl.pallas_call(
        functools.partial(_ffn_kernel, mix=mix is not None, final_norm=final_norm),
        grid=(n // FFN_ROWS,),
        in_specs=in_specs,
        out_specs=rows(D_MODEL),
        out_shape=jax.ShapeDtypeStruct((n, D_MODEL), F32),
        scratch_shapes=[pltpu.VMEM((FFN_ROWS, D_FF), BF16)],
        compiler_params=_params(("parallel",)),
        name="ffn" + ("_mix" if mix is not None else "") + ("_final" if final_norm else ""),
    )(*args)


def _rope_tables(c, sl, sr, mode):
    if mode == "both":
        return c, sl, sr
    lane = lax.broadcasted_iota(jnp.int32, c.shape, 1)
    keep = lane < HEAD_DIM if mode == "lo" else lane >= HEAD_DIM
    return jnp.where(keep, c, 1.0), jnp.where(keep, sl, 0.0), jnp.where(keep, sr, 0.0)


def _rope(v, tabs):
    c, sl, sr = tabs
    return v * c + pltpu.roll(v, LANES - ROT_DIM // 2, 1) * sl + pltpu.roll(v, ROT_DIM // 2, 1) * sr


def _inproj_kernel(x_ref, g_ref, w_ref, c_ref, sl_ref, sr_ref,
                   qp_ref, qr_ref, gt_ref, sw_ref, cmp_ref, mq_ref, mkv_ref, km_ref):
    xb = _rms(x_ref[...], g_ref[...]).astype(BF16)
    base = (c_ref[...], sl_ref[...], sr_ref[...])
    tabs = {m: _rope_tables(*base, m) for m in ("both", "lo", "hi")}

    qn = _dot(xb, w_ref[:, SEC_Q:SEC_GATE])
    lo = lax.broadcasted_iota(jnp.int32, (PROJ_ROWS, LANES), 1) < HEAD_DIM
    for pr in range(NSA_HEADS // 2):
        pair = qn[:, pr * LANES:(pr + 1) * LANES]
        for out_ref, v in ((qp_ref, pair), (qr_ref, _rope(pair, tabs["both"]))):
            for e in range(2):
                head = v if e == 0 else pltpu.roll(v, HEAD_DIM, 1)
                sl = slice((2 * pr + e) * LANES, (2 * pr + e + 1) * LANES)
                out_ref[:, sl] = jnp.where(lo, head * SCALE, 0.0).astype(BF16)
    gt_ref[...] = jax.nn.sigmoid(_dot(xb, w_ref[:, SEC_GATE:SEC_SW]))
    cmp_ref[...] = _dot(xb, w_ref[:, SEC_CMP:SEC_MQ])

    sw = _dot(xb, w_ref[:, SEC_SW:SEC_CMP])
    for i in range(4):
        sl = slice(i * LANES, (i + 1) * LANES)
        sw_ref[:, sl] = _rope(sw[:, sl], tabs["lo"]).astype(BF16)

    mq = _dot(xb, w_ref[:, SEC_MQ:SEC_MKV])
    for i in range(4):
        sl = slice(i * LANES, (i + 1) * LANES)
        mq_ref[:, sl] = (_rope(mq[:, sl], tabs["both"]) * SCALE).astype(BF16)

    mkv = _dot(xb, w_ref[:, SEC_MKV:IN_COLS_P])
    nblk = PROJ_ROWS // MOBA_BLOCK
    for h in range(MOBA_HEADS):
        sl = slice(h * LANES, (h + 1) * LANES)
        r = _rope(mkv[:, sl], tabs["lo" if h % 2 == 0 else "hi"])
        mkv_ref[:, sl] = r.astype(BF16)
        km_ref[:, 0, sl] = jnp.mean(r.reshape(nblk, MOBA_BLOCK, LANES), axis=1)


def _inproj(x2, g, w_p, layer, tabs, seq):
    n = x2.shape[0]
    tiles_per_seq = seq // PROJ_ROWS
    nblk = PROJ_ROWS // MOBA_BLOCK

    def rows(width):
        return pl.BlockSpec((PROJ_ROWS, width), lambda i: (i, 0))

    tab = pl.BlockSpec((PROJ_ROWS, LANES), lambda i: (i % tiles_per_seq, 0))
    out_shapes = (
        jax.ShapeDtypeStruct((n, 1024), BF16),
        jax.ShapeDtypeStruct((n, 1024), BF16),
        jax.ShapeDtypeStruct((n, 256), F32),
        jax.ShapeDtypeStruct((n, 512), BF16),
        jax.ShapeDtypeStruct((n, 256), F32),
        jax.ShapeDtypeStruct((n, 512), BF16),
        jax.ShapeDtypeStruct((n, 1024), BF16),
        jax.ShapeDtypeStruct((n // MOBA_BLOCK, 1, 1024), F32),
    )
    out_specs = (rows(1024), rows(1024), rows(256), rows(512), rows(256), rows(512), rows(1024),
                 pl.BlockSpec((nblk, 1, 1024), lambda i: (i, 0, 0)))
    return pl.pallas_call(
        _inproj_kernel,
        grid=(n // PROJ_ROWS,),
        in_specs=[rows(D_MODEL), _resident((1, D_MODEL)), _resident((D_MODEL, IN_COLS_P), layer),
                  tab, tab, tab],
        out_specs=out_specs,
        out_shape=out_shapes,
        compiler_params=_params(("parallel",)),
        name="inproj",
    )(x2, g.reshape(1, D_MODEL), w_p, *tabs)


def _compress_kernel(x_ref, pos_ref, wa_ref, wb_ref, w2_ref, o_ref):
    n_pad = x_ref.shape[1] // CMP_STRIDE
    width = 2 * CMP_HIDDEN
    ya = jnp.zeros((n_pad, width), F32)
    zb = jnp.zeros((n_pad, width), F32)
    for l in range(CMP_STRIDE):
        h = x_ref[0, pl.ds(l, n_pad, stride=CMP_STRIDE), :]
        ya = ya + _dot((h + pos_ref[0, l:l + 1, :]).astype(BF16), wa_ref[l])
        zb = zb + _dot((h + pos_ref[1, l:l + 1, :]).astype(BF16), wb_ref[l])
    pre = ya + pltpu.roll(zb, n_pad - 1, 0)
    out = _dot(jax.nn.gelu(pre).astype(BF16), w2_ref[...])
    row = lax.broadcasted_iota(jnp.int32, out.shape, 0)
    o_ref[0, 0] = jnp.where(row < n_pad - 1, out, 0.0).astype(BF16)


def _compress(cmp_in, pos, wa, wb, w2):
    b, seq = cmp_in.shape[:2]
    g = NSA_KV_HEADS
    n_pad = seq // CMP_STRIDE
    return pl.pallas_call(
        _compress_kernel,
        grid=(b, g),
        in_specs=[pl.BlockSpec((1, seq, LANES), lambda i, j: (i, 0, j)),
                  _resident(pos.shape), _resident(wa.shape), _resident(wb.shape), _resident(w2.shape)],
        out_specs=pl.BlockSpec((1, 1, n_pad, LANES), lambda i, j: (i, j, 0, 0)),
        out_shape=jax.ShapeDtypeStruct((b, g, n_pad, LANES), BF16),
        compiler_params=_params(("parallel", "parallel")),
        name="compress",
    )(cmp_in, pos, wa, wb, w2)


def _rank_below(s_t, n_rows, limit):
    t = s_t.shape[1]
    n_grp = n_rows // SUBLANES
    grp = [s_t[g * SUBLANES:(g + 1) * SUBLANES] for g in range(n_grp)]
    cnt = [jnp.zeros((SUBLANES, t), F32) for _ in range(n_grp)]
    sub = lax.broadcasted_iota(jnp.int32, (SUBLANES, t), 0)
    for i in range(n_rows):
        ri = s_t[i:i + 1, :]
        for g in range(n_grp):
            if g > i // SUBLANES:
                beats = ri >= grp[g]
            elif g < i // SUBLANES:
                beats = ri > grp[g]
            else:
                beats = (ri > grp[g]) | ((ri == grp[g]) & (sub > i % SUBLANES))
            cnt[g] = cnt[g] + jnp.where(beats, 1.0, 0.0)
    return jnp.concatenate(cnt, axis=0) < limit


def _block_bias(sel_t, t):
    pad = jnp.zeros((LANES - sel_t.shape[0], t), F32)
    sel = jnp.concatenate([sel_t, pad], axis=0).T
    return ((sel - 1.0) * MASK_BIAS).astype(BF16)


def _with_ones(kv):
    return jnp.concatenate([kv, jnp.ones(kv.shape, kv.dtype)], axis=1)


def _softmax_step(read_s, kv, m_ref, acc_ref):
    m_old = m_ref[...]
    m_new = jnp.maximum(m_old, jnp.broadcast_to(jnp.max(read_s(), axis=-1, keepdims=True), m_old.shape))
    m_ref[...] = m_new
    alpha = jnp.exp(m_old - m_new)
    s = read_s()
    p = jnp.exp(s - jnp.concatenate([m_new] * (s.shape[1] // LANES), axis=1)).astype(BF16)
    acc_ref[...] = jnp.concatenate([alpha, alpha], axis=1) * acc_ref[...] + _dot(p, _with_ones(kv))


def _put_scores(q_ext, keys, c, s_ref):
    kvs, onehot = keys(c)
    for h, (q, kv) in enumerate(zip(q_ext, kvs)):
        s_ref[h] = _dot_nt(q, jnp.concatenate([kv, onehot], axis=1))


def _attend(q_ext, keys, n_past, causal, m_ref, acc_ref, s_a, s_b):
    def put_scores(c, s_ref):
        _put_scores(q_ext, keys, c, s_ref)

    def softmax(c, s_ref, mask=None):
        for h, kv in enumerate(keys(c)[0]):
            if mask is None:
                read_s = functools.partial(lambda h: s_ref[h], h)
            else:
                read_s = functools.partial(lambda s: s, jnp.where(mask, s_ref[h], -MASK_BIAS))
            _softmax_step(read_s, kv, m_ref.at[h], acc_ref.at[h])

    def body(i, carry):
        c = 2 * i
        put_scores(c + 1, s_b)
        softmax(c, s_a)
        put_scores(c + 2, s_a)
        softmax(c + 1, s_b)
        return carry

    lax.fori_loop(0, n_past // 2, body, 0)

    @pl.when(n_past % 2 == 1)
    def _():
        put_scores(n_past, s_b)
        softmax(n_past - 1, s_a)
        softmax(n_past, s_b, causal)

    @pl.when(n_past % 2 == 0)
    def _():
        softmax(n_past, s_a, causal)


def _softmax_finish(acc):
    return acc[:, :LANES] / jnp.maximum(acc[:, LANES:], 1.0)


def _nsa_kernel(qp_ref, qr_ref, gt_ref, wb0_ref, wb1_ref, kvc_ref, sw_ref, ov_ref, oh_ref,
                o_ref, m_sc, acc_sc, sa_sc, sb_sc, qx_sc, part_sc, gslc_sc):
    wb_refs = (wb0_ref, wb1_ref)
    tq, r_heads, groups = NSA_TQ, NSA_GROUP, NSA_KV_HEADS
    rows = r_heads * tq
    q0 = pl.program_id(1) * tq
    lo = lax.broadcasted_iota(jnp.int32, (tq, LANES), 1) < HEAD_DIM
    n_slc = sw_ref.shape[1] // SLC_BLOCK
    blk_shift = SLC_BLOCK.bit_length() - 1

    def slab(i):
        return slice(i * LANES, (i + 1) * LANES)

    def head(o, r):
        return o[r * tq:(r + 1) * tq]

    def front(g, nb):
        def stack(ref):
            return jnp.concatenate([ref[0, :, slab(r_heads * g + r)] for r in range(r_heads)], axis=0)

        q_plain = stack(qp_ref)
        q_rot = stack(qr_ref)

        half = tq // 2
        o_win = []
        for u, wb_ref in enumerate(wb_refs):
            t0 = q0 + u * half
            w0 = pl.multiple_of(jnp.maximum(t0 - WINDOW, 0), half)
            kvw = sw_ref[0, pl.ds(w0, WINDOW + half), slab(2 * g + 1)]
            q_u = jnp.concatenate([qr_ref[0, u * half:(u + 1) * half, slab(r_heads * g + r)]
                                   for r in range(r_heads)], axis=0)
            s = _dot_nt(q_u, kvw).reshape(r_heads, half, WINDOW + half) + wb_ref[...][None]
            p = jnp.exp(s - jnp.max(s, axis=-1, keepdims=True)).reshape(r_heads * half, WINDOW + half)
            o_win.append(_softmax_finish(_dot(p.astype(BF16), _with_ones(kvw))))

        def win_head(r):
            return jnp.concatenate([o[r * half:(r + 1) * half] for o in o_win], axis=0)

        n_pad = min(kvc_ref.shape[2], -(-nb * (SLC_BLOCK // CMP_STRIDE) // LANES) * LANES)
        kvc = kvc_ref[0, g, :n_pad, :]
        n_idx = lax.broadcasted_iota(jnp.int32, (tq, n_pad), 1)
        t_idx = q0 + lax.broadcasted_iota(jnp.int32, (tq, n_pad), 0)
        valid_c = (n_idx * CMP_STRIDE + CMP_LEN - 1 <= t_idx)[None]
        s = jnp.where(valid_c, _dot_nt(q_plain, kvc).reshape(r_heads, tq, n_pad), NEG)
        m = jnp.maximum(jnp.max(s, axis=-1, keepdims=True), NEG / 2)
        p = jnp.exp(s - m)
        p = p / jnp.maximum(jnp.sum(p, axis=-1, keepdims=True), 1.0)
        o_cmp = _dot(p.reshape(rows, n_pad).astype(BF16), kvc)

        p_sum = p[0] + p[1] + p[2] + p[3]
        ov = ov_ref[:n_pad, :]
        hi = p_sum.astype(BF16)
        r1 = p_sum - hi.astype(F32)
        mid = r1.astype(BF16)
        low = (r1 - mid.astype(F32)).astype(BF16)
        imp = _dot(hi, ov) + _dot(mid, ov) + _dot(low, ov)

        j_idx = lax.broadcasted_iota(jnp.int32, (tq, LANES), 1)
        t_row = q0 + lax.broadcasted_iota(jnp.int32, (tq, LANES), 0)
        qblk = jnp.right_shift(t_row, blk_shift)
        forced = (j_idx == 0) | (j_idx == qblk) | (j_idx == qblk - 1)
        visible = j_idx * SLC_BLOCK <= t_row
        score = jnp.where(visible, jnp.where(forced, FORCE_BONUS, imp), NEG)
        if nb <= SLC_TOPN:
            sel_t = jnp.ones((nb, tq), F32)
        else:
            sel_t = _rank_below(score.T[:nb], nb, SLC_TOPN).astype(F32)
        bias = _block_bias(sel_t, tq)

        gate = gt_ref[0, :, slab(g)]

        def gate_of(r, branch):
            return jnp.broadcast_to(gate[:, 3 * r + branch:3 * r + branch + 1], (tq, LANES))

        for r in range(r_heads):
            part_sc[g, r] = gate_of(r, 0) * head(o_cmp, r) + gate_of(r, 2) * win_head(r)
            gslc_sc[g, r] = gate_of(r, 1)
        q_ext = jnp.concatenate([q_rot, jnp.concatenate([bias] * r_heads, axis=0)], axis=1)
        qx_sc[g] = q_ext
        return q_ext

    def slc_keys(c):
        ks = pl.ds(pl.multiple_of(c * NSA_KC, NSA_KC), NSA_KC)
        return tuple(sw_ref[0, ks, slab(2 * g)] for g in range(groups)), oh_ref[ks, :]

    bucket = (q0 + tq - 1) // (SLC_TOPN * SLC_BLOCK)
    for b, nb in enumerate(range(SLC_TOPN, n_slc + 1, SLC_TOPN)):
        @pl.when(bucket == b)
        def _(nb=nb):
            q_ext = [front(g, nb) for g in range(groups)]
            m_sc[...] = jnp.full(m_sc.shape, NEG, F32)
            acc_sc[...] = jnp.zeros(acc_sc.shape, F32)
            _put_scores(q_ext, slc_keys, 0, sa_sc)

    c_last = q0 // NSA_KC
    kpos = c_last * NSA_KC + lax.broadcasted_iota(jnp.int32, (rows, NSA_KC), 1)
    tpos = q0 + (lax.broadcasted_iota(jnp.int32, (rows, NSA_KC), 0) & (tq - 1))
    _attend([qx_sc[g] for g in range(groups)], slc_keys, c_last, kpos <= tpos, m_sc, acc_sc, sa_sc, sb_sc)

    for g in range(groups):
        o_slc = _softmax_finish(acc_sc[g])
        outs = [part_sc[g, r] + gslc_sc[g, r] * head(o_slc, r) for r in range(r_heads)]
        for pr in range(r_heads // 2):
            pair = jnp.where(lo, pltpu.roll(outs[2 * pr], HEAD_DIM, 1), outs[2 * pr + 1])
            o_ref[0, :, slab(2 * g + pr)] = pair.astype(BF16)


def _window_bias(seq, half):
    offsets = sorted({t0 - max(t0 - WINDOW, 0) for t0 in range(0, seq, half)})
    i = np.arange(half)[:, None]
    c = np.arange(WINDOW + half)[None, :]
    tabs = []
    for off in offsets:
        d = off + i - c
        tabs.append(np.where((d >= 0) & (d < WINDOW), 0.0, NEG))
    return np.stack(tabs).astype(np.float32)


def _nsa(qp, qr, gt, kvc, sw, ov):
    b, seq = qp.shape[:2]
    groups = NSA_KV_HEADS
    tq = NSA_TQ
    rows = NSA_GROUP * tq
    n_pad = kvc.shape[2]
    half = tq // 2
    wb = jnp.asarray(_window_bias(seq, half))
    last = wb.shape[0] - 1

    def band(u):
        return pl.BlockSpec((None, half, WINDOW + half), lambda i, k: (jnp.minimum(2 * k + u, last), 0, 0))

    return pl.pallas_call(
        _nsa_kernel,
        grid=(b, seq // tq),
        in_specs=[
            pl.BlockSpec((1, tq, NSA_HEADS * LANES), lambda i, k: (i, k, 0)),
            pl.BlockSpec((1, tq, NSA_HEADS * LANES), lambda i, k: (i, k, 0)),
            pl.BlockSpec((1, tq, groups * LANES), lambda i, k: (i, k, 0)),
            band(0), band(1),
            pl.BlockSpec((1, groups, n_pad, LANES), lambda i, k: (i, 0, 0, 0)),
            pl.BlockSpec((1, seq, 2 * groups * LANES), lambda i, k: (i, 0, 0)),
            pl.BlockSpec((n_pad, LANES), lambda i, k: (0, 0)),
            pl.BlockSpec((seq, LANES), lambda i, k: (0, 0)),
        ],
        out_specs=pl.BlockSpec((1, tq, NSA_Q_COLS), lambda i, k: (i, k, 0)),
        out_shape=jax.ShapeDtypeStruct((b, seq, NSA_Q_COLS), BF16),
        scratch_shapes=[pltpu.VMEM((groups, rows, LANES), F32), pltpu.VMEM((groups, rows, 2 * LANES), F32),
                        pltpu.VMEM((groups, rows, NSA_KC), F32), pltpu.VMEM((groups, rows, NSA_KC), F32),
                        pltpu.VMEM((groups, rows, 2 * LANES), BF16),
                        pltpu.VMEM((groups, NSA_GROUP, tq, LANES), F32), pltpu.VMEM((groups, NSA_GROUP, tq, LANES), F32)],
        compiler_params=_params(("parallel", "arbitrary")),
        name="nsa",
    )(qp, qr, gt, wb, wb, kvc, sw, ov, _block_onehot(seq, SLC_BLOCK))


def _moba_kernel(q_ref, kv_ref, km_ref, oh_ref, o_ref, m_sc, acc_sc, sa_sc, sb_sc):
    tq = MOBA_TQ
    heads = MOBA_HEADS_PER_STEP
    q0 = pl.multiple_of(pl.program_id(2) * tq, tq)
    lo = lax.broadcasted_iota(jnp.int32, (tq, LANES), 1) < HEAD_DIM
    n_blk = km_ref.shape[1]
    blk_shift = MOBA_BLOCK.bit_length() - 1
    blk = lax.broadcasted_iota(jnp.int32, (n_blk, tq), 0)
    qblk = jnp.right_shift(q0 + lax.broadcasted_iota(jnp.int32, (n_blk, tq), 1), blk_shift)
    past = blk < qblk

    def slab(h):
        return slice(h * LANES, (h + 1) * LANES)

    q_ext = []
    for h in range(heads):
        q = jnp.where(lo if h % 2 == 0 else ~lo, q_ref[0, :, slab(h // 2)], jnp.zeros((), BF16))
        gate_t = _dot_nt(km_ref[0, :, slab(h)].astype(BF16), q)
        top = _rank_below(jnp.where(past, gate_t, NEG), n_blk, MOBA_TOPK)
        bias = _block_bias(((top & past) | (blk == qblk)).astype(F32), tq)
        q_ext.append(jnp.concatenate([q, bias], axis=1))

    m_sc[...] = jnp.full(m_sc.shape, NEG, F32)
    acc_sc[...] = jnp.zeros(acc_sc.shape, F32)

    def keys(c):
        ks = pl.ds(pl.multiple_of(c * tq, tq), tq)
        return tuple(kv_ref[0, ks, slab(h)] for h in range(heads)), oh_ref[ks, :]

    causal = lax.broadcasted_iota(jnp.int32, (tq, tq), 1) <= lax.broadcasted_iota(jnp.int32, (tq, tq), 0)
    _put_scores(q_ext, keys, 0, sa_sc)
    _attend(q_ext, keys, pl.program_id(2), causal, m_sc, acc_sc, sa_sc, sb_sc)

    for pr in range(heads // 2):
        even = _softmax_finish(acc_sc[2 * pr])
        odd = _softmax_finish(acc_sc[2 * pr + 1])
        o_ref[0, :, slab(pr)] = jnp.where(lo, odd, even).astype(BF16)


def _moba(mq, mkv, km):
    b, seq = mq.shape[:2]
    tq = MOBA_TQ
    n_blk = seq // MOBA_BLOCK
    heads = MOBA_HEADS_PER_STEP
    return pl.pallas_call(
        _moba_kernel,
        grid=(b, MOBA_HEADS // heads, seq // tq),
        in_specs=[
            pl.BlockSpec((1, tq, heads // 2 * LANES), lambda i, j, k: (i, k, j)),
            pl.BlockSpec((1, seq, heads * LANES), lambda i, j, k: (i, 0, j)),
            pl.BlockSpec((1, n_blk, heads * LANES), lambda i, j, k: (i, 0, j)),
            pl.BlockSpec((seq, LANES), lambda i, j, k: (0, 0)),
        ],
        out_specs=pl.BlockSpec((1, tq, heads // 2 * LANES), lambda i, j, k: (i, k, j)),
        out_shape=jax.ShapeDtypeStruct((b, seq, MOBA_HEADS * HEAD_DIM), BF16),
        scratch_shapes=[pltpu.VMEM((heads, tq, LANES), F32), pltpu.VMEM((heads, tq, 2 * LANES), F32),
                        pltpu.VMEM((heads, tq, tq), F32), pltpu.VMEM((heads, tq, tq), F32)],
        compiler_params=_params(("parallel", "parallel", "arbitrary")),
        name="moba",
    )(mq, mkv, km, _block_onehot(seq, MOBA_BLOCK))


def _in_perm():
    d = HEAD_DIM
    perm = np.full((IN_COLS_P,), -1, np.int64)
    perm[SEC_Q:SEC_Q + NSA_Q_COLS] = np.arange(NSA_Q_COLS)
    kv0 = NSA_Q_COLS
    gate0 = kv0 + NSA_KV_COLS
    moba0 = gate0 + NSA_GATE_COLS
    per_group = NSA_GROUP * 3
    for g in range(NSA_KV_HEADS):
        perm[SEC_GATE + g * LANES:SEC_GATE + g * LANES + per_group] = gate0 + g * per_group + np.arange(per_group)

    def kv_cols(which, g):
        return kv0 + (which * NSA_KV_HEADS + g) * d + np.arange(d)

    for g in range(NSA_KV_HEADS):
        base = SEC_SW + g * 2 * LANES
        perm[base:base + d] = kv_cols(2, g)
        perm[base + d:base + 2 * d] = kv_cols(3, g)
        perm[base + 2 * d:base + 3 * d] = kv_cols(4, g)
        perm[base + 3 * d:base + 4 * d] = kv_cols(5, g)
        base = SEC_CMP + g * LANES
        perm[base:base + d] = kv_cols(0, g)
        perm[base + d:base + 2 * d] = kv_cols(1, g)
    perm[SEC_MQ:SEC_MQ + MOBA_HEADS * d] = moba0 + np.arange(MOBA_HEADS * d)
    for h in range(MOBA_HEADS):
        k_cols = moba0 + (MOBA_HEADS + h) * d + np.arange(d)
        v_cols = moba0 + (2 * MOBA_HEADS + h) * d + np.arange(d)
        base = SEC_MKV + h * LANES
        first, second = (k_cols, v_cols) if h % 2 == 0 else (v_cols, k_cols)
        perm[base:base + d] = first
        perm[base + d:base + 2 * d] = second
    return perm


def _take_runs(w, idx, axis):
    parts, i = [], 0
    while i < len(idx):
        j = i + 1
        if idx[i] < 0:
            while j < len(idx) and idx[j] < 0:
                j += 1
            shape = list(w.shape)
            shape[axis] = j - i
            parts.append(jnp.zeros(shape, w.dtype))
        else:
            while j < len(idx) and idx[j] == idx[j - 1] + 1:
                j += 1
            parts.append(lax.slice_in_dim(w, int(idx[i]), int(idx[j - 1]) + 1, axis=axis))
        i = j
    return jnp.concatenate(parts, axis=axis)


def _moba_out_rows():
    d = HEAD_DIM
    rows = []
    for pr in range(MOBA_HEADS // 2):
        rows.append(NSA_Q_COLS + (2 * pr + 1) * d + np.arange(d))
        rows.append(NSA_Q_COLS + (2 * pr) * d + np.arange(d))
    return np.concatenate(rows)


def _overlap_table(seq):
    n_pad = seq // CMP_STRIDE
    n = np.arange(n_pad)[:, None]
    j = np.arange(LANES)[None, :]
    starts = n * CMP_STRIDE
    sb = j * SLC_BLOCK
    ov = (starts < sb + SLC_BLOCK) & (starts + CMP_LEN > sb) & (n < n_pad - 1) & (j < seq // SLC_BLOCK)
    return ov.astype(np.float32)


def _block_onehot(seq, block):
    onehot = np.arange(seq)[:, None] // block == np.arange(LANES)[None, :]
    return jnp.asarray(onehot.astype(np.float32), BF16)


def _rope_lane_tables(seq):
    half = ROT_DIM // 2
    inv_freq = jnp.power(ROPE_THETA, -(jnp.arange(0, ROT_DIM, 2, dtype=F32) / ROT_DIM))
    ang = jnp.arange(seq, dtype=F32)[:, None] * inv_freq[None, :]
    cos, sin = jnp.cos(ang), jnp.sin(ang)
    rest = HEAD_DIM - ROT_DIM
    zeros_h = jnp.zeros((seq, half), F32)
    c = jnp.concatenate([cos, cos, jnp.ones((seq, rest), F32)], axis=1)
    sl = jnp.concatenate([-sin, zeros_h, jnp.zeros((seq, rest), F32)], axis=1)
    sr = jnp.concatenate([zeros_h, sin, jnp.zeros((seq, rest), F32)], axis=1)
    return tuple(jnp.tile(t, (1, LANES // HEAD_DIM)) for t in (c, sl, sr))


def _mixer(x2, b, seq, g_mix, w_p, layer, pos_ck, w_ck1, w_ck2, pos_cv, w_cv1, w_cv2, tabs, ov):
    qp, qr, gt, sw, cmp_in, mq, mkv, km = _inproj(x2, g_mix, w_p, layer, tabs, seq)

    def halves(w_k, w_v):
        wk = w_k.reshape(2, CMP_STRIDE, HEAD_DIM, CMP_HIDDEN)
        wv = w_v.reshape(2, CMP_STRIDE, HEAD_DIM, CMP_HIDDEN)
        z = jnp.zeros_like(wk)
        return jnp.concatenate([jnp.concatenate([wk, z], axis=3), jnp.concatenate([z, wv], axis=3)],
                               axis=2).astype(BF16)

    w1 = halves(w_ck1, w_cv1)
    pos = jnp.concatenate([pos_ck, pos_cv], axis=1).reshape(2, CMP_STRIDE, LANES)
    z2 = jnp.zeros_like(w_ck2)
    w2 = jnp.concatenate([jnp.concatenate([w_ck2, z2], axis=1),
                          jnp.concatenate([z2, w_cv2], axis=1)], axis=0).astype(BF16)
    kvc = _compress(cmp_in.reshape(b, seq, -1), pos, w1[0], w1[1], w2)

    o_nsa = _nsa(qp.reshape(b, seq, -1), qr.reshape(b, seq, -1), gt.reshape(b, seq, -1), kvc,
                 sw.reshape(b, seq, -1), ov)
    o_moba = _moba(mq.reshape(b, seq, -1), mkv.reshape(b, seq, -1), km.reshape(b, seq // MOBA_BLOCK, -1))

    n = b * seq
    return o_nsa.reshape(n, -1), o_moba.reshape(n, -1)


def kernel(x, norm_ffn1, w_ffn1_gate, w_ffn1_up, w_ffn1_down, norm_mix, w_in, pos_ck, w_ck1, w_ck2,
           pos_cv, w_cv1, w_cv2, w_out, norm_ffn2, w_ffn2_gate, w_ffn2_up, w_ffn2_down, norm_final):
    b, seq, d = x.shape
    assert d == D_MODEL and seq % PROJ_ROWS == 0 and SLC_TOPN <= seq // SLC_BLOCK <= LANES
    depth = norm_ffn1.shape[0]
    tabs = _rope_lane_tables(seq)
    ov = jnp.asarray(_overlap_table(seq), BF16)
    x2 = x.reshape(b * seq, d)
    w_p = _take_runs(w_in.astype(BF16), _in_perm(), axis=2)
    w_out_b = w_out.astype(BF16)
    wa = w_out_b[:, :NSA_Q_COLS]
    wm = _take_runs(w_out_b, _moba_out_rows(), axis=1)
    wg1, wu1, wd1, wg2, wu2, wd2 = (w.astype(BF16) for w in (w_ffn1_gate, w_ffn1_up, w_ffn1_down,
                                                            w_ffn2_gate, w_ffn2_up, w_ffn2_down))
    for l in range(depth):
        x2 = _ffn(x2, norm_ffn1[l], wg1, wu1, wd1, l)
        o_nsa, o_moba = _mixer(x2, b, seq, norm_mix[l], w_p, l, pos_ck[l], w_ck1[l], w_ck2[l],
                               pos_cv[l], w_cv1[l], w_cv2[l], tabs, ov)
        x2 = _ffn(x2, norm_ffn2[l], wg2, wu2, wd2, l, mix=(o_nsa, o_moba, wa[l], wm[l]),
                  g_final=norm_final if l == depth - 1 else None)
    return x2.reshape(b, seq, d)
```

```python
import functools

import numpy as np
import jax
import jax.numpy as jnp
from jax import lax
from jax.experimental import pallas as pl
from jax.experimental.pallas import tpu as pltpu

D_MODEL = 1024
HEAD_DIM = 64
NSA_HEADS = 8
NSA_KV_HEADS = 2
NSA_GROUP = NSA_HEADS // NSA_KV_HEADS
MOBA_HEADS = 8
ROT_DIM = HEAD_DIM // 4
ROPE_THETA = 500000.0
CMP_LEN = 32
CMP_STRIDE = 16
CMP_HIDDEN = 256
SLC_BLOCK = 64
SLC_TOPN = 16
WINDOW = 512
MOBA_BLOCK = 256
MOBA_TOPK = 3
D_FF = 2816
EPS = 1e-6
NEG = -1e30
FORCE_BONUS = 1e4
SCALE = HEAD_DIM ** -0.5

NSA_Q_COLS = NSA_HEADS * HEAD_DIM
NSA_KV_COLS = 3 * 2 * NSA_KV_HEADS * HEAD_DIM
NSA_GATE_COLS = NSA_HEADS * 3
MOBA_COLS = 3 * MOBA_HEADS * HEAD_DIM

LANES = 128
SUBLANES = 8
VMEM_LIMIT = 56 * 1024 * 1024

SEC_Q = 0
SEC_GATE = 512
SEC_SW = 768
SEC_CMP = 1280
SEC_MQ = 1536
SEC_MKV = 2048
IN_COLS_P = 3072

F32 = jnp.float32
BF16 = jnp.bfloat16

FFN_ROWS = 1024
FFN_CHUNK = 256
PROJ_ROWS = 512
NSA_TQ = 256
NSA_KC = 512
MOBA_HEADS_PER_STEP = 4
MOBA_TQ = 2 * MOBA_BLOCK
MASK_BIAS = 2.0 ** 100


def _dot(a, b):
    return jnp.dot(a, b, preferred_element_type=F32)


def _dot_nt(a, b):
    return lax.dot_general(a, b, (((1,), (1,)), ((), ())), preferred_element_type=F32)


def _rms(x, g):
    return x * lax.rsqrt(jnp.mean(x * x, axis=-1, keepdims=True) + EPS) * g


def _params(sem):
    return pltpu.CompilerParams(dimension_semantics=sem, vmem_limit_bytes=VMEM_LIMIT)


def _resident(shape, layer=None):
    nd = len(shape)
    if layer is None:
        return pl.BlockSpec(shape, lambda *_: (0,) * nd, pipeline_mode=pl.Buffered(1))
    return pl.BlockSpec((None,) + tuple(shape), lambda *_: (layer,) + (0,) * nd,
                        pipeline_mode=pl.Buffered(1))


def _ffn_kernel(*refs, mix, final_norm):
    refs = list(refs)
    x_ref = refs.pop(0)
    if mix:
        a_ref, mo_ref, wa_ref, wm_ref = (refs.pop(0) for _ in range(4))
    g_ref, wg_ref, wu_ref, wd_ref = (refs.pop(0) for _ in range(4))
    gf_ref = refs.pop(0) if final_norm else None
    o_ref, h_ref = refs
    x = x_ref[...]
    if mix:
        x = x + _dot(a_ref[...], wa_ref[...]) + _dot(mo_ref[...], wm_ref[...])
    xb = _rms(x, g_ref[...]).astype(BF16)
    for c in range(D_FF // FFN_CHUNK):
        sl = slice(c * FFN_CHUNK, (c + 1) * FFN_CHUNK)
        gate = _dot(xb, wg_ref[:, sl])
        up = _dot(xb, wu_ref[:, sl])
        h_ref[:, sl] = (jax.nn.silu(gate) * up).astype(BF16)
    y = x + 0.5 * _dot(h_ref[...], wd_ref[...])
    if final_norm:
        y = _rms(y, gf_ref[...])
    o_ref[...] = y


def _ffn(x2, g, wg, wu, wd, layer, mix=None, g_final=None):
    n = x2.shape[0]
    final_norm = g_final is not None

    def rows(width):
        return pl.BlockSpec((FFN_ROWS, width), lambda i: (i, 0))

    in_specs, args = [rows(D_MODEL)], [x2]
    if mix is not None:
        o_nsa, o_moba, wa, wm = mix
        in_specs += [rows(o_nsa.shape[1]), rows(o_moba.shape[1]), _resident(wa.shape), _resident(wm.shape)]
        args += [o_nsa, o_moba, wa, wm]
    in_specs += [_resident((1, D_MODEL)), _resident((D_MODEL, D_FF), layer),
                 _resident((D_MODEL, D_FF), layer), _resident((D_FF, D_MODEL), layer)]
    args += [g.reshape(1, D_MODEL), wg, wu, wd]
    if final_norm:
        in_specs.append(_resident((1, D_MODEL)))
        args.append(g_final.reshape(1, D_MODEL))
    return pl.pallas_call(
        functools.partial(_ffn_kernel, mix=mix is not None, final_norm=final_norm),
        grid=(n // FFN_ROWS,),
        in_specs=in_specs,
        out_specs=rows(D_MODEL),
        out_shape=jax.ShapeDtypeStruct((n, D_MODEL), F32),
        scratch_shapes=[pltpu.VMEM((FFN_ROWS, D_FF), BF16)],
        compiler_params=_params(("parallel",)),
        name="ffn" + ("_mix" if mix is not None else "") + ("_final" if final_norm else ""),
    )(*args)


def _rope_tables(c, sl, sr, mode):
    if mode == "both":
        return c, sl, sr
    lane = lax.broadcasted_iota(jnp.int32, c.shape, 1)
    keep = lane < HEAD_DIM if mode == "lo" else lane >= HEAD_DIM
    return jnp.where(keep, c, 1.0), jnp.where(keep, sl, 0.0), jnp.where(keep, sr, 0.0)


def _rope(v, tabs):
    c, sl, sr = tabs
    return v * c + pltpu.roll(v, LANES - ROT_DIM // 2, 1) * sl + pltpu.roll(v, ROT_DIM // 2, 1) * sr


def _inproj_kernel(x_ref, g_ref, w_ref, c_ref, sl_ref, sr_ref,
                   qp_ref, qr_ref, gt_ref, sw_ref, cmp_ref, mq_ref, mkv_ref, km_ref):
    xb = _rms(x_ref[...], g_ref[...]).astype(BF16)
    base = (c_ref[...], sl_ref[...], sr_ref[...])
    tabs = {m: _rope_tables(*base, m) for m in ("both", "lo", "hi")}

    qn = _dot(xb, w_ref[:, SEC_Q:SEC_GATE])
    lo = lax.broadcasted_iota(jnp.int32, (PROJ_ROWS, LANES), 1) < HEAD_DIM
    for pr in range(NSA_HEADS // 2):
        pair = qn[:, pr * LANES:(pr + 1) * LANES]
        for out_ref, v in ((qp_ref, pair), (qr_ref, _rope(pair, tabs["both"]))):
            for e in range(2):
                head = v if e == 0 else pltpu.roll(v, HEAD_DIM, 1)
                sl = slice((2 * pr + e) * LANES, (2 * pr + e + 1) * LANES)
                out_ref[:, sl] = jnp.where(lo, head * SCALE, 0.0).astype(BF16)
    gt_ref[...] = jax.nn.sigmoid(_dot(xb, w_ref[:, SEC_GATE:SEC_SW]))
    cmp_ref[...] = _dot(xb, w_ref[:, SEC_CMP:SEC_MQ])

    sw = _dot(xb, w_ref[:, SEC_SW:SEC_CMP])
    for i in range(4):
        sl = slice(i * LANES, (i + 1) * LANES)
        sw_ref[:, sl] = _rope(sw[:, sl], tabs["lo"]).astype(BF16)

    mq = _dot(xb, w_ref[:, SEC_MQ:SEC_MKV])
    for i in range(4):
        sl = slice(i * LANES, (i + 1) * LANES)
        mq_ref[:, sl] = (_rope(mq[:, sl], tabs["both"]) * SCALE).astype(BF16)

    mkv = _dot(xb, w_ref[:, SEC_MKV:IN_COLS_P])
    nblk = PROJ_ROWS // MOBA_BLOCK
    for h in range(MOBA_HEADS):
        sl = slice(h * LANES, (h + 1) * LANES)
        r = _rope(mkv[:, sl], tabs["lo" if h % 2 == 0 else "hi"])
        mkv_ref[:, sl] = r.astype(BF16)
        km_ref[:, 0, sl] = jnp.mean(r.reshape(nblk, MOBA_BLOCK, LANES), axis=1)


def _inproj(x2, g, w_p, layer, tabs, seq):
    n = x2.shape[0]
    tiles_per_seq = seq // PROJ_ROWS
    nblk = PROJ_ROWS // MOBA_BLOCK

    def rows(width):
        return pl.BlockSpec((PROJ_ROWS, width), lambda i: (i, 0))

    tab = pl.BlockSpec((PROJ_ROWS, LANES), lambda i: (i % tiles_per_seq, 0))
    out_shapes = (
        jax.ShapeDtypeStruct((n, 1024), BF16),
        jax.ShapeDtypeStruct((n, 1024), BF16),
        jax.ShapeDtypeStruct((n, 256), F32),
        jax.ShapeDtypeStruct((n, 512), BF16),
        jax.ShapeDtypeStruct((n, 256), F32),
        jax.ShapeDtypeStruct((n, 512), BF16),
        jax.ShapeDtypeStruct((n, 1024), BF16),
        jax.ShapeDtypeStruct((n // MOBA_BLOCK, 1, 1024), F32),
    )
    out_specs = (rows(1024), rows(1024), rows(256), rows(512), rows(256), rows(512), rows(1024),
                 pl.BlockSpec((nblk, 1, 1024), lambda i: (i, 0, 0)))
    return pl.pallas_call(
        _inproj_kernel,
        grid=(n // PROJ_ROWS,),
        in_specs=[rows(D_MODEL), _resident((1, D_MODEL)), _resident((D_MODEL, IN_COLS_P), layer),
                  tab, tab, tab],
        out_specs=out_specs,
        out_shape=out_shapes,
        compiler_params=_params(("parallel",)),
        name="inproj",
    )(x2, g.reshape(1, D_MODEL), w_p, *tabs)


def _compress_kernel(x_ref, pos_ref, wa_ref, wb_ref, w2_ref, o_ref):
    n_pad = x_ref.shape[1] // CMP_STRIDE
    width = 2 * CMP_HIDDEN
    ya = jnp.zeros((n_pad, width), F32)
    zb = jnp.zeros((n_pad, width), F32)
    for l in range(CMP_STRIDE):
        h = x_ref[0, pl.ds(l, n_pad, stride=CMP_STRIDE), :]
        ya = ya + _dot((h + pos_ref[0, l:l + 1, :]).astype(BF16), wa_ref[l])
        zb = zb + _dot((h + pos_ref[1, l:l + 1, :]).astype(BF16), wb_ref[l])
    pre = ya + pltpu.roll(zb, n_pad - 1, 0)
    out = _dot(jax.nn.gelu(pre).astype(BF16), w2_ref[...])
    row = lax.broadcasted_iota(jnp.int32, out.shape, 0)
    o_ref[0, 0] = jnp.where(row < n_pad - 1, out, 0.0).astype(BF16)


def _compress(cmp_in, pos, wa, wb, w2):
    b, seq = cmp_in.shape[:2]
    g = NSA_KV_HEADS
    n_pad = seq // CMP_STRIDE
    return pl.pallas_call(
        _compress_kernel,
        grid=(b, g),
        in_specs=[pl.BlockSpec((1, seq, LANES), lambda i, j: (i, 0, j)),
                  _resident(pos.shape), _resident(wa.shape), _resident(wb.shape), _resident(w2.shape)],
        out_specs=pl.BlockSpec((1, 1, n_pad, LANES), lambda i, j: (i, j, 0, 0)),
        out_shape=jax.ShapeDtypeStruct((b, g, n_pad, LANES), BF16),
        compiler_params=_params(("parallel", "parallel")),
        name="compress",
    )(cmp_in, pos, wa, wb, w2)


def _rank_below(s_t, n_rows, limit):
    t = s_t.shape[1]
    n_grp = n_rows // SUBLANES
    grp = [s_t[g * SUBLANES:(g + 1) * SUBLANES] for g in range(n_grp)]
    cnt = [jnp.zeros((SUBLANES, t), F32) for _ in range(n_grp)]
    sub = lax.broadcasted_iota(jnp.int32, (SUBLANES, t), 0)
    for i in range(n_rows):
        ri = s_t[i:i + 1, :]
        for g in range(n_grp):
            if g > i // SUBLANES:
                beats = ri >= grp[g]
            elif g < i // SUBLANES:
                beats = ri > grp[g]
            else:
                beats = (ri > grp[g]) | ((ri == grp[g]) & (sub > i % SUBLANES))
            cnt[g] = cnt[g] + jnp.where(beats, 1.0, 0.0)
    return jnp.concatenate(cnt, axis=0) < limit


def _block_bias(sel_t, t):
    pad = jnp.zeros((LANES - sel_t.shape[0], t), F32)
    sel = jnp.concatenate([sel_t, pad], axis=0).T
    return ((sel - 1.0) * MASK_BIAS).astype(BF16)


def _with_ones(kv):
    return jnp.concatenate([kv, jnp.ones(kv.shape, kv.dtype)], axis=1)


def _softmax_step(read_s, kv, m_ref, acc_ref):
    m_old = m_ref[...]
    m_new = jnp.maximum(m_old, jnp.broadcast_to(jnp.max(read_s(), axis=-1, keepdims=True), m_old.shape))
    m_ref[...] = m_new
    alpha = jnp.exp(m_old - m_new)
    s = read_s()
    p = jnp.exp(s - jnp.concatenate([m_new] * (s.shape[1] // LANES), axis=1)).astype(BF16)
    acc_ref[...] = jnp.concatenate([alpha, alpha], axis=1) * acc_ref[...] + _dot(p, _with_ones(kv))


def _put_scores(q_ext, keys, c, s_ref):
    kvs, onehot = keys(c)
    for h, (q, kv) in enumerate(zip(q_ext, kvs)):
        s_ref[h] = _dot_nt(q, jnp.concatenate([kv, onehot], axis=1))


def _attend(q_ext, keys, n_past, causal, m_ref, acc_ref, s_a, s_b):
    put_scores = functools.partial(_put_scores, q_ext, keys)

    def softmax(c, s_ref, mask=None):
        for h, kv in enumerate(keys(c)[0]):
            if mask is None:
                read_s = functools.partial(lambda h: s_ref[h], h)
            else:
                read_s = functools.partial(lambda s: s, jnp.where(mask, s_ref[h], -MASK_BIAS))
            _softmax_step(read_s, kv, m_ref.at[h], acc_ref.at[h])

    def body(i, carry):
        c = 2 * i
        put_scores(c + 1, s_b)
        softmax(c, s_a)
        put_scores(c + 2, s_a)
        softmax(c + 1, s_b)
        return carry

    lax.fori_loop(0, n_past // 2, body, 0)

    @pl.when(n_past % 2 == 1)
    def _():
        put_scores(n_past, s_b)
        softmax(n_past - 1, s_a)
        softmax(n_past, s_b, causal)

    @pl.when(n_past % 2 == 0)
    def _():
        softmax(n_past, s_a, causal)


def _softmax_finish(acc):
    return acc[:, :LANES] / jnp.maximum(acc[:, LANES:], 1.0)


def _nsa_kernel(qp_ref, qr_ref, gt_ref, wb0_ref, wb1_ref, kvc_ref, sw_ref, ov_ref, oh_ref,
                o_ref, m_sc, acc_sc, sa_sc, sb_sc, qx_sc, part_sc, gslc_sc):
    wb_refs = (wb0_ref, wb1_ref)
    tq, r_heads, groups = NSA_TQ, NSA_GROUP, NSA_KV_HEADS
    rows = r_heads * tq
    q0 = pl.program_id(1) * tq
    lo = lax.broadcasted_iota(jnp.int32, (tq, LANES), 1) < HEAD_DIM
    n_slc = sw_ref.shape[1] // SLC_BLOCK
    blk_shift = SLC_BLOCK.bit_length() - 1

    def slab(i):
        return slice(i * LANES, (i + 1) * LANES)

    def head(o, r):
        return o[r * tq:(r + 1) * tq]

    def front(g, nb):
        def stack(ref):
            return jnp.concatenate([ref[0, :, slab(r_heads * g + r)] for r in range(r_heads)], axis=0)

        q_plain = stack(qp_ref)
        q_rot = stack(qr_ref)

        half = tq // 2
        o_win = []
        for u, wb_ref in enumerate(wb_refs):
            t0 = q0 + u * half
            w0 = pl.multiple_of(jnp.maximum(t0 - WINDOW, 0), half)
            kvw = sw_ref[0, pl.ds(w0, WINDOW + half), slab(2 * g + 1)]
            q_u = jnp.concatenate([qr_ref[0, u * half:(u + 1) * half, slab(r_heads * g + r)]
                                   for r in range(r_heads)], axis=0)
            s = _dot_nt(q_u, kvw).reshape(r_heads, half, WINDOW + half) + wb_ref[...][None]
            p = jnp.exp(s - jnp.max(s, axis=-1, keepdims=True)).reshape(r_heads * half, WINDOW + half)
            o_win.append(_softmax_finish(_dot(p.astype(BF16), _with_ones(kvw))))

        def win_head(r):
            return jnp.concatenate([o[r * half:(r + 1) * half] for o in o_win], axis=0)

        n_pad = min(kvc_ref.shape[2], -(-nb * (SLC_BLOCK // CMP_STRIDE) // LANES) * LANES)
        kvc = kvc_ref[0, g, :n_pad, :]
        n_idx = lax.broadcasted_iota(jnp.int32, (tq, n_pad), 1)
        t_idx = q0 + lax.broadcasted_iota(jnp.int32, (tq, n_pad), 0)
        valid_c = (n_idx * CMP_STRIDE + CMP_LEN - 1 <= t_idx)[None]
        s = jnp.where(valid_c, _dot_nt(q_plain, kvc).reshape(r_heads, tq, n_pad), NEG)
        m = jnp.maximum(jnp.max(s, axis=-1, keepdims=True), NEG / 2)
        p = jnp.exp(s - m)
        p = p / jnp.maximum(jnp.sum(p, axis=-1, keepdims=True), 1.0)
        o_cmp = _dot(p.reshape(rows, n_pad).astype(BF16), kvc)

        p_sum = p[0] + p[1] + p[2] + p[3]
        ov = ov_ref[:n_pad, :]
        hi = p_sum.astype(BF16)
        r1 = p_sum - hi.astype(F32)
        mid = r1.astype(BF16)
        low = (r1 - mid.astype(F32)).astype(BF16)
        imp = _dot(hi, ov) + _dot(mid, ov) + _dot(low, ov)

        j_idx = lax.broadcasted_iota(jnp.int32, (tq, LANES), 1)
        t_row = q0 + lax.broadcasted_iota(jnp.int32, (tq, LANES), 0)
        qblk = jnp.right_shift(t_row, blk_shift)
        forced = (j_idx == 0) | (j_idx == qblk) | (j_idx == qblk - 1)
        visible = j_idx * SLC_BLOCK <= t_row
        score = jnp.where(visible, jnp.where(forced, FORCE_BONUS, imp), NEG)
        if nb <= SLC_TOPN:
            sel_t = jnp.ones((nb, tq), F32)
        else:
            sel_t = _rank_below(score.T[:nb], nb, SLC_TOPN).astype(F32)
        bias = _block_bias(sel_t, tq)

        gate = gt_ref[0, :, slab(g)]

        def gate_of(r, branch):
            return jnp.broadcast_to(gate[:, 3 * r + branch:3 * r + branch + 1], (tq, LANES))

        for r in range(r_heads):
            part_sc[g, r] = gate_of(r, 0) * head(o_cmp, r) + gate_of(r, 2) * win_head(r)
            gslc_sc[g, r] = gate_of(r, 1)
        q_ext = jnp.concatenate([q_rot, jnp.concatenate([bias] * r_heads, axis=0)], axis=1)
        qx_sc[g] = q_ext
        return q_ext

    def slc_keys(c):
        ks = pl.ds(pl.multiple_of(c * NSA_KC, NSA_KC), NSA_KC)
        return tuple(sw_ref[0, ks, slab(2 * g)] for g in range(groups)), oh_ref[ks, :]

    bucket = (q0 + tq - 1) // (SLC_TOPN * SLC_BLOCK)
    for b, nb in enumerate(range(SLC_TOPN, n_slc + 1, SLC_TOPN)):
        @pl.when(bucket == b)
        def _(nb=nb):
            q_ext = [front(g, nb) for g in range(groups)]
            m_sc[...] = jnp.full(m_sc.shape, NEG, F32)
            acc_sc[...] = jnp.zeros(acc_sc.shape, F32)
            _put_scores(q_ext, slc_keys, 0, sa_sc)

    c_last = q0 // NSA_KC
    kpos = c_last * NSA_KC + lax.broadcasted_iota(jnp.int32, (rows, NSA_KC), 1)
    tpos = q0 + (lax.broadcasted_iota(jnp.int32, (rows, NSA_KC), 0) & (tq - 1))
    _attend([qx_sc[g] for g in range(groups)], slc_keys, c_last, kpos <= tpos, m_sc, acc_sc, sa_sc, sb_sc)

    for g in range(groups):
        o_slc = _softmax_finish(acc_sc[g])
        outs = [part_sc[g, r] + gslc_sc[g, r] * head(o_slc, r) for r in range(r_heads)]
        for pr in range(r_heads // 2):
            pair = jnp.where(lo, pltpu.roll(outs[2 * pr], HEAD_DIM, 1), outs[2 * pr + 1])
            o_ref[0, :, slab(2 * g + pr)] = pair.astype(BF16)


def _window_bias(seq, half):
    offsets = sorted({t0 - max(t0 - WINDOW, 0) for t0 in range(0, seq, half)})
    i = np.arange(half)[:, None]
    c = np.arange(WINDOW + half)[None, :]
    tabs = []
    for off in offsets:
        d = off + i - c
        tabs.append(np.where((d >= 0) & (d < WINDOW), 0.0, NEG))
    return np.stack(tabs).astype(np.float32)


def _nsa(qp, qr, gt, kvc, sw, ov):
    b, seq = qp.shape[:2]
    groups = NSA_KV_HEADS
    tq = NSA_TQ
    rows = NSA_GROUP * tq
    n_pad = kvc.shape[2]
    half = tq // 2
    wb = jnp.asarray(_window_bias(seq, half))
    last = wb.shape[0] - 1

    def band(u):
        return pl.BlockSpec((None, half, WINDOW + half), lambda i, k: (jnp.minimum(2 * k + u, last), 0, 0))

    return pl.pallas_call(
        _nsa_kernel,
        grid=(b, seq // tq),
        in_specs=[
            pl.BlockSpec((1, tq, NSA_HEADS * LANES), lambda i, k: (i, k, 0)),
            pl.BlockSpec((1, tq, NSA_HEADS * LANES), lambda i, k: (i, k, 0)),
            pl.BlockSpec((1, tq, groups * LANES), lambda i, k: (i, k, 0)),
            band(0), band(1),
            pl.BlockSpec((1, groups, n_pad, LANES), lambda i, k: (i, 0, 0, 0)),
            pl.BlockSpec((1, seq, 2 * groups * LANES), lambda i, k: (i, 0, 0)),
            pl.BlockSpec((n_pad, LANES), lambda i, k: (0, 0)),
            pl.BlockSpec((seq, LANES), lambda i, k: (0, 0)),
        ],
        out_specs=pl.BlockSpec((1, tq, NSA_Q_COLS), lambda i, k: (i, k, 0)),
        out_shape=jax.ShapeDtypeStruct((b, seq, NSA_Q_COLS), BF16),
        scratch_shapes=[pltpu.VMEM((groups, rows, LANES), F32), pltpu.VMEM((groups, rows, 2 * LANES), F32),
                        pltpu.VMEM((groups, rows, NSA_KC), F32), pltpu.VMEM((groups, rows, NSA_KC), F32),
                        pltpu.VMEM((groups, rows, 2 * LANES), BF16),
                        pltpu.VMEM((groups, NSA_GROUP, tq, LANES), F32), pltpu.VMEM((groups, NSA_GROUP, tq, LANES), F32)],
        compiler_params=_params(("parallel", "arbitrary")),
        name="nsa",
    )(qp, qr, gt, wb, wb, kvc, sw, ov, _block_onehot(seq, SLC_BLOCK))


def _moba_kernel(q_ref, kv_ref, km_ref, oh_ref, o_ref, m_sc, acc_sc, sa_sc, sb_sc):
    tq = MOBA_TQ
    heads = MOBA_HEADS_PER_STEP
    q0 = pl.multiple_of(pl.program_id(2) * tq, tq)
    lo = lax.broadcasted_iota(jnp.int32, (tq, LANES), 1) < HEAD_DIM
    n_blk = km_ref.shape[1]
    blk_shift = MOBA_BLOCK.bit_length() - 1
    blk = lax.broadcasted_iota(jnp.int32, (n_blk, tq), 0)
    qblk = jnp.right_shift(q0 + lax.broadcasted_iota(jnp.int32, (n_blk, tq), 1), blk_shift)
    past = blk < qblk

    def slab(h):
        return slice(h * LANES, (h + 1) * LANES)

    q_ext = []
    for h in range(heads):
        q = jnp.where(lo if h % 2 == 0 else ~lo, q_ref[0, :, slab(h // 2)], jnp.zeros((), BF16))
        gate_t = _dot_nt(km_ref[0, :, slab(h)].astype(BF16), q)
        top = _rank_below(jnp.where(past, gate_t, NEG), n_blk, MOBA_TOPK)
        bias = _block_bias(((top & past) | (blk == qblk)).astype(F32), tq)
        q_ext.append(jnp.concatenate([q, bias], axis=1))

    m_sc[...] = jnp.full(m_sc.shape, NEG, F32)
    acc_sc[...] = jnp.zeros(acc_sc.shape, F32)

    def keys(c):
        ks = pl.ds(pl.multiple_of(c * tq, tq), tq)
        return tuple(kv_ref[0, ks, slab(h)] for h in range(heads)), oh_ref[ks, :]

    causal = lax.broadcasted_iota(jnp.int32, (tq, tq), 1) <= lax.broadcasted_iota(jnp.int32, (tq, tq), 0)
    _put_scores(q_ext, keys, 0, sa_sc)
    _attend(q_ext, keys, pl.program_id(2), causal, m_sc, acc_sc, sa_sc, sb_sc)

    for pr in range(heads // 2):
        even = _softmax_finish(acc_sc[2 * pr])
        odd = _softmax_finish(acc_sc[2 * pr + 1])
        o_ref[0, :, slab(pr)] = jnp.where(lo, odd, even).astype(BF16)


def _moba(mq, mkv, km):
    b, seq = mq.shape[:2]
    tq = MOBA_TQ
    n_blk = seq // MOBA_BLOCK
    heads = MOBA_HEADS_PER_STEP
    return pl.pallas_call(
        _moba_kernel,
        grid=(b, MOBA_HEADS // heads, seq // tq),
        in_specs=[
            pl.BlockSpec((1, tq, heads // 2 * LANES), lambda i, j, k: (i, k, j)),
            pl.BlockSpec((1, seq, heads * LANES), lambda i, j, k: (i, 0, j)),
            pl.BlockSpec((1, n_blk, heads * LANES), lambda i, j, k: (i, 0, j)),
            pl.BlockSpec((seq, LANES), lambda i, j, k: (0, 0)),
        ],
        out_specs=pl.BlockSpec((1, tq, heads // 2 * LANES), lambda i, j, k: (i, k, j)),
        out_shape=jax.ShapeDtypeStruct((b, seq, MOBA_HEADS * HEAD_DIM), BF16),
        scratch_shapes=[pltpu.VMEM((heads, tq, LANES), F32), pltpu.VMEM((heads, tq, 2 * LANES), F32),
                        pltpu.VMEM((heads, tq, tq), F32), pltpu.VMEM((heads, tq, tq), F32)],
        compiler_params=_params(("parallel", "parallel", "arbitrary")),
        name="moba",
    )(mq, mkv, km, _block_onehot(seq, MOBA_BLOCK))


def _in_perm():
    d = HEAD_DIM
    perm = np.full((IN_COLS_P,), -1, np.int64)
    perm[SEC_Q:SEC_Q + NSA_Q_COLS] = np.arange(NSA_Q_COLS)
    kv0 = NSA_Q_COLS
    gate0 = kv0 + NSA_KV_COLS
    moba0 = gate0 + NSA_GATE_COLS
    per_group = NSA_GROUP * 3
    for g in range(NSA_KV_HEADS):
        perm[SEC_GATE + g * LANES:SEC_GATE + g * LANES + per_group] = gate0 + g * per_group + np.arange(per_group)

    def kv_cols(which, g):
        return kv0 + (which * NSA_KV_HEADS + g) * d + np.arange(d)

    for g in range(NSA_KV_HEADS):
        base = SEC_SW + g * 2 * LANES
        perm[base:base + d] = kv_cols(2, g)
        perm[base + d:base + 2 * d] = kv_cols(3, g)
        perm[base + 2 * d:base + 3 * d] = kv_cols(4, g)
        perm[base + 3 * d:base + 4 * d] = kv_cols(5, g)
        base = SEC_CMP + g * LANES
        perm[base:base + d] = kv_cols(0, g)
        perm[base + d:base + 2 * d] = kv_cols(1, g)
    perm[SEC_MQ:SEC_MQ + MOBA_HEADS * d] = moba0 + np.arange(MOBA_HEADS * d)
    for h in range(MOBA_HEADS):
        k_cols = moba0 + (MOBA_HEADS + h) * d + np.arange(d)
        v_cols = moba0 + (2 * MOBA_HEADS + h) * d + np.arange(d)
        base = SEC_MKV + h * LANES
        first, second = (k_cols, v_cols) if h % 2 == 0 else (v_cols, k_cols)
        perm[base:base + d] = first
        perm[base + d:base + 2 * d] = second
    return perm


def _take_runs(w, idx, axis):
    parts, i = [], 0
    while i < len(idx):
        j = i + 1
        if idx[i] < 0:
            while j < len(idx) and idx[j] < 0:
                j += 1
            shape = list(w.shape)
            shape[axis] = j - i
            parts.append(jnp.zeros(shape, w.dtype))
        else:
            while j < len(idx) and idx[j] == idx[j - 1] + 1:
                j += 1
            parts.append(lax.slice_in_dim(w, int(idx[i]), int(idx[j - 1]) + 1, axis=axis))
        i = j
    return jnp.concatenate(parts, axis=axis)


def _moba_out_rows():
    d = HEAD_DIM
    rows = []
    for pr in range(MOBA_HEADS // 2):
        rows.append(NSA_Q_COLS + (2 * pr + 1) * d + np.arange(d))
        rows.append(NSA_Q_COLS + (2 * pr) * d + np.arange(d))
    return np.concatenate(rows)


def _overlap_table(seq):
    n_pad = seq // CMP_STRIDE
    n = np.arange(n_pad)[:, None]
    j = np.arange(LANES)[None, :]
    starts = n * CMP_STRIDE
    sb = j * SLC_BLOCK
    ov = (starts < sb + SLC_BLOCK) & (starts + CMP_LEN > sb) & (n < n_pad - 1) & (j < seq // SLC_BLOCK)
    return ov.astype(np.float32)


def _block_onehot(seq, block):
    onehot = np.arange(seq)[:, None] // block == np.arange(LANES)[None, :]
    return jnp.asarray(onehot.astype(np.float32), BF16)


def _rope_lane_tables(seq):
    half = ROT_DIM // 2
    inv_freq = jnp.power(ROPE_THETA, -(jnp.arange(0, ROT_DIM, 2, dtype=F32) / ROT_DIM))
    ang = jnp.arange(seq, dtype=F32)[:, None] * inv_freq[None, :]
    cos, sin = jnp.cos(ang), jnp.sin(ang)
    rest = HEAD_DIM - ROT_DIM
    zeros_h = jnp.zeros((seq, half), F32)
    c = jnp.concatenate([cos, cos, jnp.ones((seq, rest), F32)], axis=1)
    sl = jnp.concatenate([-sin, zeros_h, jnp.zeros((seq, rest), F32)], axis=1)
    sr = jnp.concatenate([zeros_h, sin, jnp.zeros((seq, rest), F32)], axis=1)
    return tuple(jnp.tile(t, (1, LANES // HEAD_DIM)) for t in (c, sl, sr))


def _mixer(x2, b, seq, g_mix, w_p, layer, pos_ck, w_ck1, w_ck2, pos_cv, w_cv1, w_cv2, tabs, ov):
    qp, qr, gt, sw, cmp_in, mq, mkv, km = _inproj(x2, g_mix, w_p, layer, tabs, seq)

    def halves(w_k, w_v):
        wk = w_k.reshape(2, CMP_STRIDE, HEAD_DIM, CMP_HIDDEN)
        wv = w_v.reshape(2, CMP_STRIDE, HEAD_DIM, CMP_HIDDEN)
        z = jnp.zeros_like(wk)
        return jnp.concatenate([jnp.concatenate([wk, z], axis=3), jnp.concatenate([z, wv], axis=3)],
                               axis=2).astype(BF16)

    w1 = halves(w_ck1, w_cv1)
    pos = jnp.concatenate([pos_ck, pos_cv], axis=1).reshape(2, CMP_STRIDE, LANES)
    z2 = jnp.zeros_like(w_ck2)
    w2 = jnp.concatenate([jnp.concatenate([w_ck2, z2], axis=1),
                          jnp.concatenate([z2, w_cv2], axis=1)], axis=0).astype(BF16)
    kvc = _compress(cmp_in.reshape(b, seq, -1), pos, w1[0], w1[1], w2)

    o_nsa = _nsa(qp.reshape(b, seq, -1), qr.reshape(b, seq, -1), gt.reshape(b, seq, -1), kvc,
                 sw.reshape(b, seq, -1), ov)
    o_moba = _moba(mq.reshape(b, seq, -1), mkv.reshape(b, seq, -1), km.reshape(b, seq // MOBA_BLOCK, -1))

    n = b * seq
    return o_nsa.reshape(n, -1), o_moba.reshape(n, -1)


def kernel(x, norm_ffn1, w_ffn1_gate, w_ffn1_up, w_ffn1_down, norm_mix, w_in, pos_ck, w_ck1, w_ck2,
           pos_cv, w_cv1, w_cv2, w_out, norm_ffn2, w_ffn2_gate, w_ffn2_up, w_ffn2_down, norm_final):
    b, seq, d = x.shape
    assert d == D_MODEL and seq % PROJ_ROWS == 0 and SLC_TOPN <= seq // SLC_BLOCK <= LANES
    depth = norm_ffn1.shape[0]
    tabs = _rope_lane_tables(seq)
    ov = jnp.asarray(_overlap_table(seq), BF16)
    x2 = x.reshape(b * seq, d)
    w_p = _take_runs(w_in.astype(BF16), _in_perm(), axis=2)
    w_out_b = w_out.astype(BF16)
    wa = w_out_b[:, :NSA_Q_COLS]
    wm = _take_runs(w_out_b, _moba_out_rows(), axis=1)
    wg1, wu1, wd1, wg2, wu2, wd2 = (w.astype(BF16) for w in (w_ffn1_gate, w_ffn1_up, w_ffn1_down,
                                                            w_ffn2_gate, w_ffn2_up, w_ffn2_down))
    for l in range(depth):
        x2 = _ffn(x2, norm_ffn1[l], wg1, wu1, wd1, l)
        o_nsa, o_moba = _mixer(x2, b, seq, norm_mix[l], w_p, l, pos_ck[l], w_ck1[l], w_ck2[l],
                               pos_cv[l], w_cv1[l], w_cv2[l], tabs, ov)
        x2 = _ffn(x2, norm_ffn2[l], wg2, wu2, wd2, l, mix=(o_nsa, o_moba, wa[l], wm[l]),
                  g_final=norm_final if l == depth - 1 else None)
    return x2.reshape(b, seq, d)
```

```python
import functools

import numpy as np
import jax
import jax.numpy as jnp
from jax import lax
from jax.experimental import pallas as pl
from jax.experimental.pallas import tpu as pltpu

D_MODEL = 1024
HEAD_DIM = 64
NSA_HEADS = 8
NSA_KV_HEADS = 2
NSA_GROUP = NSA_HEADS // NSA_KV_HEADS
MOBA_HEADS = 8
ROT_DIM = HEAD_DIM // 4
ROPE_THETA = 500000.0
CMP_LEN = 32
CMP_STRIDE = 16
CMP_HIDDEN = 256
SLC_BLOCK = 64
SLC_TOPN = 16
WINDOW = 512
MOBA_BLOCK = 256
MOBA_TOPK = 3
D_FF = 2816
EPS = 1e-6
NEG = -1e30
FORCE_BONUS = 1e4
SCALE = HEAD_DIM ** -0.5

NSA_Q_COLS = NSA_HEADS * HEAD_DIM
NSA_KV_COLS = 3 * 2 * NSA_KV_HEADS * HEAD_DIM
NSA_GATE_COLS = NSA_HEADS * 3
MOBA_COLS = 3 * MOBA_HEADS * HEAD_DIM

LANES = 128
SUBLANES = 8
VMEM_LIMIT = 56 * 1024 * 1024

SEC_Q = 0
SEC_GATE = 512
SEC_SW = 768
SEC_CMP = 1280
SEC_MQ = 1536
SEC_MKV = 2048
IN_COLS_P = 3072

F32 = jnp.float32
BF16 = jnp.bfloat16

PIECE = HEAD_DIM // 2


def _lane_dim(lane):
    j, hi = lane % PIECE, lane >= LANES // 2
    half = ROT_DIM // 2
    if not hi:
        return j if j < half else j + half
    return j + half if j < half else j + PIECE


def _lane_is_a(lane):
    return (lane // PIECE) % 2 == 0


def _slab_cols(cols_a, cols_b):
    return np.array([(cols_a if _lane_is_a(l) else cols_b)[_lane_dim(l)] for l in range(LANES)])


def _mask_a(shape):
    return (lax.broadcasted_iota(jnp.int32, shape, len(shape) - 1) & PIECE) == 0

FFN_ROWS = 1024
FFN_CHUNK = 256
PROJ_ROWS = 1024
NSA_TQ = 256
NSA_KC = 512
MOBA_HEADS_PER_STEP = 4
MOBA_TQ = 2 * MOBA_BLOCK
MASK_BIAS = 2.0 ** 100


def _dot(a, b):
    return jnp.dot(a, b, preferred_element_type=F32)


def _dot_nt(a, b):
    return lax.dot_general(a, b, (((1,), (1,)), ((), ())), preferred_element_type=F32)


def _rms(x, g):
    return x * lax.rsqrt(jnp.mean(x * x, axis=-1, keepdims=True) + EPS) * g


def _params(sem):
    return pltpu.CompilerParams(dimension_semantics=sem, vmem_limit_bytes=VMEM_LIMIT)


def _resident(shape, layer=None):
    nd = len(shape)
    if layer is None:
        return pl.BlockSpec(shape, lambda *_: (0,) * nd, pipeline_mode=pl.Buffered(1))
    return pl.BlockSpec((None,) + tuple(shape), lambda *_: (layer,) + (0,) * nd,
                        pipeline_mode=pl.Buffered(1))


def _ffn_kernel(*refs, mix, final_norm):
    refs = list(refs)
    x_ref = refs.pop(0)
    if mix:
        a_ref, mo_ref, wa_ref, wm_ref = (refs.pop(0) for _ in range(4))
    g_ref, wg_ref, wu_ref, wd_ref = (refs.pop(0) for _ in range(4))
    gf_ref = refs.pop(0) if final_norm else None
    o_ref, h_ref = refs
    x = x_ref[...]
    if mix:
        x = x + _dot(a_ref[...], wa_ref[...]) + _dot(mo_ref[...], wm_ref[...])
    xb = _rms(x, g_ref[...]).astype(BF16)
    for c in range(D_FF // FFN_CHUNK):
        sl = slice(c * FFN_CHUNK, (c + 1) * FFN_CHUNK)
        gate = _dot(xb, wg_ref[:, sl])
        up = _dot(xb, wu_ref[:, sl])
        h_ref[:, sl] = (jax.nn.silu(gate) * up).astype(BF16)
    y = x + 0.5 * _dot(h_ref[...], wd_ref[...])
    if final_norm:
        y = _rms(y, gf_ref[...])
    o_ref[...] = y


def _ffn(x2, g, wg, wu, wd, layer, mix=None, g_final=None):
    n = x2.shape[0]
    final_norm = g_final is not None

    def rows(width):
        return pl.BlockSpec((FFN_ROWS, width), lambda i: (i, 0))

    in_specs, args = [rows(D_MODEL)], [x2]
    if mix is not None:
        o_nsa, o_moba, wa, wm = mix
        in_specs += [rows(o_nsa.shape[1]), rows(o_moba.shape[1]), _resident(wa.shape), _resident(wm.shape)]
        args += [o_nsa, o_moba, wa, wm]
    in_specs += [_resident((1, D_MODEL)), _resident((D_MODEL, D_FF), layer),
                 _resident((D_MODEL, D_FF), layer), _resident((D_FF, D_MODEL), layer)]
    args += [g.reshape(1, D_MODEL), wg, wu, wd]
    if final_norm:
        in_specs.append(_resident((1, D_MODEL)))
        args.append(g_final.reshape(1, D_MODEL))
    return pl.pallas_call(
        functools.partial(_ffn_kernel, mix=mix is not None, final_norm=final_norm),
        grid=(n // FFN_ROWS,),
        in_specs=in_specs,
        out_specs=rows(D_MODEL),
        out_shape=jax.ShapeDtypeStruct((n, D_MODEL), F32),
        scratch_shapes=[pltpu.VMEM((FFN_ROWS, D_FF), BF16)],
        compiler_params=_params(("parallel",)),
        name="ffn" + ("_mix" if mix is not None else "") + ("_final" if final_norm else ""),
    )(*args)


def _rope_tables(c, sn, mode):
    if mode == "both":
        return c, sn
    keep = _mask_a(c.shape) if mode == "a" else ~_mask_a(c.shape)
    return jnp.where(keep, c, 1.0), jnp.where(keep, sn, 0.0)


def _rope(v, tabs):
    c, sn = tabs
    return v * c + pltpu.roll(v, LANES // 2, 1) * sn


def _inproj_kernel(x_ref, g_ref, w_ref, c_ref, sn_ref,
                   qp_ref, qr_ref, gt_ref, sw_ref, cmp_ref, mq_ref, mkv_ref, km_ref):
    xb = _rms(x_ref[...], g_ref[...]).astype(BF16)
    base = (c_ref[...], sn_ref[...])
    tabs = {m: _rope_tables(*base, m) for m in ("both", "a", "b")}

    qn = _dot(xb, w_ref[:, SEC_Q:SEC_GATE])
    on_a = _mask_a((PROJ_ROWS, LANES))
    for pr in range(NSA_HEADS // 2):
        pair = qn[:, pr * LANES:(pr + 1) * LANES]
        for out_ref, v in ((qp_ref, pair), (qr_ref, _rope(pair, tabs["both"]))):
            for e in range(2):
                head = v if e == 0 else pltpu.roll(v, LANES - PIECE, 1)
                sl = slice((2 * pr + e) * LANES, (2 * pr + e + 1) * LANES)
                out_ref[:, sl] = jnp.where(on_a, head * SCALE, 0.0).astype(BF16)
    gt_ref[...] = jax.nn.sigmoid(_dot(xb, w_ref[:, SEC_GATE:SEC_SW]))
    cmp_ref[...] = _dot(xb, w_ref[:, SEC_CMP:SEC_MQ])

    sw = _dot(xb, w_ref[:, SEC_SW:SEC_CMP])
    for i in range(4):
        sl = slice(i * LANES, (i + 1) * LANES)
        sw_ref[:, sl] = _rope(sw[:, sl], tabs["a"]).astype(BF16)

    mq = _dot(xb, w_ref[:, SEC_MQ:SEC_MKV])
    for i in range(4):
        sl = slice(i * LANES, (i + 1) * LANES)
        mq_ref[:, sl] = (_rope(mq[:, sl], tabs["both"]) * SCALE).astype(BF16)

    mkv = _dot(xb, w_ref[:, SEC_MKV:IN_COLS_P])
    nblk = PROJ_ROWS // MOBA_BLOCK
    for h in range(MOBA_HEADS):
        sl = slice(h * LANES, (h + 1) * LANES)
        r = _rope(mkv[:, sl], tabs["a" if h % 2 == 0 else "b"])
        mkv_ref[:, sl] = r.astype(BF16)
        km_ref[:, 0, sl] = jnp.mean(r.reshape(nblk, MOBA_BLOCK, LANES), axis=1)


def _inproj(x2, g, w_p, layer, tabs, seq):
    n = x2.shape[0]
    tiles_per_seq = seq // PROJ_ROWS
    nblk = PROJ_ROWS // MOBA_BLOCK

    def rows(width):
        return pl.BlockSpec((PROJ_ROWS, width), lambda i: (i, 0))

    tab = pl.BlockSpec((PROJ_ROWS, LANES), lambda i: (i % tiles_per_seq, 0))
    out_shapes = (
        jax.ShapeDtypeStruct((n, 1024), BF16),
        jax.ShapeDtypeStruct((n, 1024), BF16),
        jax.ShapeDtypeStruct((n, 256), F32),
        jax.ShapeDtypeStruct((n, 512), BF16),
        jax.ShapeDtypeStruct((n, 256), F32),
        jax.ShapeDtypeStruct((n, 512), BF16),
        jax.ShapeDtypeStruct((n, 1024), BF16),
        jax.ShapeDtypeStruct((n // MOBA_BLOCK, 1, 1024), F32),
    )
    out_specs = (rows(1024), rows(1024), rows(256), rows(512), rows(256), rows(512), rows(1024),
                 pl.BlockSpec((nblk, 1, 1024), lambda i: (i, 0, 0)))
    return pl.pallas_call(
        _inproj_kernel,
        grid=(n // PROJ_ROWS,),
        in_specs=[rows(D_MODEL), _resident((1, D_MODEL)), _resident((D_MODEL, IN_COLS_P), layer),
                  tab, tab],
        out_specs=out_specs,
        out_shape=out_shapes,
        compiler_params=_params(("parallel",)),
        name="inproj",
    )(x2, g.reshape(1, D_MODEL), w_p, *tabs)


def _compress_kernel(x_ref, pos_ref, wa_ref, wb_ref, w2_ref, o_ref):
    n_pad = x_ref.shape[1] // CMP_STRIDE
    width = 2 * CMP_HIDDEN
    ya = jnp.zeros((n_pad, width), F32)
    zb = jnp.zeros((n_pad, width), F32)
    for l in range(CMP_STRIDE):
        h = x_ref[0, pl.ds(l, n_pad, stride=CMP_STRIDE), :]
        ya = ya + _dot((h + pos_ref[0, l:l + 1, :]).astype(BF16), wa_ref[l])
        zb = zb + _dot((h + pos_ref[1, l:l + 1, :]).astype(BF16), wb_ref[l])
    pre = ya + pltpu.roll(zb, n_pad - 1, 0)
    out = _dot(jax.nn.gelu(pre).astype(BF16), w2_ref[...])
    row = lax.broadcasted_iota(jnp.int32, out.shape, 0)
    o_ref[0, 0] = jnp.where(row < n_pad - 1, out, 0.0).astype(BF16)


def _compress(cmp_in, pos, wa, wb, w2):
    b, seq = cmp_in.shape[:2]
    g = NSA_KV_HEADS
    n_pad = seq // CMP_STRIDE
    return pl.pallas_call(
        _compress_kernel,
        grid=(b, g),
        in_specs=[pl.BlockSpec((1, seq, LANES), lambda i, j: (i, 0, j)),
                  _resident(pos.shape), _resident(wa.shape), _resident(wb.shape), _resident(w2.shape)],
        out_specs=pl.BlockSpec((1, 1, n_pad, LANES), lambda i, j: (i, j, 0, 0)),
        out_shape=jax.ShapeDtypeStruct((b, g, n_pad, LANES), BF16),
        compiler_params=_params(("parallel", "parallel")),
        name="compress",
    )(cmp_in, pos, wa, wb, w2)


def _rank_below(s_t, n_rows, limit):
    t = s_t.shape[1]
    n_grp = n_rows // SUBLANES
    grp = [s_t[g * SUBLANES:(g + 1) * SUBLANES] for g in range(n_grp)]
    cnt = [jnp.zeros((SUBLANES, t), F32) for _ in range(n_grp)]
    sub = lax.broadcasted_iota(jnp.int32, (SUBLANES, t), 0)
    for i in range(n_rows):
        ri = s_t[i:i + 1, :]
        for g in range(n_grp):
            if g > i // SUBLANES:
                beats = ri >= grp[g]
            elif g < i // SUBLANES:
                beats = ri > grp[g]
            else:
                beats = (ri > grp[g]) | ((ri == grp[g]) & (sub > i % SUBLANES))
            cnt[g] = cnt[g] + jnp.where(beats, 1.0, 0.0)
    return jnp.concatenate(cnt, axis=0) < limit


def _block_bias(sel_t, t):
    pad = jnp.zeros((LANES - sel_t.shape[0], t), F32)
    sel = jnp.concatenate([sel_t, pad], axis=0).T
    return ((sel - 1.0) * MASK_BIAS).astype(BF16)


def _with_ones(kv):
    return jnp.concatenate([kv, jnp.ones(kv.shape, kv.dtype)], axis=1)


def _softmax_step(read_s, kv, m_ref, acc_ref):
    m_old = m_ref[...]
    m_new = jnp.maximum(m_old, jnp.broadcast_to(jnp.max(read_s(), axis=-1, keepdims=True), m_old.shape))
    m_ref[...] = m_new
    alpha = jnp.exp(m_old - m_new)
    s = read_s()
    p = jnp.exp(s - jnp.concatenate([m_new] * (s.shape[1] // LANES), axis=1)).astype(BF16)
    acc_ref[...] = jnp.concatenate([alpha, alpha], axis=1) * acc_ref[...] + _dot(p, _with_ones(kv))


def _put_scores(q_ext, keys, c, s_ref):
    kvs, onehot = keys(c)
    for h, (q, kv) in enumerate(zip(q_ext, kvs)):
        s_ref[h] = _dot_nt(q, jnp.concatenate([kv, onehot], axis=1))


def _attend(q_ext, keys, n_past, causal, m_ref, acc_ref, s_a, s_b):
    put_scores = functools.partial(_put_scores, q_ext, keys)

    def softmax(c, s_ref, mask=None):
        for h, kv in enumerate(keys(c)[0]):
            if mask is None:
                read_s = functools.partial(lambda h: s_ref[h], h)
            else:
                read_s = functools.partial(lambda s: s, jnp.where(mask, s_ref[h], -MASK_BIAS))
            _softmax_step(read_s, kv, m_ref.at[h], acc_ref.at[h])

    def body(i, carry):
        c = 2 * i
        put_scores(c + 1, s_b)
        softmax(c, s_a)
        put_scores(c + 2, s_a)
        softmax(c + 1, s_b)
        return carry

    lax.fori_loop(0, n_past // 2, body, 0)

    @pl.when(n_past % 2 == 1)
    def _():
        put_scores(n_past, s_b)
        softmax(n_past - 1, s_a)
        softmax(n_past, s_b, causal)

    @pl.when(n_past % 2 == 0)
    def _():
        softmax(n_past, s_a, causal)


def _softmax_finish(acc):
    return acc[:, :LANES] / jnp.maximum(acc[:, LANES:], 1.0)


def _nsa_kernel(qp_ref, qr_ref, gt_ref, wb0_ref, wb1_ref, kvc_ref, sw_ref, ov_ref, oh_ref,
                o_ref, m_sc, acc_sc, sa_sc, sb_sc, qx_sc, part_sc, gslc_sc):
    wb_refs = (wb0_ref, wb1_ref)
    tq, r_heads, groups = NSA_TQ, NSA_GROUP, NSA_KV_HEADS
    rows = r_heads * tq
    q0 = pl.program_id(1) * tq
    on_a = _mask_a((tq, LANES))
    n_slc = sw_ref.shape[1] // SLC_BLOCK
    blk_shift = SLC_BLOCK.bit_length() - 1

    def slab(i):
        return slice(i * LANES, (i + 1) * LANES)

    def head(o, r):
        return o[r * tq:(r + 1) * tq]

    def front(g, nb):
        def stack(ref):
            return jnp.concatenate([ref[0, :, slab(r_heads * g + r)] for r in range(r_heads)], axis=0)

        q_plain = stack(qp_ref)
        q_rot = stack(qr_ref)

        half = tq // 2
        o_win = []
        for u, wb_ref in enumerate(wb_refs):
            t0 = q0 + u * half
            w0 = pl.multiple_of(jnp.maximum(t0 - WINDOW, 0), half)
            kvw = sw_ref[0, pl.ds(w0, WINDOW + half), slab(2 * g + 1)]
            q_u = jnp.concatenate([qr_ref[0, u * half:(u + 1) * half, slab(r_heads * g + r)]
                                   for r in range(r_heads)], axis=0)
            s = _dot_nt(q_u, kvw).reshape(r_heads, half, WINDOW + half) + wb_ref[...][None]
            p = jnp.exp(s - jnp.max(s, axis=-1, keepdims=True)).reshape(r_heads * half, WINDOW + half)
            o_win.append(_softmax_finish(_dot(p.astype(BF16), _with_ones(kvw))))

        def win_head(r):
            return jnp.concatenate([o[r * half:(r + 1) * half] for o in o_win], axis=0)

        n_pad = min(kvc_ref.shape[2], -(-nb * (SLC_BLOCK // CMP_STRIDE) // LANES) * LANES)
        kvc = kvc_ref[0, g, :n_pad, :]
        n_idx = lax.broadcasted_iota(jnp.int32, (tq, n_pad), 1)
        t_idx = q0 + lax.broadcasted_iota(jnp.int32, (tq, n_pad), 0)
        valid_c = (n_idx * CMP_STRIDE + CMP_LEN - 1 <= t_idx)[None]
        s = jnp.where(valid_c, _dot_nt(q_plain, kvc).reshape(r_heads, tq, n_pad), NEG)
        m = jnp.maximum(jnp.max(s, axis=-1, keepdims=True), NEG / 2)
        p = jnp.exp(s - m)
        p = p / jnp.maximum(jnp.sum(p, axis=-1, keepdims=True), 1.0)
        o_cmp = _dot(p.reshape(rows, n_pad).astype(BF16), kvc)

        p_sum = p[0] + p[1] + p[2] + p[3]
        ov = ov_ref[:n_pad, :]
        hi = p_sum.astype(BF16)
        r1 = p_sum - hi.astype(F32)
        mid = r1.astype(BF16)
        low = (r1 - mid.astype(F32)).astype(BF16)
        imp = _dot(hi, ov) + _dot(mid, ov) + _dot(low, ov)

        j_idx = lax.broadcasted_iota(jnp.int32, (tq, LANES), 1)
        t_row = q0 + lax.broadcasted_iota(jnp.int32, (tq, LANES), 0)
        qblk = jnp.right_shift(t_row, blk_shift)
        forced = (j_idx == 0) | (j_idx == qblk) | (j_idx == qblk - 1)
        visible = j_idx * SLC_BLOCK <= t_row
        score = jnp.where(visible, jnp.where(forced, FORCE_BONUS, imp), NEG)
        if nb <= SLC_TOPN:
            sel_t = jnp.ones((nb, tq), F32)
        else:
            sel_t = _rank_below(score.T[:nb], nb, SLC_TOPN).astype(F32)
        bias = _block_bias(sel_t, tq)

        gate = gt_ref[0, :, slab(g)]

        def gate_of(r, branch):
            return jnp.broadcast_to(gate[:, 3 * r + branch:3 * r + branch + 1], (tq, LANES))

        for r in range(r_heads):
            part_sc[g, r] = gate_of(r, 0) * head(o_cmp, r) + gate_of(r, 2) * win_head(r)
            gslc_sc[g, r] = gate_of(r, 1)
        q_ext = jnp.concatenate([q_rot, jnp.concatenate([bias] * r_heads, axis=0)], axis=1)
        qx_sc[g] = q_ext
        return q_ext

    def slc_keys(c):
        ks = pl.ds(pl.multiple_of(c * NSA_KC, NSA_KC), NSA_KC)
        return tuple(sw_ref[0, ks, slab(2 * g)] for g in range(groups)), oh_ref[ks, :]

    bucket = (q0 + tq - 1) // (SLC_TOPN * SLC_BLOCK)
    for b, nb in enumerate(range(SLC_TOPN, n_slc + 1, SLC_TOPN)):
        @pl.when(bucket == b)
        def _(nb=nb):
            q_ext = [front(g, nb) for g in range(groups)]
            m_sc[...] = jnp.full(m_sc.shape, NEG, F32)
            acc_sc[...] = jnp.zeros(acc_sc.shape, F32)
            _put_scores(q_ext, slc_keys, 0, sa_sc)

    c_last = q0 // NSA_KC
    kpos = c_last * NSA_KC + lax.broadcasted_iota(jnp.int32, (rows, NSA_KC), 1)
    tpos = q0 + (lax.broadcasted_iota(jnp.int32, (rows, NSA_KC), 0) & (tq - 1))
    _attend([qx_sc[g] for g in range(groups)], slc_keys, c_last, kpos <= tpos, m_sc, acc_sc, sa_sc, sb_sc)

    for g in range(groups):
        o_slc = _softmax_finish(acc_sc[g])
        outs = [part_sc[g, r] + gslc_sc[g, r] * head(o_slc, r) for r in range(r_heads)]
        for pr in range(r_heads // 2):
            pair = jnp.where(on_a, pltpu.roll(outs[2 * pr], LANES - PIECE, 1), outs[2 * pr + 1])
            o_ref[0, :, slab(2 * g + pr)] = pair.astype(BF16)


def _window_bias(seq, half):
    offsets = sorted({t0 - max(t0 - WINDOW, 0) for t0 in range(0, seq, half)})
    i = np.arange(half)[:, None]
    c = np.arange(WINDOW + half)[None, :]
    tabs = []
    for off in offsets:
        d = off + i - c
        tabs.append(np.where((d >= 0) & (d < WINDOW), 0.0, NEG))
    return np.stack(tabs).astype(np.float32)


def _nsa(qp, qr, gt, kvc, sw, ov):
    b, seq = qp.shape[:2]
    groups = NSA_KV_HEADS
    tq = NSA_TQ
    rows = NSA_GROUP * tq
    n_pad = kvc.shape[2]
    half = tq // 2
    wb = jnp.asarray(_window_bias(seq, half))
    last = wb.shape[0] - 1

    def band(u):
        return pl.BlockSpec((None, half, WINDOW + half), lambda i, k: (jnp.minimum(2 * k + u, last), 0, 0))

    return pl.pallas_call(
        _nsa_kernel,
        grid=(b, seq // tq),
        in_specs=[
            pl.BlockSpec((1, tq, NSA_HEADS * LANES), lambda i, k: (i, k, 0)),
            pl.BlockSpec((1, tq, NSA_HEADS * LANES), lambda i, k: (i, k, 0)),
            pl.BlockSpec((1, tq, groups * LANES), lambda i, k: (i, k, 0)),
            band(0), band(1),
            pl.BlockSpec((1, groups, n_pad, LANES), lambda i, k: (i, 0, 0, 0)),
            pl.BlockSpec((1, seq, 2 * groups * LANES), lambda i, k: (i, 0, 0)),
            pl.BlockSpec((n_pad, LANES), lambda i, k: (0, 0)),
            pl.BlockSpec((seq, LANES), lambda i, k: (0, 0)),
        ],
        out_specs=pl.BlockSpec((1, tq, NSA_Q_COLS), lambda i, k: (i, k, 0)),
        out_shape=jax.ShapeDtypeStruct((b, seq, NSA_Q_COLS), BF16),
        scratch_shapes=[pltpu.VMEM((groups, rows, LANES), F32), pltpu.VMEM((groups, rows, 2 * LANES), F32),
                        pltpu.VMEM((groups, rows, NSA_KC), F32), pltpu.VMEM((groups, rows, NSA_KC), F32),
                        pltpu.VMEM((groups, rows, 2 * LANES), BF16),
                        pltpu.VMEM((groups, NSA_GROUP, tq, LANES), F32), pltpu.VMEM((groups, NSA_GROUP, tq, LANES), F32)],
        compiler_params=_params(("parallel", "arbitrary")),
        name="nsa",
    )(qp, qr, gt, wb, wb, kvc, sw, ov, _block_onehot(seq, SLC_BLOCK))


def _moba_kernel(q_ref, kv_ref, km_ref, oh_ref, o_ref, m_sc, acc_sc, sa_sc, sb_sc):
    tq = MOBA_TQ
    heads = MOBA_HEADS_PER_STEP
    q0 = pl.multiple_of(pl.program_id(2) * tq, tq)
    on_a = _mask_a((tq, LANES))
    n_blk = km_ref.shape[1]
    blk_shift = MOBA_BLOCK.bit_length() - 1
    blk = lax.broadcasted_iota(jnp.int32, (n_blk, tq), 0)
    qblk = jnp.right_shift(q0 + lax.broadcasted_iota(jnp.int32, (n_blk, tq), 1), blk_shift)
    past = blk < qblk

    def slab(h):
        return slice(h * LANES, (h + 1) * LANES)

    q_ext = []
    for h in range(heads):
        q = jnp.where(on_a if h % 2 == 0 else ~on_a, q_ref[0, :, slab(h // 2)], jnp.zeros((), BF16))
        gate_t = _dot_nt(km_ref[0, :, slab(h)].astype(BF16), q)
        top = _rank_below(jnp.where(past, gate_t, NEG), n_blk, MOBA_TOPK)
        bias = _block_bias(((top & past) | (blk == qblk)).astype(F32), tq)
        q_ext.append(jnp.concatenate([q, bias], axis=1))

    m_sc[...] = jnp.full(m_sc.shape, NEG, F32)
    acc_sc[...] = jnp.zeros(acc_sc.shape, F32)

    def keys(c):
        ks = pl.ds(pl.multiple_of(c * tq, tq), tq)
        return tuple(kv_ref[0, ks, slab(h)] for h in range(heads)), oh_ref[ks, :]

    causal = lax.broadcasted_iota(jnp.int32, (tq, tq), 1) <= lax.broadcasted_iota(jnp.int32, (tq, tq), 0)
    _put_scores(q_ext, keys, 0, sa_sc)
    _attend(q_ext, keys, pl.program_id(2), causal, m_sc, acc_sc, sa_sc, sb_sc)

    for pr in range(heads // 2):
        even = _softmax_finish(acc_sc[2 * pr])
        odd = _softmax_finish(acc_sc[2 * pr + 1])
        o_ref[0, :, slab(pr)] = jnp.where(on_a, odd, even).astype(BF16)


def _moba(mq, mkv, km):
    b, seq = mq.shape[:2]
    tq = MOBA_TQ
    n_blk = seq // MOBA_BLOCK
    heads = MOBA_HEADS_PER_STEP
    return pl.pallas_call(
        _moba_kernel,
        grid=(b, MOBA_HEADS // heads, seq // tq),
        in_specs=[
            pl.BlockSpec((1, tq, heads // 2 * LANES), lambda i, j, k: (i, k, j)),
            pl.BlockSpec((1, seq, heads * LANES), lambda i, j, k: (i, 0, j)),
            pl.BlockSpec((1, n_blk, heads * LANES), lambda i, j, k: (i, 0, j)),
            pl.BlockSpec((seq, LANES), lambda i, j, k: (0, 0)),
        ],
        out_specs=pl.BlockSpec((1, tq, heads // 2 * LANES), lambda i, j, k: (i, k, j)),
        out_shape=jax.ShapeDtypeStruct((b, seq, MOBA_HEADS * HEAD_DIM), BF16),
        scratch_shapes=[pltpu.VMEM((heads, tq, LANES), F32), pltpu.VMEM((heads, tq, 2 * LANES), F32),
                        pltpu.VMEM((heads, tq, tq), F32), pltpu.VMEM((heads, tq, tq), F32)],
        compiler_params=_params(("parallel", "parallel", "arbitrary")),
        name="moba",
    )(mq, mkv, km, _block_onehot(seq, MOBA_BLOCK))


def _in_perm():
    d = HEAD_DIM
    perm = np.full((IN_COLS_P,), -1, np.int64)
    kv0 = NSA_Q_COLS
    gate0 = kv0 + NSA_KV_COLS
    moba0 = gate0 + NSA_GATE_COLS
    per_group = NSA_GROUP * 3

    def put(base, cols_a, cols_b):
        perm[base:base + LANES] = _slab_cols(cols_a, cols_b)

    def kv_cols(which, g):
        return kv0 + (which * NSA_KV_HEADS + g) * d + np.arange(d)

    for pr in range(NSA_HEADS // 2):
        put(SEC_Q + pr * LANES, 2 * pr * d + np.arange(d), (2 * pr + 1) * d + np.arange(d))
    for g in range(NSA_KV_HEADS):
        perm[SEC_GATE + g * LANES:SEC_GATE + g * LANES + per_group] = gate0 + g * per_group + np.arange(per_group)
        put(SEC_SW + g * 2 * LANES, kv_cols(2, g), kv_cols(3, g))
        put(SEC_SW + g * 2 * LANES + LANES, kv_cols(4, g), kv_cols(5, g))
        base = SEC_CMP + g * LANES
        perm[base:base + d] = kv_cols(0, g)
        perm[base + d:base + 2 * d] = kv_cols(1, g)
    for pr in range(MOBA_HEADS // 2):
        put(SEC_MQ + pr * LANES, moba0 + 2 * pr * d + np.arange(d), moba0 + (2 * pr + 1) * d + np.arange(d))
    for h in range(MOBA_HEADS):
        k_cols = moba0 + (MOBA_HEADS + h) * d + np.arange(d)
        v_cols = moba0 + (2 * MOBA_HEADS + h) * d + np.arange(d)
        put(SEC_MKV + h * LANES, *((k_cols, v_cols) if h % 2 == 0 else (v_cols, k_cols)))
    return perm


def _out_rows():
    d = HEAD_DIM
    nsa = [_slab_cols(2 * pr * d + np.arange(d), (2 * pr + 1) * d + np.arange(d)) for pr in range(NSA_HEADS // 2)]
    moba = [_slab_cols(NSA_Q_COLS + (2 * pr + 1) * d + np.arange(d), NSA_Q_COLS + 2 * pr * d + np.arange(d))
            for pr in range(MOBA_HEADS // 2)]
    return np.concatenate(nsa), np.concatenate(moba)


def _take_runs(w, idx, axis):
    parts, i = [], 0
    while i < len(idx):
        j = i + 1
        if idx[i] < 0:
            while j < len(idx) and idx[j] < 0:
                j += 1
            shape = list(w.shape)
            shape[axis] = j - i
            parts.append(jnp.zeros(shape, w.dtype))
        else:
            while j < len(idx) and idx[j] == idx[j - 1] + 1:
                j += 1
            parts.append(lax.slice_in_dim(w, int(idx[i]), int(idx[j - 1]) + 1, axis=axis))
        i = j
    return jnp.concatenate(parts, axis=axis)


def _overlap_table(seq):
    n_pad = seq // CMP_STRIDE
    n = np.arange(n_pad)[:, None]
    j = np.arange(LANES)[None, :]
    starts = n * CMP_STRIDE
    sb = j * SLC_BLOCK
    ov = (starts < sb + SLC_BLOCK) & (starts + CMP_LEN > sb) & (n < n_pad - 1) & (j < seq // SLC_BLOCK)
    return ov.astype(np.float32)


def _block_onehot(seq, block):
    onehot = np.arange(seq)[:, None] // block == np.arange(LANES)[None, :]
    return jnp.asarray(onehot.astype(np.float32), BF16)


def _rope_lane_tables(seq):
    half = ROT_DIM // 2
    inv_freq = jnp.power(ROPE_THETA, -(jnp.arange(0, ROT_DIM, 2, dtype=F32) / ROT_DIM))
    ang = jnp.arange(seq, dtype=F32)[:, None] * inv_freq[None, :]
    cos, sin = jnp.cos(ang), jnp.sin(ang)
    ones = jnp.ones((seq, PIECE - half), F32)
    zeros = jnp.zeros((seq, PIECE - half), F32)
    c_piece = jnp.concatenate([cos, ones], axis=1)
    lo_piece = jnp.concatenate([-sin, zeros], axis=1)
    hi_piece = jnp.concatenate([sin, zeros], axis=1)
    return (jnp.concatenate([c_piece] * 4, axis=1),
            jnp.concatenate([lo_piece, lo_piece, hi_piece, hi_piece], axis=1))


def _mixer(x2, b, seq, g_mix, w_p, layer, pos_ck, w_ck1, w_ck2, pos_cv, w_cv1, w_cv2, tabs, ov):
    qp, qr, gt, sw, cmp_in, mq, mkv, km = _inproj(x2, g_mix, w_p, layer, tabs, seq)

    def halves(w_k, w_v):
        wk = w_k.reshape(2, CMP_STRIDE, HEAD_DIM, CMP_HIDDEN)
        wv = w_v.reshape(2, CMP_STRIDE, HEAD_DIM, CMP_HIDDEN)
        z = jnp.zeros_like(wk)
        return jnp.concatenate([jnp.concatenate([wk, z], axis=3), jnp.concatenate([z, wv], axis=3)],
                               axis=2).astype(BF16)

    w1 = halves(w_ck1, w_cv1)
    pos = jnp.concatenate([pos_ck, pos_cv], axis=1).reshape(2, CMP_STRIDE, LANES)
    z2 = jnp.zeros_like(w_ck2)
    w2 = jnp.concatenate([jnp.concatenate([w_ck2, z2], axis=1),
                          jnp.concatenate([z2, w_cv2], axis=1)], axis=0).astype(BF16)
    w2 = _take_runs(w2, _slab_cols(np.arange(HEAD_DIM), HEAD_DIM + np.arange(HEAD_DIM)), axis=1)
    kvc = _compress(cmp_in.reshape(b, seq, -1), pos, w1[0], w1[1], w2)

    o_nsa = _nsa(qp.reshape(b, seq, -1), qr.reshape(b, seq, -1), gt.reshape(b, seq, -1), kvc,
                 sw.reshape(b, seq, -1), ov)
    o_moba = _moba(mq.reshape(b, seq, -1), mkv.reshape(b, seq, -1), km.reshape(b, seq // MOBA_BLOCK, -1))

    n = b * seq
    return o_nsa.reshape(n, -1), o_moba.reshape(n, -1)


def kernel(x, norm_ffn1, w_ffn1_gate, w_ffn1_up, w_ffn1_down, norm_mix, w_in, pos_ck, w_ck1, w_ck2,
           pos_cv, w_cv1, w_cv2, w_out, norm_ffn2, w_ffn2_gate, w_ffn2_up, w_ffn2_down, norm_final):
    b, seq, d = x.shape
    assert d == D_MODEL and seq % PROJ_ROWS == 0 and SLC_TOPN <= seq // SLC_BLOCK <= LANES
    depth = norm_ffn1.shape[0]
    tabs = _rope_lane_tables(seq)
    ov = jnp.asarray(_overlap_table(seq), BF16)
    x2 = x.reshape(b * seq, d)
    w_p = _take_runs(w_in.astype(BF16), _in_perm(), axis=2)
    w_out_b = w_out.astype(BF16)
    rows_nsa, rows_moba = _out_rows()
    wa = _take_runs(w_out_b, rows_nsa, axis=1)
    wm = _take_runs(w_out_b, rows_moba, axis=1)
    wg1, wu1, wd1, wg2, wu2, wd2 = (w.astype(BF16) for w in (w_ffn1_gate, w_ffn1_up, w_ffn1_down,
                                                            w_ffn2_gate, w_ffn2_up, w_ffn2_down))
    for l in range(depth):
        x2 = _ffn(x2, norm_ffn1[l], wg1, wu1, wd1, l)
        o_nsa, o_moba = _mixer(x2, b, seq, norm_mix[l], w_p, l, pos_ck[l], w_ck1[l], w_ck2[l],
                               pos_cv[l], w_cv1[l], w_cv2[l], tabs, ov)
        x2 = _ffn(x2, norm_ffn2[l], wg2, wu2, wd2, l, mix=(o_nsa, o_moba, wa[l], wm[l]),
                  g_final=norm_final if l == depth - 1 else None)
    return x2.reshape(b, seq, d)
```

```python
import functools

import numpy as np
import jax
import jax.numpy as jnp
from jax import lax
from jax.experimental import pallas as pl
from jax.experimental.pallas import tpu as pltpu

D_MODEL = 1024
HEAD_DIM = 64
NSA_HEADS = 8
NSA_KV_HEADS = 2
NSA_GROUP = NSA_HEADS // NSA_KV_HEADS
MOBA_HEADS = 8
ROT_DIM = HEAD_DIM // 4
ROPE_THETA = 500000.0
CMP_LEN = 32
CMP_STRIDE = 16
CMP_HIDDEN = 256
SLC_BLOCK = 64
SLC_TOPN = 16
WINDOW = 512
MOBA_BLOCK = 256
MOBA_TOPK = 3
D_FF = 2816
EPS = 1e-6
NEG = -1e30
FORCE_BONUS = 1e4
SCALE = HEAD_DIM ** -0.5

NSA_Q_COLS = NSA_HEADS * HEAD_DIM
NSA_KV_COLS = 3 * 2 * NSA_KV_HEADS * HEAD_DIM
NSA_GATE_COLS = NSA_HEADS * 3
MOBA_COLS = 3 * MOBA_HEADS * HEAD_DIM

LANES = 128
SUBLANES = 8
VMEM_LIMIT = 56 * 1024 * 1024

SEC_Q = 0
SEC_GATE = 512
SEC_SW = 768
SEC_CMP = 1280
SEC_MQ = 1536
SEC_MKV = 2048
IN_COLS_P = 3072

F32 = jnp.float32
BF16 = jnp.bfloat16

FFN_ROWS = 1024
FFN_CHUNK = 256
PROJ_ROWS = 1024
NSA_TQ = 256
NSA_KC = 512
MOBA_HEADS_PER_STEP = 4
MOBA_TQ = 2 * MOBA_BLOCK
MASK_BIAS = 2.0 ** 100


def _dot(a, b):
    return jnp.dot(a, b, preferred_element_type=F32)


def _dot_nt(a, b):
    return lax.dot_general(a, b, (((1,), (1,)), ((), ())), preferred_element_type=F32)


def _rms(x, g):
    return x * lax.rsqrt(jnp.mean(x * x, axis=-1, keepdims=True) + EPS) * g


def _params(sem):
    return pltpu.CompilerParams(dimension_semantics=sem, vmem_limit_bytes=VMEM_LIMIT)


def _resident(shape, layer=None):
    nd = len(shape)
    if layer is None:
        return pl.BlockSpec(shape, lambda *_: (0,) * nd, pipeline_mode=pl.Buffered(1))
    return pl.BlockSpec((None,) + tuple(shape), lambda *_: (layer,) + (0,) * nd,
                        pipeline_mode=pl.Buffered(1))


def _ffn_kernel(*refs, mix, final_norm):
    refs = list(refs)
    x_ref = refs.pop(0)
    if mix:
        a_ref, mo_ref, wa_ref, wm_ref = (refs.pop(0) for _ in range(4))
    g_ref, wg_ref, wu_ref, wd_ref = (refs.pop(0) for _ in range(4))
    gf_ref = refs.pop(0) if final_norm else None
    o_ref, h_ref = refs
    x = x_ref[...]
    if mix:
        x = x + _dot(a_ref[...], wa_ref[...]) + _dot(mo_ref[...], wm_ref[...])
    xb = _rms(x, g_ref[...]).astype(BF16)
    for c in range(D_FF // FFN_CHUNK):
        sl = slice(c * FFN_CHUNK, (c + 1) * FFN_CHUNK)
        gate = _dot(xb, wg_ref[:, sl])
        up = _dot(xb, wu_ref[:, sl])
        h_ref[:, sl] = (jax.nn.silu(gate) * up).astype(BF16)
    y = x + 0.5 * _dot(h_ref[...], wd_ref[...])
    if final_norm:
        y = _rms(y, gf_ref[...])
    o_ref[...] = y


def _ffn(x2, g, wg, wu, wd, layer, mix=None, g_final=None):
    n = x2.shape[0]
    final_norm = g_final is not None

    def rows(width):
        return pl.BlockSpec((FFN_ROWS, width), lambda i: (i, 0))

    in_specs, args = [rows(D_MODEL)], [x2]
    if mix is not None:
        o_nsa, o_moba, wa, wm = mix
        in_specs += [rows(o_nsa.shape[1]), rows(o_moba.shape[1]), _resident(wa.shape), _resident(wm.shape)]
        args += [o_nsa, o_moba, wa, wm]
    in_specs += [_resident((1, D_MODEL)), _resident((D_MODEL, D_FF), layer),
                 _resident((D_MODEL, D_FF), layer), _resident((D_FF, D_MODEL), layer)]
    args += [g.reshape(1, D_MODEL), wg, wu, wd]
    if final_norm:
        in_specs.append(_resident((1, D_MODEL)))
        args.append(g_final.reshape(1, D_MODEL))
    return pl.pallas_call(
        functools.partial(_ffn_kernel, mix=mix is not None, final_norm=final_norm),
        grid=(n // FFN_ROWS,),
        in_specs=in_specs,
        out_specs=rows(D_MODEL),
        out_shape=jax.ShapeDtypeStruct((n, D_MODEL), F32),
        scratch_shapes=[pltpu.VMEM((FFN_ROWS, D_FF), BF16)],
        compiler_params=_params(("parallel",)),
        name="ffn" + ("_mix" if mix is not None else "") + ("_final" if final_norm else ""),
    )(*args)


def _rope_tables(c, sl, sr, mode):
    if mode == "both":
        return c, sl, sr
    lane = lax.broadcasted_iota(jnp.int32, c.shape, 1)
    keep = lane < HEAD_DIM if mode == "lo" else lane >= HEAD_DIM
    return jnp.where(keep, c, 1.0), jnp.where(keep, sl, 0.0), jnp.where(keep, sr, 0.0)


def _rope(v, tabs):
    c, sl, sr = tabs
    return v * c + pltpu.roll(v, LANES - ROT_DIM // 2, 1) * sl + pltpu.roll(v, ROT_DIM // 2, 1) * sr


def _inproj_kernel(x_ref, g_ref, w_ref, c_ref, sl_ref, sr_ref,
                   qp_ref, qr_ref, gt_ref, sw_ref, cmp_ref, mq_ref, mkv_ref, km_ref):
    xb = _rms(x_ref[...], g_ref[...]).astype(BF16)
    base = (c_ref[...], sl_ref[...], sr_ref[...])
    tabs = {m: _rope_tables(*base, m) for m in ("both", "lo", "hi")}

    qn = _dot(xb, w_ref[:, SEC_Q:SEC_GATE])
    lo = lax.broadcasted_iota(jnp.int32, (PROJ_ROWS, LANES), 1) < HEAD_DIM
    for pr in range(NSA_HEADS // 2):
        pair = qn[:, pr * LANES:(pr + 1) * LANES]
        for out_ref, v in ((qp_ref, pair), (qr_ref, _rope(pair, tabs["both"]))):
            for e in range(2):
                head = v if e == 0 else pltpu.roll(v, HEAD_DIM, 1)
                sl = slice((2 * pr + e) * LANES, (2 * pr + e + 1) * LANES)
                out_ref[:, sl] = jnp.where(lo, head * SCALE, 0.0).astype(BF16)
    gt_ref[...] = jax.nn.sigmoid(_dot(xb, w_ref[:, SEC_GATE:SEC_SW]))
    cmp_ref[...] = _dot(xb, w_ref[:, SEC_CMP:SEC_MQ])

    sw = _dot(xb, w_ref[:, SEC_SW:SEC_CMP])
    for i in range(4):
        sl = slice(i * LANES, (i + 1) * LANES)
        sw_ref[:, sl] = _rope(sw[:, sl], tabs["lo"]).astype(BF16)

    mq = _dot(xb, w_ref[:, SEC_MQ:SEC_MKV])
    for i in range(4):
        sl = slice(i * LANES, (i + 1) * LANES)
        mq_ref[:, sl] = (_rope(mq[:, sl], tabs["both"]) * SCALE).astype(BF16)

    mkv = _dot(xb, w_ref[:, SEC_MKV:IN_COLS_P])
    nblk = PROJ_ROWS // MOBA_BLOCK
    for h in range(MOBA_HEADS):
        sl = slice(h * LANES, (h + 1) * LANES)
        r = _rope(mkv[:, sl], tabs["lo" if h % 2 == 0 else "hi"])
        mkv_ref[:, sl] = r.astype(BF16)
        km_ref[:, 0, sl] = jnp.mean(r.reshape(nblk, MOBA_BLOCK, LANES), axis=1)


def _inproj(x2, g, w_p, layer, tabs, seq):
    n = x2.shape[0]
    tiles_per_seq = seq // PROJ_ROWS
    nblk = PROJ_ROWS // MOBA_BLOCK

    def rows(width):
        return pl.BlockSpec((PROJ_ROWS, width), lambda i: (i, 0))

    tab = pl.BlockSpec((PROJ_ROWS, LANES), lambda i: (i % tiles_per_seq, 0))
    out_shapes = (
        jax.ShapeDtypeStruct((n, 1024), BF16),
        jax.ShapeDtypeStruct((n, 1024), BF16),
        jax.ShapeDtypeStruct((n, 256), F32),
        jax.ShapeDtypeStruct((n, 512), BF16),
        jax.ShapeDtypeStruct((n, 256), F32),
        jax.ShapeDtypeStruct((n, 512), BF16),
        jax.ShapeDtypeStruct((n, 1024), BF16),
        jax.ShapeDtypeStruct((n // MOBA_BLOCK, 1, 1024), F32),
    )
    out_specs = (rows(1024), rows(1024), rows(256), rows(512), rows(256), rows(512), rows(1024),
                 pl.BlockSpec((nblk, 1, 1024), lambda i: (i, 0, 0)))
    return pl.pallas_call(
        _inproj_kernel,
        grid=(n // PROJ_ROWS,),
        in_specs=[rows(D_MODEL), _resident((1, D_MODEL)), _resident((D_MODEL, IN_COLS_P), layer),
                  tab, tab, tab],
        out_specs=out_specs,
        out_shape=out_shapes,
        compiler_params=_params(("parallel",)),
        name="inproj",
    )(x2, g.reshape(1, D_MODEL), w_p, *tabs)


def _compress_kernel(x_ref, pos_ref, wa_ref, wb_ref, w2_ref, o_ref):
    n_pad = x_ref.shape[1] // CMP_STRIDE
    width = 2 * CMP_HIDDEN
    ya = jnp.zeros((n_pad, width), F32)
    zb = jnp.zeros((n_pad, width), F32)
    for l in range(CMP_STRIDE):
        h = x_ref[0, pl.ds(l, n_pad, stride=CMP_STRIDE), :]
        ya = ya + _dot((h + pos_ref[0, l:l + 1, :]).astype(BF16), wa_ref[l])
        zb = zb + _dot((h + pos_ref[1, l:l + 1, :]).astype(BF16), wb_ref[l])
    pre = ya + pltpu.roll(zb, n_pad - 1, 0)
    out = _dot(jax.nn.gelu(pre).astype(BF16), w2_ref[...])
    row = lax.broadcasted_iota(jnp.int32, out.shape, 0)
    o_ref[0, 0] = jnp.where(row < n_pad - 1, out, 0.0).astype(BF16)


def _compress(cmp_in, pos, wa, wb, w2):
    b, seq = cmp_in.shape[:2]
    g = NSA_KV_HEADS
    n_pad = seq // CMP_STRIDE
    return pl.pallas_call(
        _compress_kernel,
        grid=(b, g),
        in_specs=[pl.BlockSpec((1, seq, LANES), lambda i, j: (i, 0, j)),
                  _resident(pos.shape), _resident(wa.shape), _resident(wb.shape), _resident(w2.shape)],
        out_specs=pl.BlockSpec((1, 1, n_pad, LANES), lambda i, j: (i, j, 0, 0)),
        out_shape=jax.ShapeDtypeStruct((b, g, n_pad, LANES), BF16),
        compiler_params=_params(("parallel", "parallel")),
        name="compress",
    )(cmp_in, pos, wa, wb, w2)


def _rank_below(s_t, n_rows, limit):
    t = s_t.shape[1]
    n_grp = n_rows // SUBLANES
    grp = [s_t[g * SUBLANES:(g + 1) * SUBLANES] for g in range(n_grp)]
    cnt = [jnp.zeros((SUBLANES, t), F32) for _ in range(n_grp)]
    sub = lax.broadcasted_iota(jnp.int32, (SUBLANES, t), 0)
    for i in range(n_rows):
        ri = s_t[i:i + 1, :]
        for g in range(n_grp):
            if g > i // SUBLANES:
                beats = ri >= grp[g]
            elif g < i // SUBLANES:
                beats = ri > grp[g]
            else:
                beats = (ri > grp[g]) | ((ri == grp[g]) & (sub > i % SUBLANES))
            cnt[g] = cnt[g] + jnp.where(beats, 1.0, 0.0)
    return jnp.concatenate(cnt, axis=0) < limit


def _block_bias(sel_t, t):
    pad = jnp.zeros((LANES - sel_t.shape[0], t), F32)
    sel = jnp.concatenate([sel_t, pad], axis=0).T
    return ((sel - 1.0) * MASK_BIAS).astype(BF16)


def _with_ones(kv):
    return jnp.concatenate([kv, jnp.ones(kv.shape, kv.dtype)], axis=1)


def _softmax_step(read_s, kv, m_ref, acc_ref):
    m_old = m_ref[...]
    m_new = jnp.maximum(m_old, jnp.broadcast_to(jnp.max(read_s(), axis=-1, keepdims=True), m_old.shape))
    m_ref[...] = m_new
    alpha = jnp.exp(m_old - m_new)
    s = read_s()
    p = jnp.exp(s - jnp.concatenate([m_new] * (s.shape[1] // LANES), axis=1)).astype(BF16)
    acc_ref[...] = jnp.concatenate([alpha, alpha], axis=1) * acc_ref[...] + _dot(p, _with_ones(kv))


def _put_scores(q_ext, keys, c, s_ref):
    kvs, onehot = keys(c)
    for h, (q, kv) in enumerate(zip(q_ext, kvs)):
        s_ref[h] = _dot_nt(q, jnp.concatenate([kv, onehot], axis=1))


def _attend(q_ext, keys, n_past, causal, m_ref, acc_ref, s_a, s_b):
    put_scores = functools.partial(_put_scores, q_ext, keys)

    def softmax(c, s_ref, mask=None):
        for h, kv in enumerate(keys(c)[0]):
            if mask is None:
                read_s = functools.partial(lambda h: s_ref[h], h)
            else:
                read_s = functools.partial(lambda s: s, jnp.where(mask, s_ref[h], -MASK_BIAS))
            _softmax_step(read_s, kv, m_ref.at[h], acc_ref.at[h])

    def body(i, carry):
        c = 2 * i
        put_scores(c + 1, s_b)
        softmax(c, s_a)
        put_scores(c + 2, s_a)
        softmax(c + 1, s_b)
        return carry

    lax.fori_loop(0, n_past // 2, body, 0)

    @pl.when(n_past % 2 == 1)
    def _():
        put_scores(n_past, s_b)
        softmax(n_past - 1, s_a)
        softmax(n_past, s_b, causal)

    @pl.when(n_past % 2 == 0)
    def _():
        softmax(n_past, s_a, causal)


def _softmax_finish(acc):
    return acc[:, :LANES] / jnp.maximum(acc[:, LANES:], 1.0)


def _nsa_kernel(qp_ref, qr_ref, gt_ref, wb0_ref, wb1_ref, kvc_ref, sw_ref, ov_ref, oh_ref,
                o_ref, m_sc, acc_sc, sa_sc, sb_sc, qx_sc, part_sc, gslc_sc):
    wb_refs = (wb0_ref, wb1_ref)
    tq, r_heads, groups = NSA_TQ, NSA_GROUP, NSA_KV_HEADS
    rows = r_heads * tq
    q0 = pl.program_id(1) * tq
    lo = lax.broadcasted_iota(jnp.int32, (tq, LANES), 1) < HEAD_DIM
    n_slc = sw_ref.shape[1] // SLC_BLOCK
    blk_shift = SLC_BLOCK.bit_length() - 1

    def slab(i):
        return slice(i * LANES, (i + 1) * LANES)

    def head(o, r):
        return o[r * tq:(r + 1) * tq]

    def front(g, nb):
        def stack(ref):
            return jnp.concatenate([ref[0, :, slab(r_heads * g + r)] for r in range(r_heads)], axis=0)

        q_plain = stack(qp_ref)
        q_rot = stack(qr_ref)

        half = tq // 2
        o_win = []
        for u, wb_ref in enumerate(wb_refs):
            t0 = q0 + u * half
            w0 = pl.multiple_of(jnp.maximum(t0 - WINDOW, 0), half)
            kvw = sw_ref[0, pl.ds(w0, WINDOW + half), slab(2 * g + 1)]
            q_u = jnp.concatenate([qr_ref[0, u * half:(u + 1) * half, slab(r_heads * g + r)]
                                   for r in range(r_heads)], axis=0)
            s = _dot_nt(q_u, kvw).reshape(r_heads, half, WINDOW + half) + wb_ref[...][None]
            p = jnp.exp(s - jnp.max(s, axis=-1, keepdims=True)).reshape(r_heads * half, WINDOW + half)
            o_win.append(_softmax_finish(_dot(p.astype(BF16), _with_ones(kvw))))

        def win_head(r):
            return jnp.concatenate([o[r * half:(r + 1) * half] for o in o_win], axis=0)

        n_pad = min(kvc_ref.shape[2], -(-nb * (SLC_BLOCK // CMP_STRIDE) // LANES) * LANES)
        kvc = kvc_ref[0, g, :n_pad, :]
        n_idx = lax.broadcasted_iota(jnp.int32, (tq, n_pad), 1)
        t_idx = q0 + lax.broadcasted_iota(jnp.int32, (tq, n_pad), 0)
        valid_c = (n_idx * CMP_STRIDE + CMP_LEN - 1 <= t_idx)[None]
        s = jnp.where(valid_c, _dot_nt(q_plain, kvc).reshape(r_heads, tq, n_pad), NEG)
        m = jnp.maximum(jnp.max(s, axis=-1, keepdims=True), NEG / 2)
        p = jnp.exp(s - m)
        p = p / jnp.maximum(jnp.sum(p, axis=-1, keepdims=True), 1.0)
        o_cmp = _dot(p.reshape(rows, n_pad).astype(BF16), kvc)

        p_sum = p[0] + p[1] + p[2] + p[3]
        ov = ov_ref[:n_pad, :]
        hi = p_sum.astype(BF16)
        r1 = p_sum - hi.astype(F32)
        mid = r1.astype(BF16)
        low = (r1 - mid.astype(F32)).astype(BF16)
        imp = _dot(hi, ov) + _dot(mid, ov) + _dot(low, ov)

        j_idx = lax.broadcasted_iota(jnp.int32, (tq, LANES), 1)
        t_row = q0 + lax.broadcasted_iota(jnp.int32, (tq, LANES), 0)
        qblk = jnp.right_shift(t_row, blk_shift)
        forced = (j_idx == 0) | (j_idx == qblk) | (j_idx == qblk - 1)
        visible = j_idx * SLC_BLOCK <= t_row
        score = jnp.where(visible, jnp.where(forced, FORCE_BONUS, imp), NEG)
        if nb <= SLC_TOPN:
            sel_t = jnp.ones((nb, tq), F32)
        else:
            sel_t = _rank_below(score.T[:nb], nb, SLC_TOPN).astype(F32)
        bias = _block_bias(sel_t, tq)

        gate = gt_ref[0, :, slab(g)]

        def gate_of(r, branch):
            return jnp.broadcast_to(gate[:, 3 * r + branch:3 * r + branch + 1], (tq, LANES))

        for r in range(r_heads):
            part_sc[g, r] = gate_of(r, 0) * head(o_cmp, r) + gate_of(r, 2) * win_head(r)
            gslc_sc[g, r] = gate_of(r, 1)
        q_ext = jnp.concatenate([q_rot, jnp.concatenate([bias] * r_heads, axis=0)], axis=1)
        qx_sc[g] = q_ext
        return q_ext

    def slc_keys(c):
        ks = pl.ds(pl.multiple_of(c * NSA_KC, NSA_KC), NSA_KC)
        return tuple(sw_ref[0, ks, slab(2 * g)] for g in range(groups)), oh_ref[ks, :]

    bucket = (q0 + tq - 1) // (SLC_TOPN * SLC_BLOCK)
    for b, nb in enumerate(range(SLC_TOPN, n_slc + 1, SLC_TOPN)):
        @pl.when(bucket == b)
        def _(nb=nb):
            q_ext = [front(g, nb) for g in range(groups)]
            m_sc[...] = jnp.full(m_sc.shape, NEG, F32)
            acc_sc[...] = jnp.zeros(acc_sc.shape, F32)
            _put_scores(q_ext, slc_keys, 0, sa_sc)

    c_last = q0 // NSA_KC
    kpos = c_last * NSA_KC + lax.broadcasted_iota(jnp.int32, (rows, NSA_KC), 1)
    tpos = q0 + (lax.broadcasted_iota(jnp.int32, (rows, NSA_KC), 0) & (tq - 1))
    _attend([qx_sc[g] for g in range(groups)], slc_keys, c_last, kpos <= tpos, m_sc, acc_sc, sa_sc, sb_sc)

    for g in range(groups):
        o_slc = _softmax_finish(acc_sc[g])
        outs = [part_sc[g, r] + gslc_sc[g, r] * head(o_slc, r) for r in range(r_heads)]
        for pr in range(r_heads // 2):
            pair = jnp.where(lo, pltpu.roll(outs[2 * pr], HEAD_DIM, 1), outs[2 * pr + 1])
            o_ref[0, :, slab(2 * g + pr)] = pair.astype(BF16)


def _window_bias(seq, half):
    offsets = sorted({t0 - max(t0 - WINDOW, 0) for t0 in range(0, seq, half)})
    i = np.arange(half)[:, None]
    c = np.arange(WINDOW + half)[None, :]
    tabs = []
    for off in offsets:
        d = off + i - c
        tabs.append(np.where((d >= 0) & (d < WINDOW), 0.0, NEG))
    return np.stack(tabs).astype(np.float32)


def _nsa(qp, qr, gt, kvc, sw, ov):
    b, seq = qp.shape[:2]
    groups = NSA_KV_HEADS
    tq = NSA_TQ
    rows = NSA_GROUP * tq
    n_pad = kvc.shape[2]
    half = tq // 2
    wb = jnp.asarray(_window_bias(seq, half))
    last = wb.shape[0] - 1

    def band(u):
        return pl.BlockSpec((None, half, WINDOW + half), lambda i, k: (jnp.minimum(2 * k + u, last), 0, 0))

    return pl.pallas_call(
        _nsa_kernel,
        grid=(b, seq // tq),
        in_specs=[
            pl.BlockSpec((1, tq, NSA_HEADS * LANES), lambda i, k: (i, k, 0)),
            pl.BlockSpec((1, tq, NSA_HEADS * LANES), lambda i, k: (i, k, 0)),
            pl.BlockSpec((1, tq, groups * LANES), lambda i, k: (i, k, 0)),
            band(0), band(1),
            pl.BlockSpec((1, groups, n_pad, LANES), lambda i, k: (i, 0, 0, 0)),
            pl.BlockSpec((1, seq, 2 * groups * LANES), lambda i, k: (i, 0, 0)),
            pl.BlockSpec((n_pad, LANES), lambda i, k: (0, 0)),
            pl.BlockSpec((seq, LANES), lambda i, k: (0, 0)),
        ],
        out_specs=pl.BlockSpec((1, tq, NSA_Q_COLS), lambda i, k: (i, k, 0)),
        out_shape=jax.ShapeDtypeStruct((b, seq, NSA_Q_COLS), BF16),
        scratch_shapes=[pltpu.VMEM((groups, rows, LANES), F32), pltpu.VMEM((groups, rows, 2 * LANES), F32),
                        pltpu.VMEM((groups, rows, NSA_KC), F32), pltpu.VMEM((groups, rows, NSA_KC), F32),
                        pltpu.VMEM((groups, rows, 2 * LANES), BF16),
                        pltpu.VMEM((groups, NSA_GROUP, tq, LANES), F32), pltpu.VMEM((groups, NSA_GROUP, tq, LANES), F32)],
        compiler_params=_params(("parallel", "arbitrary")),
        name="nsa",
    )(qp, qr, gt, wb, wb, kvc, sw, ov, _block_onehot(seq, SLC_BLOCK))


def _moba_kernel(q_ref, kv_ref, km_ref, oh_ref, o_ref, m_sc, acc_sc, sa_sc, sb_sc):
    tq = MOBA_TQ
    heads = MOBA_HEADS_PER_STEP
    q0 = pl.multiple_of(pl.program_id(2) * tq, tq)
    lo = lax.broadcasted_iota(jnp.int32, (tq, LANES), 1) < HEAD_DIM
    n_blk = km_ref.shape[1]
    blk_shift = MOBA_BLOCK.bit_length() - 1
    blk = lax.broadcasted_iota(jnp.int32, (n_blk, tq), 0)
    qblk = jnp.right_shift(q0 + lax.broadcasted_iota(jnp.int32, (n_blk, tq), 1), blk_shift)
    past = blk < qblk

    def slab(h):
        return slice(h * LANES, (h + 1) * LANES)

    q_ext = []
    for h in range(heads):
        q = jnp.where(lo if h % 2 == 0 else ~lo, q_ref[0, :, slab(h // 2)], jnp.zeros((), BF16))
        gate_t = _dot_nt(km_ref[0, :, slab(h)].astype(BF16), q)
        top = _rank_below(jnp.where(past, gate_t, NEG), n_blk, MOBA_TOPK)
        bias = _block_bias(((top & past) | (blk == qblk)).astype(F32), tq)
        q_ext.append(jnp.concatenate([q, bias], axis=1))

    m_sc[...] = jnp.full(m_sc.shape, NEG, F32)
    acc_sc[...] = jnp.zeros(acc_sc.shape, F32)

    def keys(c):
        ks = pl.ds(pl.multiple_of(c * tq, tq), tq)
        return tuple(kv_ref[0, ks, slab(h)] for h in range(heads)), oh_ref[ks, :]

    causal = lax.broadcasted_iota(jnp.int32, (tq, tq), 1) <= lax.broadcasted_iota(jnp.int32, (tq, tq), 0)
    _put_scores(q_ext, keys, 0, sa_sc)
    _attend(q_ext, keys, pl.program_id(2), causal, m_sc, acc_sc, sa_sc, sb_sc)

    for pr in range(heads // 2):
        even = _softmax_finish(acc_sc[2 * pr])
        odd = _softmax_finish(acc_sc[2 * pr + 1])
        o_ref[0, :, slab(pr)] = jnp.where(lo, odd, even).astype(BF16)


def _moba(mq, mkv, km):
    b, seq = mq.shape[:2]
    tq = MOBA_TQ
    n_blk = seq // MOBA_BLOCK
    heads = MOBA_HEADS_PER_STEP
    return pl.pallas_call(
        _moba_kernel,
        grid=(b, MOBA_HEADS // heads, seq // tq),
        in_specs=[
            pl.BlockSpec((1, tq, heads // 2 * LANES), lambda i, j, k: (i, k, j)),
            pl.BlockSpec((1, seq, heads * LANES), lambda i, j, k: (i, 0, j)),
            pl.BlockSpec((1, n_blk, heads * LANES), lambda i, j, k: (i, 0, j)),
            pl.BlockSpec((seq, LANES), lambda i, j, k: (0, 0)),
        ],
        out_specs=pl.BlockSpec((1, tq, heads // 2 * LANES), lambda i, j, k: (i, k, j)),
        out_shape=jax.ShapeDtypeStruct((b, seq, MOBA_HEADS * HEAD_DIM), BF16),
        scratch_shapes=[pltpu.VMEM((heads, tq, LANES), F32), pltpu.VMEM((heads, tq, 2 * LANES), F32),
                        pltpu.VMEM((heads, tq, tq), F32), pltpu.VMEM((heads, tq, tq), F32)],
        compiler_params=_params(("parallel", "parallel", "arbitrary")),
        name="moba",
    )(mq, mkv, km, _block_onehot(seq, MOBA_BLOCK))


def _in_perm():
    d = HEAD_DIM
    perm = np.full((IN_COLS_P,), -1, np.int64)
    perm[SEC_Q:SEC_Q + NSA_Q_COLS] = np.arange(NSA_Q_COLS)
    kv0 = NSA_Q_COLS
    gate0 = kv0 + NSA_KV_COLS
    moba0 = gate0 + NSA_GATE_COLS
    per_group = NSA_GROUP * 3
    for g in range(NSA_KV_HEADS):
        perm[SEC_GATE + g * LANES:SEC_GATE + g * LANES + per_group] = gate0 + g * per_group + np.arange(per_group)

    def kv_cols(which, g):
        return kv0 + (which * NSA_KV_HEADS + g) * d + np.arange(d)

    for g in range(NSA_KV_HEADS):
        base = SEC_SW + g * 2 * LANES
        perm[base:base + d] = kv_cols(2, g)
        perm[base + d:base + 2 * d] = kv_cols(3, g)
        perm[base + 2 * d:base + 3 * d] = kv_cols(4, g)
        perm[base + 3 * d:base + 4 * d] = kv_cols(5, g)
        base = SEC_CMP + g * LANES
        perm[base:base + d] = kv_cols(0, g)
        perm[base + d:base + 2 * d] = kv_cols(1, g)
    perm[SEC_MQ:SEC_MQ + MOBA_HEADS * d] = moba0 + np.arange(MOBA_HEADS * d)
    for h in range(MOBA_HEADS):
        k_cols = moba0 + (MOBA_HEADS + h) * d + np.arange(d)
        v_cols = moba0 + (2 * MOBA_HEADS + h) * d + np.arange(d)
        base = SEC_MKV + h * LANES
        first, second = (k_cols, v_cols) if h % 2 == 0 else (v_cols, k_cols)
        perm[base:base + d] = first
        perm[base + d:base + 2 * d] = second
    return perm


def _take_runs(w, idx, axis):
    parts, i = [], 0
    while i < len(idx):
        j = i + 1
        if idx[i] < 0:
            while j < len(idx) and idx[j] < 0:
                j += 1
            shape = list(w.shape)
            shape[axis] = j - i
            parts.append(jnp.zeros(shape, w.dtype))
        else:
            while j < len(idx) and idx[j] == idx[j - 1] + 1:
                j += 1
            parts.append(lax.slice_in_dim(w, int(idx[i]), int(idx[j - 1]) + 1, axis=axis))
        i = j
    return jnp.concatenate(parts, axis=axis)


def _moba_out_rows():
    d = HEAD_DIM
    rows = []
    for pr in range(MOBA_HEADS // 2):
        rows.append(NSA_Q_COLS + (2 * pr + 1) * d + np.arange(d))
        rows.append(NSA_Q_COLS + (2 * pr) * d + np.arange(d))
    return np.concatenate(rows)


def _overlap_table(seq):
    n_pad = seq // CMP_STRIDE
    n = np.arange(n_pad)[:, None]
    j = np.arange(LANES)[None, :]
    starts = n * CMP_STRIDE
    sb = j * SLC_BLOCK
    ov = (starts < sb + SLC_BLOCK) & (starts + CMP_LEN > sb) & (n < n_pad - 1) & (j < seq // SLC_BLOCK)
    return ov.astype(np.float32)


def _block_onehot(seq, block):
    onehot = np.arange(seq)[:, None] // block == np.arange(LANES)[None, :]
    return jnp.asarray(onehot.astype(np.float32), BF16)


def _rope_lane_tables(seq):
    half = ROT_DIM // 2
    inv_freq = jnp.power(ROPE_THETA, -(jnp.arange(0, ROT_DIM, 2, dtype=F32) / ROT_DIM))
    ang = jnp.arange(seq, dtype=F32)[:, None] * inv_freq[None, :]
    cos, sin = jnp.cos(ang), jnp.sin(ang)
    rest = HEAD_DIM - ROT_DIM
    zeros_h = jnp.zeros((seq, half), F32)
    c = jnp.concatenate([cos, cos, jnp.ones((seq, rest), F32)], axis=1)
    sl = jnp.concatenate([-sin, zeros_h, jnp.zeros((seq, rest), F32)], axis=1)
    sr = jnp.concatenate([zeros_h, sin, jnp.zeros((seq, rest), F32)], axis=1)
    return tuple(jnp.tile(t, (1, LANES // HEAD_DIM)) for t in (c, sl, sr))


def _mixer(x2, b, seq, g_mix, w_p, layer, pos_ck, w_ck1, w_ck2, pos_cv, w_cv1, w_cv2, tabs, ov):
    qp, qr, gt, sw, cmp_in, mq, mkv, km = _inproj(x2, g_mix, w_p, layer, tabs, seq)

    def halves(w_k, w_v):
        wk = w_k.reshape(2, CMP_STRIDE, HEAD_DIM, CMP_HIDDEN)
        wv = w_v.reshape(2, CMP_STRIDE, HEAD_DIM, CMP_HIDDEN)
        z = jnp.zeros_like(wk)
        return jnp.concatenate([jnp.concatenate([wk, z], axis=3), jnp.concatenate([z, wv], axis=3)],
                               axis=2).astype(BF16)

    w1 = halves(w_ck1, w_cv1)
    pos = jnp.concatenate([pos_ck, pos_cv], axis=1).reshape(2, CMP_STRIDE, LANES)
    z2 = jnp.zeros_like(w_ck2)
    w2 = jnp.concatenate([jnp.concatenate([w_ck2, z2], axis=1),
                          jnp.concatenate([z2, w_cv2], axis=1)], axis=0).astype(BF16)
    kvc = _compress(cmp_in.reshape(b, seq, -1), pos, w1[0], w1[1], w2)

    o_nsa = _nsa(qp.reshape(b, seq, -1), qr.reshape(b, seq, -1), gt.reshape(b, seq, -1), kvc,
                 sw.reshape(b, seq, -1), ov)
    o_moba = _moba(mq.reshape(b, seq, -1), mkv.reshape(b, seq, -1), km.reshape(b, seq // MOBA_BLOCK, -1))

    n = b * seq
    return o_nsa.reshape(n, -1), o_moba.reshape(n, -1)


def kernel(x, norm_ffn1, w_ffn1_gate, w_ffn1_up, w_ffn1_down, norm_mix, w_in, pos_ck, w_ck1, w_ck2,
           pos_cv, w_cv1, w_cv2, w_out, norm_ffn2, w_ffn2_gate, w_ffn2_up, w_ffn2_down, norm_final):
    b, seq, d = x.shape
    assert d == D_MODEL and seq % PROJ_ROWS == 0 and SLC_TOPN <= seq // SLC_BLOCK <= LANES
    depth = norm_ffn1.shape[0]
    tabs = _rope_lane_tables(seq)
    ov = jnp.asarray(_overlap_table(seq), BF16)
    x2 = x.reshape(b * seq, d)
    w_p = _take_runs(w_in.astype(BF16), _in_perm(), axis=2)
    w_out_b = w_out.astype(BF16)
    wa = w_out_b[:, :NSA_Q_COLS]
    wm = _take_runs(w_out_b, _moba_out_rows(), axis=1)
    wg1, wu1, wd1, wg2, wu2, wd2 = (w.astype(BF16) for w in (w_ffn1_gate, w_ffn1_up, w_ffn1_down,
                                                            w_ffn2_gate, w_ffn2_up, w_ffn2_down))
    for l in range(depth):
        x2 = _ffn(x2, norm_ffn1[l], wg1, wu1, wd1, l)
        o_nsa, o_moba = _mixer(x2, b, seq, norm_mix[l], w_p, l, pos_ck[l], w_ck1[l], w_ck2[l],
                               pos_cv[l], w_cv1[l], w_cv2[l], tabs, ov)
        x2 = _ffn(x2, norm_ffn2[l], wg2, wu2, wd2, l, mix=(o_nsa, o_moba, wa[l], wm[l]),
                  g_final=norm_final if l == depth - 1 else None)
    return x2.reshape(b, seq, d)
```

```python
import functools

import numpy as np
import jax
import jax.numpy as jnp
from jax import lax
from jax.experimental import pallas as pl
from jax.experimental.pallas import tpu as pltpu

D_MODEL = 1024
HEAD_DIM = 64
NSA_HEADS = 8
NSA_KV_HEADS = 2
NSA_GROUP = NSA_HEADS // NSA_KV_HEADS
MOBA_HEADS = 8
ROT_DIM = HEAD_DIM // 4
ROPE_THETA = 500000.0
CMP_LEN = 32
CMP_STRIDE = 16
CMP_HIDDEN = 256
SLC_BLOCK = 64
SLC_TOPN = 16
WINDOW = 512
MOBA_BLOCK = 256
MOBA_TOPK = 3
D_FF = 2816
EPS = 1e-6
NEG = -1e30
FORCE_BONUS = 1e4
SCALE = HEAD_DIM ** -0.5

NSA_Q_COLS = NSA_HEADS * HEAD_DIM
NSA_KV_COLS = 3 * 2 * NSA_KV_HEADS * HEAD_DIM
NSA_GATE_COLS = NSA_HEADS * 3
MOBA_COLS = 3 * MOBA_HEADS * HEAD_DIM

LANES = 128
SUBLANES = 8
VMEM_LIMIT = 56 * 1024 * 1024

SEC_Q = 0
SEC_GATE = 512
SEC_SW = 768
SEC_CMP = 1280
SEC_MQ = 1536
SEC_MKV = 2048
IN_COLS_P = 3072

F32 = jnp.float32
BF16 = jnp.bfloat16

FFN_ROWS = 1024
FFN_CHUNK = 256
PROJ_ROWS = 512
NSA_TQ = 256
NSA_KC = 512
MOBA_HEADS_PER_STEP = 4
MOBA_TQ = 2 * MOBA_BLOCK
LOG2E = 1.4426950408889634
MASK_BIAS = 2.0 ** 100


def _dot(a, b):
    return jnp.dot(a, b, preferred_element_type=F32)


def _dot_nt(a, b):
    return lax.dot_general(a, b, (((1,), (1,)), ((), ())), preferred_element_type=F32)


def _rms(x, g):
    return x * lax.rsqrt(jnp.mean(x * x, axis=-1, keepdims=True) + EPS) * g


def _params(sem):
    return pltpu.CompilerParams(dimension_semantics=sem, vmem_limit_bytes=VMEM_LIMIT)


def _resident(shape, layer=None):
    nd = len(shape)
    if layer is None:
        return pl.BlockSpec(shape, lambda *_: (0,) * nd, pipeline_mode=pl.Buffered(1))
    return pl.BlockSpec((None,) + tuple(shape), lambda *_: (layer,) + (0,) * nd,
                        pipeline_mode=pl.Buffered(1))


def _ffn_kernel(*refs, mix, final_norm):
    refs = list(refs)
    x_ref = refs.pop(0)
    if mix:
        a_ref, mo_ref, wa_ref, wm_ref = (refs.pop(0) for _ in range(4))
    g_ref, wg_ref, wu_ref, wd_ref = (refs.pop(0) for _ in range(4))
    gf_ref = refs.pop(0) if final_norm else None
    o_ref, h_ref = refs
    x = x_ref[...]
    if mix:
        x = x + _dot(a_ref[...], wa_ref[...]) + _dot(mo_ref[...], wm_ref[...])
    xb = _rms(x, g_ref[...]).astype(BF16)
    for c in range(D_FF // FFN_CHUNK):
        sl = slice(c * FFN_CHUNK, (c + 1) * FFN_CHUNK)
        gate = _dot(xb, wg_ref[:, sl])
        up = _dot(xb, wu_ref[:, sl])
        h_ref[:, sl] = (jax.nn.silu(gate) * up).astype(BF16)
    y = x + 0.5 * _dot(h_ref[...], wd_ref[...])
    if final_norm:
        y = _rms(y, gf_ref[...])
    o_ref[...] = y


def _ffn(x2, g, wg, wu, wd, layer, mix=None, g_final=None):
    n = x2.shape[0]
    final_norm = g_final is not None

    def rows(width):
        return pl.BlockSpec((FFN_ROWS, width), lambda i: (i, 0))

    in_specs, args = [rows(D_MODEL)], [x2]
    if mix is not None:
        o_nsa, o_moba, wa, wm = mix
        in_specs += [rows(o_nsa.shape[1]), rows(o_moba.shape[1]), _resident(wa.shape), _resident(wm.shape)]
        args += [o_nsa, o_moba, wa, wm]
    in_specs += [_resident((1, D_MODEL)), _resident((D_MODEL, D_FF), layer),
                 _resident((D_MODEL, D_FF), layer), _resident((D_FF, D_MODEL), layer)]
    args += [g.reshape(1, D_MODEL), wg, wu, wd]
    if final_norm:
        in_specs.append(_resident((1, D_MODEL)))
        args.append(g_final.reshape(1, D_MODEL))
    return pl.pallas_call(
        functools.partial(_ffn_kernel, mix=mix is not None, final_norm=final_norm),
        grid=(n // FFN_ROWS,),
        in_specs=in_specs,
        out_specs=rows(D_MODEL),
        out_shape=jax.ShapeDtypeStruct((n, D_MODEL), F32),
        scratch_shapes=[pltpu.VMEM((FFN_ROWS, D_FF), BF16)],
        compiler_params=_params(("parallel",)),
        name="ffn" + ("_mix" if mix is not None else "") + ("_final" if final_norm else ""),
    )(*args)


def _rope_tables(c, sl, sr, mode):
    if mode == "both":
        return c, sl, sr
    lane = lax.broadcasted_iota(jnp.int32, c.shape, 1)
    keep = lane < HEAD_DIM if mode == "lo" else lane >= HEAD_DIM
    return jnp.where(keep, c, 1.0), jnp.where(keep, sl, 0.0), jnp.where(keep, sr, 0.0)


def _rope(v, tabs):
    c, sl, sr = tabs
    return v * c + pltpu.roll(v, LANES - ROT_DIM // 2, 1) * sl + pltpu.roll(v, ROT_DIM // 2, 1) * sr


def _inproj_kernel(x_ref, g_ref, w_ref, c_ref, sl_ref, sr_ref,
                   qp_ref, qr_ref, gt_ref, sw_ref, cmp_ref, mq_ref, mkv_ref, km_ref):
    xb = _rms(x_ref[...], g_ref[...]).astype(BF16)
    base = (c_ref[...], sl_ref[...], sr_ref[...])
    tabs = {m: _rope_tables(*base, m) for m in ("both", "lo", "hi")}

    qn = _dot(xb, w_ref[:, SEC_Q:SEC_GATE])
    lo = lax.broadcasted_iota(jnp.int32, (PROJ_ROWS, LANES), 1) < HEAD_DIM
    for pr in range(NSA_HEADS // 2):
        pair = qn[:, pr * LANES:(pr + 1) * LANES]
        for out_ref, v, scale in ((qp_ref, pair, SCALE), (qr_ref, _rope(pair, tabs["both"]), SCALE * LOG2E)):
            for e in range(2):
                head = v if e == 0 else pltpu.roll(v, HEAD_DIM, 1)
                sl = slice((2 * pr + e) * LANES, (2 * pr + e + 1) * LANES)
                out_ref[:, sl] = jnp.where(lo, head * scale, 0.0).astype(BF16)
    gt_ref[...] = jax.nn.sigmoid(_dot(xb, w_ref[:, SEC_GATE:SEC_SW]))
    cmp_ref[...] = _dot(xb, w_ref[:, SEC_CMP:SEC_MQ])

    sw = _dot(xb, w_ref[:, SEC_SW:SEC_CMP])
    for i in range(4):
        sl = slice(i * LANES, (i + 1) * LANES)
        sw_ref[:, sl] = _rope(sw[:, sl], tabs["lo"]).astype(BF16)

    mq = _dot(xb, w_ref[:, SEC_MQ:SEC_MKV])
    for i in range(4):
        sl = slice(i * LANES, (i + 1) * LANES)
        mq_ref[:, sl] = (_rope(mq[:, sl], tabs["both"]) * (SCALE * LOG2E)).astype(BF16)

    mkv = _dot(xb, w_ref[:, SEC_MKV:IN_COLS_P])
    nblk = PROJ_ROWS // MOBA_BLOCK
    for h in range(MOBA_HEADS):
        sl = slice(h * LANES, (h + 1) * LANES)
        r = _rope(mkv[:, sl], tabs["lo" if h % 2 == 0 else "hi"])
        mkv_ref[:, sl] = r.astype(BF16)
        km_ref[:, 0, sl] = jnp.mean(r.reshape(nblk, MOBA_BLOCK, LANES), axis=1)


def _inproj(x2, g, w_p, layer, tabs, seq):
    n = x2.shape[0]
    tiles_per_seq = seq // PROJ_ROWS
    nblk = PROJ_ROWS // MOBA_BLOCK

    def rows(width):
        return pl.BlockSpec((PROJ_ROWS, width), lambda i: (i, 0))

    tab = pl.BlockSpec((PROJ_ROWS, LANES), lambda i: (i % tiles_per_seq, 0))
    out_shapes = (
        jax.ShapeDtypeStruct((n, 1024), BF16),
        jax.ShapeDtypeStruct((n, 1024), BF16),
        jax.ShapeDtypeStruct((n, 256), F32),
        jax.ShapeDtypeStruct((n, 512), BF16),
        jax.ShapeDtypeStruct((n, 256), F32),
        jax.ShapeDtypeStruct((n, 512), BF16),
        jax.ShapeDtypeStruct((n, 1024), BF16),
        jax.ShapeDtypeStruct((n // MOBA_BLOCK, 1, 1024), F32),
    )
    out_specs = (rows(1024), rows(1024), rows(256), rows(512), rows(256), rows(512), rows(1024),
                 pl.BlockSpec((nblk, 1, 1024), lambda i: (i, 0, 0)))
    return pl.pallas_call(
        _inproj_kernel,
        grid=(n // PROJ_ROWS,),
        in_specs=[rows(D_MODEL), _resident((1, D_MODEL)), _resident((D_MODEL, IN_COLS_P), layer),
                  tab, tab, tab],
        out_specs=out_specs,
        out_shape=out_shapes,
        compiler_params=_params(("parallel",)),
        name="inproj",
    )(x2, g.reshape(1, D_MODEL), w_p, *tabs)


def _compress_kernel(x_ref, pos_ref, wa_ref, wb_ref, w2_ref, o_ref):
    n_pad = x_ref.shape[1] // CMP_STRIDE
    width = 2 * CMP_HIDDEN
    ya = jnp.zeros((n_pad, width), F32)
    zb = jnp.zeros((n_pad, width), F32)
    for l in range(CMP_STRIDE):
        h = x_ref[0, pl.ds(l, n_pad, stride=CMP_STRIDE), :]
        ya = ya + _dot((h + pos_ref[0, l:l + 1, :]).astype(BF16), wa_ref[l])
        zb = zb + _dot((h + pos_ref[1, l:l + 1, :]).astype(BF16), wb_ref[l])
    pre = ya + pltpu.roll(zb, n_pad - 1, 0)
    out = _dot(jax.nn.gelu(pre).astype(BF16), w2_ref[...])
    row = lax.broadcasted_iota(jnp.int32, out.shape, 0)
    o_ref[0, 0] = jnp.where(row < n_pad - 1, out, 0.0).astype(BF16)


def _compress(cmp_in, pos, wa, wb, w2):
    b, seq = cmp_in.shape[:2]
    g = NSA_KV_HEADS
    n_pad = seq // CMP_STRIDE
    return pl.pallas_call(
        _compress_kernel,
        grid=(b, g),
        in_specs=[pl.BlockSpec((1, seq, LANES), lambda i, j: (i, 0, j)),
                  _resident(pos.shape), _resident(wa.shape), _resident(wb.shape), _resident(w2.shape)],
        out_specs=pl.BlockSpec((1, 1, n_pad, LANES), lambda i, j: (i, j, 0, 0)),
        out_shape=jax.ShapeDtypeStruct((b, g, n_pad, LANES), BF16),
        compiler_params=_params(("parallel", "parallel")),
        name="compress",
    )(cmp_in, pos, wa, wb, w2)


def _rank_below(s_t, n_rows, limit):
    t = s_t.shape[1]
    n_grp = n_rows // SUBLANES
    grp = [s_t[g * SUBLANES:(g + 1) * SUBLANES] for g in range(n_grp)]
    cnt = [jnp.zeros((SUBLANES, t), F32) for _ in range(n_grp)]
    sub = lax.broadcasted_iota(jnp.int32, (SUBLANES, t), 0)
    for i in range(n_rows):
        ri = s_t[i:i + 1, :]
        for g in range(n_grp):
            if g > i // SUBLANES:
                beats = ri >= grp[g]
            elif g < i // SUBLANES:
                beats = ri > grp[g]
            else:
                beats = (ri > grp[g]) | ((ri == grp[g]) & (sub > i % SUBLANES))
            cnt[g] = cnt[g] + jnp.where(beats, 1.0, 0.0)
    return jnp.concatenate(cnt, axis=0) < limit


def _block_bias(sel_t, t):
    pad = jnp.zeros((LANES - sel_t.shape[0], t), F32)
    sel = jnp.concatenate([sel_t, pad], axis=0).T
    return ((sel - 1.0) * MASK_BIAS).astype(BF16)


def _with_ones(kv):
    return jnp.concatenate([kv, jnp.ones(kv.shape, kv.dtype)], axis=1)


def _softmax_step(read_s, kv, m_ref, acc_ref):
    m_old = m_ref[...]
    m_new = jnp.maximum(m_old, jnp.broadcast_to(jnp.max(read_s(), axis=-1, keepdims=True), m_old.shape))
    m_ref[...] = m_new
    alpha = jnp.exp2(m_old - m_new)
    s = read_s()
    p = jnp.exp2(s - jnp.concatenate([m_new] * (s.shape[1] // LANES), axis=1)).astype(BF16)
    acc_ref[...] = jnp.concatenate([alpha, alpha], axis=1) * acc_ref[...] + _dot(p, _with_ones(kv))


def _put_scores(q_ext, keys, c, s_ref):
    kvs, onehot = keys(c)
    for h, (q, kv) in enumerate(zip(q_ext, kvs)):
        s_ref[h] = _dot_nt(q, jnp.concatenate([kv, onehot], axis=1))


def _attend(q_ext, keys, n_past, causal, m_ref, acc_ref, s_a, s_b):
    put_scores = functools.partial(_put_scores, q_ext, keys)

    def softmax(c, s_ref, mask=None):
        for h, kv in enumerate(keys(c)[0]):
            if mask is None:
                read_s = functools.partial(lambda h: s_ref[h], h)
            else:
                read_s = functools.partial(lambda s: s, jnp.where(mask, s_ref[h], -MASK_BIAS))
            _softmax_step(read_s, kv, m_ref.at[h], acc_ref.at[h])

    def body(i, carry):
        c = 2 * i
        put_scores(c + 1, s_b)
        softmax(c, s_a)
        put_scores(c + 2, s_a)
        softmax(c + 1, s_b)
        return carry

    lax.fori_loop(0, n_past // 2, body, 0)

    @pl.when(n_past % 2 == 1)
    def _():
        put_scores(n_past, s_b)
        softmax(n_past - 1, s_a)
        softmax(n_past, s_b, causal)

    @pl.when(n_past % 2 == 0)
    def _():
        softmax(n_past, s_a, causal)


def _softmax_finish(acc):
    return acc[:, :LANES] / jnp.maximum(acc[:, LANES:], 1.0)


def _nsa_kernel(qp_ref, qr_ref, gt_ref, wb0_ref, wb1_ref, kvc_ref, sw_ref, ov_ref, oh_ref,
                o_ref, m_sc, acc_sc, sa_sc, sb_sc, qx_sc, part_sc, gslc_sc):
    wb_refs = (wb0_ref, wb1_ref)
    tq, r_heads, groups = NSA_TQ, NSA_GROUP, NSA_KV_HEADS
    rows = r_heads * tq
    q0 = pl.program_id(1) * tq
    lo = lax.broadcasted_iota(jnp.int32, (tq, LANES), 1) < HEAD_DIM
    n_slc = sw_ref.shape[1] // SLC_BLOCK
    blk_shift = SLC_BLOCK.bit_length() - 1

    def slab(i):
        return slice(i * LANES, (i + 1) * LANES)

    def head(o, r):
        return o[r * tq:(r + 1) * tq]

    def front(g, nb):
        def stack(ref):
            return jnp.concatenate([ref[0, :, slab(r_heads * g + r)] for r in range(r_heads)], axis=0)

        q_plain = stack(qp_ref)
        q_rot = stack(qr_ref)

        half = tq // 2
        o_win = []
        for u, wb_ref in enumerate(wb_refs):
            t0 = q0 + u * half
            w0 = pl.multiple_of(jnp.maximum(t0 - WINDOW, 0), half)
            kvw = sw_ref[0, pl.ds(w0, WINDOW + half), slab(2 * g + 1)]
            q_u = jnp.concatenate([qr_ref[0, u * half:(u + 1) * half, slab(r_heads * g + r)]
                                   for r in range(r_heads)], axis=0)
            s = _dot_nt(q_u, kvw).reshape(r_heads, half, WINDOW + half) + wb_ref[...][None]
            p = jnp.exp2(s - jnp.max(s, axis=-1, keepdims=True)).reshape(r_heads * half, WINDOW + half)
            o_win.append(_softmax_finish(_dot(p.astype(BF16), _with_ones(kvw))))

        def win_head(r):
            return jnp.concatenate([o[r * half:(r + 1) * half] for o in o_win], axis=0)

        n_pad = min(kvc_ref.shape[2], -(-nb * (SLC_BLOCK // CMP_STRIDE) // LANES) * LANES)
        kvc = kvc_ref[0, g, :n_pad, :]
        n_idx = lax.broadcasted_iota(jnp.int32, (tq, n_pad), 1)
        t_idx = q0 + lax.broadcasted_iota(jnp.int32, (tq, n_pad), 0)
        valid_c = (n_idx * CMP_STRIDE + CMP_LEN - 1 <= t_idx)[None]
        s = jnp.where(valid_c, _dot_nt(q_plain, kvc).reshape(r_heads, tq, n_pad), NEG)
        m = jnp.maximum(jnp.max(s, axis=-1, keepdims=True), NEG / 2)
        p = jnp.exp(s - m)
        p = p / jnp.maximum(jnp.sum(p, axis=-1, keepdims=True), 1.0)
        o_cmp = _dot(p.reshape(rows, n_pad).astype(BF16), kvc)

        p_sum = p[0] + p[1] + p[2] + p[3]
        ov = ov_ref[:n_pad, :]
        hi = p_sum.astype(BF16)
        r1 = p_sum - hi.astype(F32)
        mid = r1.astype(BF16)
        low = (r1 - mid.astype(F32)).astype(BF16)
        imp = _dot(hi, ov) + _dot(mid, ov) + _dot(low, ov)

        j_idx = lax.broadcasted_iota(jnp.int32, (tq, LANES), 1)
        t_row = q0 + lax.broadcasted_iota(jnp.int32, (tq, LANES), 0)
        qblk = jnp.right_shift(t_row, blk_shift)
        forced = (j_idx == 0) | (j_idx == qblk) | (j_idx == qblk - 1)
        visible = j_idx * SLC_BLOCK <= t_row
        score = jnp.where(visible, jnp.where(forced, FORCE_BONUS, imp), NEG)
        if nb <= SLC_TOPN:
            sel_t = jnp.ones((nb, tq), F32)
        else:
            sel_t = _rank_below(score.T[:nb], nb, SLC_TOPN).astype(F32)
        bias = _block_bias(sel_t, tq)

        gate = gt_ref[0, :, slab(g)]

        def gate_of(r, branch):
            return jnp.broadcast_to(gate[:, 3 * r + branch:3 * r + branch + 1], (tq, LANES))

        for r in range(r_heads):
            part_sc[g, r] = gate_of(r, 0) * head(o_cmp, r) + gate_of(r, 2) * win_head(r)
            gslc_sc[g, r] = gate_of(r, 1)
        q_ext = jnp.concatenate([q_rot, jnp.concatenate([bias] * r_heads, axis=0)], axis=1)
        qx_sc[g] = q_ext
        return q_ext

    def slc_keys(c):
        ks = pl.ds(pl.multiple_of(c * NSA_KC, NSA_KC), NSA_KC)
        return tuple(sw_ref[0, ks, slab(2 * g)] for g in range(groups)), oh_ref[ks, :]

    bucket = (q0 + tq - 1) // (SLC_TOPN * SLC_BLOCK)
    for b, nb in enumerate(range(SLC_TOPN, n_slc + 1, SLC_TOPN)):
        @pl.when(bucket == b)
        def _(nb=nb):
            q_ext = [front(g, nb) for g in range(groups)]
            m_sc[...] = jnp.full(m_sc.shape, NEG, F32)
            acc_sc[...] = jnp.zeros(acc_sc.shape, F32)
            _put_scores(q_ext, slc_keys, 0, sa_sc)

    c_last = q0 // NSA_KC
    kpos = c_last * NSA_KC + lax.broadcasted_iota(jnp.int32, (rows, NSA_KC), 1)
    tpos = q0 + (lax.broadcasted_iota(jnp.int32, (rows, NSA_KC), 0) & (tq - 1))
    _attend([qx_sc[g] for g in range(groups)], slc_keys, c_last, kpos <= tpos, m_sc, acc_sc, sa_sc, sb_sc)

    for g in range(groups):
        o_slc = _softmax_finish(acc_sc[g])
        outs = [part_sc[g, r] + gslc_sc[g, r] * head(o_slc, r) for r in range(r_heads)]
        for pr in range(r_heads // 2):
            pair = jnp.where(lo, pltpu.roll(outs[2 * pr], HEAD_DIM, 1), outs[2 * pr + 1])
            o_ref[0, :, slab(2 * g + pr)] = pair.astype(BF16)


def _window_bias(seq, half):
    offsets = sorted({t0 - max(t0 - WINDOW, 0) for t0 in range(0, seq, half)})
    i = np.arange(half)[:, None]
    c = np.arange(WINDOW + half)[None, :]
    tabs = []
    for off in offsets:
        d = off + i - c
        tabs.append(np.where((d >= 0) & (d < WINDOW), 0.0, NEG))
    return np.stack(tabs).astype(np.float32)


def _nsa(qp, qr, gt, kvc, sw, ov):
    b, seq = qp.shape[:2]
    groups = NSA_KV_HEADS
    tq = NSA_TQ
    rows = NSA_GROUP * tq
    n_pad = kvc.shape[2]
    half = tq // 2
    wb = jnp.asarray(_window_bias(seq, half))
    last = wb.shape[0] - 1

    def band(u):
        return pl.BlockSpec((None, half, WINDOW + half), lambda i, k: (jnp.minimum(2 * k + u, last), 0, 0))

    return pl.pallas_call(
        _nsa_kernel,
        grid=(b, seq // tq),
        in_specs=[
            pl.BlockSpec((1, tq, NSA_HEADS * LANES), lambda i, k: (i, k, 0)),
            pl.BlockSpec((1, tq, NSA_HEADS * LANES), lambda i, k: (i, k, 0)),
            pl.BlockSpec((1, tq, groups * LANES), lambda i, k: (i, k, 0)),
            band(0), band(1),
            pl.BlockSpec((1, groups, n_pad, LANES), lambda i, k: (i, 0, 0, 0)),
            pl.BlockSpec((1, seq, 2 * groups * LANES), lambda i, k: (i, 0, 0)),
            pl.BlockSpec((n_pad, LANES), lambda i, k: (0, 0)),
            pl.BlockSpec((seq, LANES), lambda i, k: (0, 0)),
        ],
        out_specs=pl.BlockSpec((1, tq, NSA_Q_COLS), lambda i, k: (i, k, 0)),
        out_shape=jax.ShapeDtypeStruct((b, seq, NSA_Q_COLS), BF16),
        scratch_shapes=[pltpu.VMEM((groups, rows, LANES), F32), pltpu.VMEM((groups, rows, 2 * LANES), F32),
                        pltpu.VMEM((groups, rows, NSA_KC), F32), pltpu.VMEM((groups, rows, NSA_KC), F32),
                        pltpu.VMEM((groups, rows, 2 * LANES), BF16),
                        pltpu.VMEM((groups, NSA_GROUP, tq, LANES), F32), pltpu.VMEM((groups, NSA_GROUP, tq, LANES), F32)],
        compiler_params=_params(("parallel", "arbitrary")),
        name="nsa",
    )(qp, qr, gt, wb, wb, kvc, sw, ov, _block_onehot(seq, SLC_BLOCK))


def _moba_kernel(q_ref, kv_ref, km_ref, oh_ref, o_ref, m_sc, acc_sc, sa_sc, sb_sc):
    tq = MOBA_TQ
    heads = MOBA_HEADS_PER_STEP
    q0 = pl.multiple_of(pl.program_id(2) * tq, tq)
    lo = lax.broadcasted_iota(jnp.int32, (tq, LANES), 1) < HEAD_DIM
    n_blk = km_ref.shape[1]
    blk_shift = MOBA_BLOCK.bit_length() - 1
    blk = lax.broadcasted_iota(jnp.int32, (n_blk, tq), 0)
    qblk = jnp.right_shift(q0 + lax.broadcasted_iota(jnp.int32, (n_blk, tq), 1), blk_shift)
    past = blk < qblk

    def slab(h):
        return slice(h * LANES, (h + 1) * LANES)

    q_ext = []
    for h in range(heads):
        q = jnp.where(lo if h % 2 == 0 else ~lo, q_ref[0, :, slab(h // 2)], jnp.zeros((), BF16))
        gate_t = _dot_nt(km_ref[0, :, slab(h)].astype(BF16), q)
        top = _rank_below(jnp.where(past, gate_t, NEG), n_blk, MOBA_TOPK)
        bias = _block_bias(((top & past) | (blk == qblk)).astype(F32), tq)
        q_ext.append(jnp.concatenate([q, bias], axis=1))

    m_sc[...] = jnp.full(m_sc.shape, NEG, F32)
    acc_sc[...] = jnp.zeros(acc_sc.shape, F32)

    def keys(c):
        ks = pl.ds(pl.multiple_of(c * tq, tq), tq)
        return tuple(kv_ref[0, ks, slab(h)] for h in range(heads)), oh_ref[ks, :]

    causal = lax.broadcasted_iota(jnp.int32, (tq, tq), 1) <= lax.broadcasted_iota(jnp.int32, (tq, tq), 0)
    _put_scores(q_ext, keys, 0, sa_sc)
    _attend(q_ext, keys, pl.program_id(2), causal, m_sc, acc_sc, sa_sc, sb_sc)

    for pr in range(heads // 2):
        even = _softmax_finish(acc_sc[2 * pr])
        odd = _softmax_finish(acc_sc[2 * pr + 1])
        o_ref[0, :, slab(pr)] = jnp.where(lo, odd, even).astype(BF16)


def _moba(mq, mkv, km):
    b, seq = mq.shape[:2]
    tq = MOBA_TQ
    n_blk = seq // MOBA_BLOCK
    heads = MOBA_HEADS_PER_STEP
    return pl.pallas_call(
        _moba_kernel,
        grid=(b, MOBA_HEADS // heads, seq // tq),
        in_specs=[
            pl.BlockSpec((1, tq, heads // 2 * LANES), lambda i, j, k: (i, k, j)),
            pl.BlockSpec((1, seq, heads * LANES), lambda i, j, k: (i, 0, j)),
            pl.BlockSpec((1, n_blk, heads * LANES), lambda i, j, k: (i, 0, j)),
            pl.BlockSpec((seq, LANES), lambda i, j, k: (0, 0)),
        ],
        out_specs=pl.BlockSpec((1, tq, heads // 2 * LANES), lambda i, j, k: (i, k, j)),
        out_shape=jax.ShapeDtypeStruct((b, seq, MOBA_HEADS * HEAD_DIM), BF16),
        scratch_shapes=[pltpu.VMEM((heads, tq, LANES), F32), pltpu.VMEM((heads, tq, 2 * LANES), F32),
                        pltpu.VMEM((heads, tq, tq), F32), pltpu.VMEM((heads, tq, tq), F32)],
        compiler_params=_params(("parallel", "parallel", "arbitrary")),
        name="moba",
    )(mq, mkv, km, _block_onehot(seq, MOBA_BLOCK))


def _in_perm():
    d = HEAD_DIM
    perm = np.full((IN_COLS_P,), -1, np.int64)
    perm[SEC_Q:SEC_Q + NSA_Q_COLS] = np.arange(NSA_Q_COLS)
    kv0 = NSA_Q_COLS
    gate0 = kv0 + NSA_KV_COLS
    moba0 = gate0 + NSA_GATE_COLS
    per_group = NSA_GROUP * 3
    for g in range(NSA_KV_HEADS):
        perm[SEC_GATE + g * LANES:SEC_GATE + g * LANES + per_group] = gate0 + g * per_group + np.arange(per_group)

    def kv_cols(which, g):
        return kv0 + (which * NSA_KV_HEADS + g) * d + np.arange(d)

    for g in range(NSA_KV_HEADS):
        base = SEC_SW + g * 2 * LANES
        perm[base:base + d] = kv_cols(2, g)
        perm[base + d:base + 2 * d] = kv_cols(3, g)
        perm[base + 2 * d:base + 3 * d] = kv_cols(4, g)
        perm[base + 3 * d:base + 4 * d] = kv_cols(5, g)
        base = SEC_CMP + g * LANES
        perm[base:base + d] = kv_cols(0, g)
        perm[base + d:base + 2 * d] = kv_cols(1, g)
    perm[SEC_MQ:SEC_MQ + MOBA_HEADS * d] = moba0 + np.arange(MOBA_HEADS * d)
    for h in range(MOBA_HEADS):
        k_cols = moba0 + (MOBA_HEADS + h) * d + np.arange(d)
        v_cols = moba0 + (2 * MOBA_HEADS + h) * d + np.arange(d)
        base = SEC_MKV + h * LANES
        first, second = (k_cols, v_cols) if h % 2 == 0 else (v_cols, k_cols)
        perm[base:base + d] = first
        perm[base + d:base + 2 * d] = second
    return perm


def _take_runs(w, idx, axis):
    parts, i = [], 0
    while i < len(idx):
        j = i + 1
        if idx[i] < 0:
            while j < len(idx) and idx[j] < 0:
                j += 1
            shape = list(w.shape)
            shape[axis] = j - i
            parts.append(jnp.zeros(shape, w.dtype))
        else:
            while j < len(idx) and idx[j] == idx[j - 1] + 1:
                j += 1
            parts.append(lax.slice_in_dim(w, int(idx[i]), int(idx[j - 1]) + 1, axis=axis))
        i = j
    return jnp.concatenate(parts, axis=axis)


def _moba_out_rows():
    d = HEAD_DIM
    rows = []
    for pr in range(MOBA_HEADS // 2):
        rows.append(NSA_Q_COLS + (2 * pr + 1) * d + np.arange(d))
        rows.append(NSA_Q_COLS + (2 * pr) * d + np.arange(d))
    return np.concatenate(rows)


def _overlap_table(seq):
    n_pad = seq // CMP_STRIDE
    n = np.arange(n_pad)[:, None]
    j = np.arange(LANES)[None, :]
    starts = n * CMP_STRIDE
    sb = j * SLC_BLOCK
    ov = (starts < sb + SLC_BLOCK) & (starts + CMP_LEN > sb) & (n < n_pad - 1) & (j < seq // SLC_BLOCK)
    return ov.astype(np.float32)


def _block_onehot(seq, block):
    onehot = np.arange(seq)[:, None] // block == np.arange(LANES)[None, :]
    return jnp.asarray(onehot.astype(np.float32), BF16)


def _rope_lane_tables(seq):
    half = ROT_DIM // 2
    inv_freq = jnp.power(ROPE_THETA, -(jnp.arange(0, ROT_DIM, 2, dtype=F32) / ROT_DIM))
    ang = jnp.arange(seq, dtype=F32)[:, None] * inv_freq[None, :]
    cos, sin = jnp.cos(ang), jnp.sin(ang)
    rest = HEAD_DIM - ROT_DIM
    zeros_h = jnp.zeros((seq, half), F32)
    c = jnp.concatenate([cos, cos, jnp.ones((seq, rest), F32)], axis=1)
    sl = jnp.concatenate([-sin, zeros_h, jnp.zeros((seq, rest), F32)], axis=1)
    sr = jnp.concatenate([zeros_h, sin, jnp.zeros((seq, rest), F32)], axis=1)
    return tuple(jnp.tile(t, (1, LANES // HEAD_DIM)) for t in (c, sl, sr))


def _mixer(x2, b, seq, g_mix, w_p, layer, pos_ck, w_ck1, w_ck2, pos_cv, w_cv1, w_cv2, tabs, ov):
    qp, qr, gt, sw, cmp_in, mq, mkv, km = _inproj(x2, g_mix, w_p, layer, tabs, seq)

    def halves(w_k, w_v):
        wk = w_k.reshape(2, CMP_STRIDE, HEAD_DIM, CMP_HIDDEN)
        wv = w_v.reshape(2, CMP_STRIDE, HEAD_DIM, CMP_HIDDEN)
        z = jnp.zeros_like(wk)
        return jnp.concatenate([jnp.concatenate([wk, z], axis=3), jnp.concatenate([z, wv], axis=3)],
                               axis=2).astype(BF16)

    w1 = halves(w_ck1, w_cv1)
    pos = jnp.concatenate([pos_ck, pos_cv], axis=1).reshape(2, CMP_STRIDE, LANES)
    z2 = jnp.zeros_like(w_ck2)
    w2 = jnp.concatenate([jnp.concatenate([w_ck2, z2], axis=1),
                          jnp.concatenate([z2, w_cv2], axis=1)], axis=0).astype(BF16)
    kvc = _compress(cmp_in.reshape(b, seq, -1), pos, w1[0], w1[1], w2)

    o_nsa = _nsa(qp.reshape(b, seq, -1), qr.reshape(b, seq, -1), gt.reshape(b, seq, -1), kvc,
                 sw.reshape(b, seq, -1), ov)
    o_moba = _moba(mq.reshape(b, seq, -1), mkv.reshape(b, seq, -1), km.reshape(b, seq // MOBA_BLOCK, -1))

    n = b * seq
    return o_nsa.reshape(n, -1), o_moba.reshape(n, -1)


def kernel(x, norm_ffn1, w_ffn1_gate, w_ffn1_up, w_ffn1_down, norm_mix, w_in, pos_ck, w_ck1, w_ck2,
           pos_cv, w_cv1, w_cv2, w_out, norm_ffn2, w_ffn2_gate, w_ffn2_up, w_ffn2_down, norm_final):
    b, seq, d = x.shape
    assert d == D_MODEL and seq % PROJ_ROWS == 0 and SLC_TOPN <= seq // SLC_BLOCK <= LANES
    depth = norm_ffn1.shape[0]
    tabs = _rope_lane_tables(seq)
    ov = jnp.asarray(_overlap_table(seq), BF16)
    x2 = x.reshape(b * seq, d)
    w_p = _take_runs(w_in.astype(BF16), _in_perm(), axis=2)
    w_out_b = w_out.astype(BF16)
    wa = w_out_b[:, :NSA_Q_COLS]
    wm = _take_runs(w_out_b, _moba_out_rows(), axis=1)
    wg1, wu1, wd1, wg2, wu2, wd2 = (w.astype(BF16) for w in (w_ffn1_gate, w_ffn1_up, w_ffn1_down,
                                                            w_ffn2_gate, w_ffn2_up, w_ffn2_down))
    for l in range(depth):
        x2 = _ffn(x2, norm_ffn1[l], wg1, wu1, wd1, l)
        o_nsa, o_moba = _mixer(x2, b, seq, norm_mix[l], w_p, l, pos_ck[l], w_ck1[l], w_ck2[l],
                               pos_cv[l], w_cv1[l], w_cv2[l], tabs, ov)
        x2 = _ffn(x2, norm_ffn2[l], wg2, wu2, wd2, l, mix=(o_nsa, o_moba, wa[l], wm[l]),
                  g_final=norm_final if l == depth - 1 else None)
    return x2.reshape(b, seq, d)
```

```python
import functools

import numpy as np
import jax
import jax.numpy as jnp
from jax import lax
from jax.experimental import pallas as pl
from jax.experimental.pallas import tpu as pltpu

D_MODEL = 1024
HEAD_DIM = 64
NSA_HEADS = 8
NSA_KV_HEADS = 2
NSA_GROUP = NSA_HEADS // NSA_KV_HEADS
MOBA_HEADS = 8
ROT_DIM = HEAD_DIM // 4
ROPE_THETA = 500000.0
CMP_LEN = 32
CMP_STRIDE = 16
CMP_HIDDEN = 256
SLC_BLOCK = 64
SLC_TOPN = 16
WINDOW = 512
MOBA_BLOCK = 256
MOBA_TOPK = 3
D_FF = 2816
EPS = 1e-6
NEG = -1e30
FORCE_BONUS = 1e4
SCALE = HEAD_DIM ** -0.5

NSA_Q_COLS = NSA_HEADS * HEAD_DIM
NSA_KV_COLS = 3 * 2 * NSA_KV_HEADS * HEAD_DIM
NSA_GATE_COLS = NSA_HEADS * 3
MOBA_COLS = 3 * MOBA_HEADS * HEAD_DIM

LANES = 128
SUBLANES = 8
VMEM_LIMIT = 56 * 1024 * 1024

SEC_Q = 0
SEC_GATE = 512
SEC_SW = 768
SEC_CMP = 1280
SEC_MQ = 1536
SEC_MKV = 2048
IN_COLS_P = 3072

F32 = jnp.float32
BF16 = jnp.bfloat16

FFN_ROWS = 1024
FFN_CHUNK = 256
PROJ_ROWS = 512
NSA_TQ = 256
NSA_KC = 512
MOBA_HEADS_PER_STEP = 4
MOBA_TQ = 2 * MOBA_BLOCK
LOG2E = 1.4426950408889634
MASK_BIAS = 2.0 ** 100


def _dot(a, b):
    return jnp.dot(a, b, preferred_element_type=F32)


def _dot_nt(a, b):
    return lax.dot_general(a, b, (((1,), (1,)), ((), ())), preferred_element_type=F32)


def _rms(x, g):
    return x * lax.rsqrt(jnp.mean(x * x, axis=-1, keepdims=True) + EPS) * g


def _params(sem):
    return pltpu.CompilerParams(dimension_semantics=sem, vmem_limit_bytes=VMEM_LIMIT)


def _resident(shape, layer=None):
    nd = len(shape)
    if layer is None:
        return pl.BlockSpec(shape, lambda *_: (0,) * nd, pipeline_mode=pl.Buffered(1))
    return pl.BlockSpec((None,) + tuple(shape), lambda *_: (layer,) + (0,) * nd,
                        pipeline_mode=pl.Buffered(1))


def _ffn_kernel(*refs, mix, final_norm):
    refs = list(refs)
    x_ref = refs.pop(0)
    if mix:
        a_ref, mo_ref, wa_ref, wm_ref = (refs.pop(0) for _ in range(4))
    g_ref, wg_ref, wu_ref, wd_ref = (refs.pop(0) for _ in range(4))
    gf_ref = refs.pop(0) if final_norm else None
    o_ref, h_ref = refs
    x = x_ref[...]
    if mix:
        x = x + _dot(a_ref[...], wa_ref[...]) + _dot(mo_ref[...], wm_ref[...])
    xb = _rms(x, g_ref[...]).astype(BF16)
    for c in range(D_FF // FFN_CHUNK):
        sl = slice(c * FFN_CHUNK, (c + 1) * FFN_CHUNK)
        gate = _dot(xb, wg_ref[:, sl])
        up = _dot(xb, wu_ref[:, sl])
        h_ref[:, sl] = (jax.nn.silu(gate) * up).astype(BF16)
    y = x + 0.5 * _dot(h_ref[...], wd_ref[...])
    if final_norm:
        y = _rms(y, gf_ref[...])
    o_ref[...] = y


def _ffn(x2, g, wg, wu, wd, layer, mix=None, g_final=None):
    n = x2.shape[0]
    final_norm = g_final is not None

    def rows(width):
        return pl.BlockSpec((FFN_ROWS, width), lambda i: (i, 0))

    in_specs, args = [rows(D_MODEL)], [x2]
    if mix is not None:
        o_nsa, o_moba, wa, wm = mix
        in_specs += [rows(o_nsa.shape[1]), rows(o_moba.shape[1]), _resident(wa.shape), _resident(wm.shape)]
        args += [o_nsa, o_moba, wa, wm]
    in_specs += [_resident((1, D_MODEL)), _resident((D_MODEL, D_FF), layer),
                 _resident((D_MODEL, D_FF), layer), _resident((D_FF, D_MODEL), layer)]
    args += [g.reshape(1, D_MODEL), wg, wu, wd]
    if final_norm:
        in_specs.append(_resident((1, D_MODEL)))
        args.append(g_final.reshape(1, D_MODEL))
    return pl.pallas_call(
        functools.partial(_ffn_kernel, mix=mix is not None, final_norm=final_norm),
        grid=(n // FFN_ROWS,),
        in_specs=in_specs,
        out_specs=rows(D_MODEL),
        out_shape=jax.ShapeDtypeStruct((n, D_MODEL), F32),
        scratch_shapes=[pltpu.VMEM((FFN_ROWS, D_FF), BF16)],
        compiler_params=_params(("parallel",)),
        name="ffn" + ("_mix" if mix is not None else "") + ("_final" if final_norm else ""),
    )(*args)


def _rope_tables(c, sl, sr, mode):
    if mode == "both":
        return c, sl, sr
    lane = lax.broadcasted_iota(jnp.int32, c.shape, 1)
    keep = lane < HEAD_DIM if mode == "lo" else lane >= HEAD_DIM
    return jnp.where(keep, c, 1.0), jnp.where(keep, sl, 0.0), jnp.where(keep, sr, 0.0)


def _rope(v, tabs):
    c, sl, sr = tabs
    return v * c + pltpu.roll(v, LANES - ROT_DIM // 2, 1) * sl + pltpu.roll(v, ROT_DIM // 2, 1) * sr


def _inproj_kernel(x_ref, g_ref, w_ref, c_ref, sl_ref, sr_ref,
                   qp_ref, qr_ref, gt_ref, sw_ref, cmp_ref, mq_ref, mkv_ref, km_ref):
    xb = _rms(x_ref[...], g_ref[...]).astype(BF16)
    base = (c_ref[...], sl_ref[...], sr_ref[...])
    tabs = {m: _rope_tables(*base, m) for m in ("both", "lo", "hi")}

    qn = _dot(xb, w_ref[:, SEC_Q:SEC_GATE])
    lo = lax.broadcasted_iota(jnp.int32, (PROJ_ROWS, LANES), 1) < HEAD_DIM
    for pr in range(NSA_HEADS // 2):
        pair = qn[:, pr * LANES:(pr + 1) * LANES]
        for out_ref, v, scale in ((qp_ref, pair, SCALE), (qr_ref, _rope(pair, tabs["both"]), SCALE * LOG2E)):
            for e in range(2):
                head = v if e == 0 else pltpu.roll(v, HEAD_DIM, 1)
                sl = slice((2 * pr + e) * LANES, (2 * pr + e + 1) * LANES)
                out_ref[:, sl] = jnp.where(lo, head * scale, 0.0).astype(BF16)
    gt_ref[...] = jax.nn.sigmoid(_dot(xb, w_ref[:, SEC_GATE:SEC_SW]))
    cmp_ref[...] = _dot(xb, w_ref[:, SEC_CMP:SEC_MQ])

    sw = _dot(xb, w_ref[:, SEC_SW:SEC_CMP])
    for i in range(4):
        sl = slice(i * LANES, (i + 1) * LANES)
        sw_ref[:, sl] = _rope(sw[:, sl], tabs["lo"]).astype(BF16)

    mq = _dot(xb, w_ref[:, SEC_MQ:SEC_MKV])
    for i in range(4):
        sl = slice(i * LANES, (i + 1) * LANES)
        mq_ref[:, sl] = (_rope(mq[:, sl], tabs["both"]) * (SCALE * LOG2E)).astype(BF16)

    mkv = _dot(xb, w_ref[:, SEC_MKV:IN_COLS_P])
    nblk = PROJ_ROWS // MOBA_BLOCK
    for h in range(MOBA_HEADS):
        sl = slice(h * LANES, (h + 1) * LANES)
        r = _rope(mkv[:, sl], tabs["lo" if h % 2 == 0 else "hi"])
        mkv_ref[:, sl] = r.astype(BF16)
        km_ref[:, 0, sl] = jnp.mean(r.reshape(nblk, MOBA_BLOCK, LANES), axis=1)


def _inproj(x2, g, w_p, layer, tabs, seq):
    n = x2.shape[0]
    tiles_per_seq = seq // PROJ_ROWS
    nblk = PROJ_ROWS // MOBA_BLOCK

    def rows(width):
        return pl.BlockSpec((PROJ_ROWS, width), lambda i: (i, 0))

    tab = pl.BlockSpec((PROJ_ROWS, LANES), lambda i: (i % tiles_per_seq, 0))
    out_shapes = (
        jax.ShapeDtypeStruct((n, 1024), BF16),
        jax.ShapeDtypeStruct((n, 1024), BF16),
        jax.ShapeDtypeStruct((n, 256), F32),
        jax.ShapeDtypeStruct((n, 512), BF16),
        jax.ShapeDtypeStruct((n, 256), F32),
        jax.ShapeDtypeStruct((n, 512), BF16),
        jax.ShapeDtypeStruct((n, 1024), BF16),
        jax.ShapeDtypeStruct((n // MOBA_BLOCK, 1, 1024), F32),
    )
    out_specs = (rows(1024), rows(1024), rows(256), rows(512), rows(256), rows(512), rows(1024),
                 pl.BlockSpec((nblk, 1, 1024), lambda i: (i, 0, 0)))
    return pl.pallas_call(
        _inproj_kernel,
        grid=(n // PROJ_ROWS,),
        in_specs=[rows(D_MODEL), _resident((1, D_MODEL)), _resident((D_MODEL, IN_COLS_P), layer),
                  tab, tab, tab],
        out_specs=out_specs,
        out_shape=out_shapes,
        compiler_params=_params(("parallel",)),
        name="inproj",
    )(x2, g.reshape(1, D_MODEL), w_p, *tabs)


def _compress_kernel(x_ref, pos_ref, wa_ref, wb_ref, w2_ref, o_ref):
    n_pad = x_ref.shape[1] // CMP_STRIDE
    width = 2 * CMP_HIDDEN
    ya = jnp.zeros((n_pad, width), F32)
    zb = jnp.zeros((n_pad, width), F32)
    for l in range(CMP_STRIDE):
        h = x_ref[0, pl.ds(l, n_pad, stride=CMP_STRIDE), :]
        ya = ya + _dot((h + pos_ref[0, l:l + 1, :]).astype(BF16), wa_ref[l])
        zb = zb + _dot((h + pos_ref[1, l:l + 1, :]).astype(BF16), wb_ref[l])
    pre = ya + pltpu.roll(zb, n_pad - 1, 0)
    out = _dot(jax.nn.gelu(pre).astype(BF16), w2_ref[...])
    row = lax.broadcasted_iota(jnp.int32, out.shape, 0)
    o_ref[0, 0] = jnp.where(row < n_pad - 1, out, 0.0).astype(BF16)


def _compress(cmp_in, pos, wa, wb, w2):
    b, seq = cmp_in.shape[:2]
    g = NSA_KV_HEADS
    n_pad = seq // CMP_STRIDE
    return pl.pallas_call(
        _compress_kernel,
        grid=(b, g),
        in_specs=[pl.BlockSpec((1, seq, LANES), lambda i, j: (i, 0, j)),
                  _resident(pos.shape), _resident(wa.shape), _resident(wb.shape), _resident(w2.shape)],
        out_specs=pl.BlockSpec((1, 1, n_pad, LANES), lambda i, j: (i, j, 0, 0)),
        out_shape=jax.ShapeDtypeStruct((b, g, n_pad, LANES), BF16),
        compiler_params=_params(("parallel", "parallel")),
        name="compress",
    )(cmp_in, pos, wa, wb, w2)


def _rank_below(s_t, n_rows, limit):
    t = s_t.shape[1]
    n_grp = n_rows // SUBLANES
    grp = [s_t[g * SUBLANES:(g + 1) * SUBLANES] for g in range(n_grp)]
    cnt = [jnp.zeros((SUBLANES, t), F32) for _ in range(n_grp)]
    sub = lax.broadcasted_iota(jnp.int32, (SUBLANES, t), 0)
    for i in range(n_rows):
        ri = s_t[i:i + 1, :]
        for g in range(n_grp):
            if g > i // SUBLANES:
                beats = ri >= grp[g]
            elif g < i // SUBLANES:
                beats = ri > grp[g]
            else:
                beats = (ri > grp[g]) | ((ri == grp[g]) & (sub > i % SUBLANES))
            cnt[g] = cnt[g] + jnp.where(beats, 1.0, 0.0)
    return jnp.concatenate(cnt, axis=0) < limit


def _block_bias(sel_t, t):
    pad = jnp.zeros((LANES - sel_t.shape[0], t), F32)
    sel = jnp.concatenate([sel_t, pad], axis=0).T
    return ((sel - 1.0) * MASK_BIAS).astype(BF16)


def _with_ones(kv):
    return jnp.concatenate([kv, jnp.ones(kv.shape, kv.dtype)], axis=1)


def _softmax_step(read_s, kv, m_ref, acc_ref):
    m_old = m_ref[...]
    m_new = jnp.maximum(m_old, jnp.broadcast_to(jnp.max(read_s(), axis=-1, keepdims=True), m_old.shape))
    m_ref[...] = m_new
    alpha = jnp.exp2(m_old - m_new)
    s = read_s()
    acc = jnp.concatenate([alpha, alpha], axis=1) * acc_ref[...]
    half = s.shape[1] // 2
    for c0 in (0, half):
        p = jnp.exp2(s[:, c0:c0 + half] - jnp.concatenate([m_new] * (half // LANES), axis=1)).astype(BF16)
        acc = acc + _dot(p, _with_ones(kv[c0:c0 + half]))
    acc_ref[...] = acc


def _put_scores(q_ext, keys, c, s_ref):
    kvs, onehot = keys(c)
    for h, (q, kv) in enumerate(zip(q_ext, kvs)):
        s_ref[h] = _dot_nt(q, jnp.concatenate([kv, onehot], axis=1))


def _attend(q_ext, keys, n_past, causal, m_ref, acc_ref, s_a, s_b):
    put_scores = functools.partial(_put_scores, q_ext, keys)

    def softmax(c, s_ref, mask=None):
        for h, kv in enumerate(keys(c)[0]):
            if mask is None:
                read_s = functools.partial(lambda h: s_ref[h], h)
            else:
                read_s = functools.partial(lambda s: s, jnp.where(mask, s_ref[h], -MASK_BIAS))
            _softmax_step(read_s, kv, m_ref.at[h], acc_ref.at[h])

    def body(i, carry):
        c = 2 * i
        put_scores(c + 1, s_b)
        softmax(c, s_a)
        put_scores(c + 2, s_a)
        softmax(c + 1, s_b)
        return carry

    lax.fori_loop(0, n_past // 2, body, 0)

    @pl.when(n_past % 2 == 1)
    def _():
        put_scores(n_past, s_b)
        softmax(n_past - 1, s_a)
        softmax(n_past, s_b, causal)

    @pl.when(n_past % 2 == 0)
    def _():
        softmax(n_past, s_a, causal)


def _softmax_finish(acc):
    return acc[:, :LANES] / jnp.maximum(acc[:, LANES:], 1.0)


def _nsa_kernel(qp_ref, qr_ref, gt_ref, wb0_ref, wb1_ref, kvc_ref, sw_ref, ov_ref, oh_ref,
                o_ref, m_sc, acc_sc, sa_sc, sb_sc, qx_sc, part_sc, gslc_sc):
    wb_refs = (wb0_ref, wb1_ref)
    tq, r_heads, groups = NSA_TQ, NSA_GROUP, NSA_KV_HEADS
    rows = r_heads * tq
    q0 = pl.program_id(1) * tq
    lo = lax.broadcasted_iota(jnp.int32, (tq, LANES), 1) < HEAD_DIM
    n_slc = sw_ref.shape[1] // SLC_BLOCK
    blk_shift = SLC_BLOCK.bit_length() - 1

    def slab(i):
        return slice(i * LANES, (i + 1) * LANES)

    def head(o, r):
        return o[r * tq:(r + 1) * tq]

    def front(g, nb):
        def stack(ref):
            return jnp.concatenate([ref[0, :, slab(r_heads * g + r)] for r in range(r_heads)], axis=0)

        q_plain = stack(qp_ref)
        q_rot = stack(qr_ref)

        half = tq // 2
        o_win = []
        for u, wb_ref in enumerate(wb_refs):
            t0 = q0 + u * half
            w0 = pl.multiple_of(jnp.maximum(t0 - WINDOW, 0), half)
            kvw = sw_ref[0, pl.ds(w0, WINDOW + half), slab(2 * g + 1)]
            q_u = jnp.concatenate([qr_ref[0, u * half:(u + 1) * half, slab(r_heads * g + r)]
                                   for r in range(r_heads)], axis=0)
            s = _dot_nt(q_u, kvw).reshape(r_heads, half, WINDOW + half) + wb_ref[...][None]
            p = jnp.exp2(s - jnp.max(s, axis=-1, keepdims=True)).reshape(r_heads * half, WINDOW + half)
            o_win.append(_softmax_finish(_dot(p.astype(BF16), _with_ones(kvw))))

        def win_head(r):
            return jnp.concatenate([o[r * half:(r + 1) * half] for o in o_win], axis=0)

        n_pad = min(kvc_ref.shape[2], -(-nb * (SLC_BLOCK // CMP_STRIDE) // LANES) * LANES)
        kvc = kvc_ref[0, g, :n_pad, :]
        n_idx = lax.broadcasted_iota(jnp.int32, (tq, n_pad), 1)
        t_idx = q0 + lax.broadcasted_iota(jnp.int32, (tq, n_pad), 0)
        valid_c = (n_idx * CMP_STRIDE + CMP_LEN - 1 <= t_idx)[None]
        s = jnp.where(valid_c, _dot_nt(q_plain, kvc).reshape(r_heads, tq, n_pad), NEG)
        m = jnp.maximum(jnp.max(s, axis=-1, keepdims=True), NEG / 2)
        p = jnp.exp(s - m)
        p = p / jnp.maximum(jnp.sum(p, axis=-1, keepdims=True), 1.0)
        o_cmp = _dot(p.reshape(rows, n_pad).astype(BF16), kvc)

        p_sum = p[0] + p[1] + p[2] + p[3]
        ov = ov_ref[:n_pad, :]
        hi = p_sum.astype(BF16)
        r1 = p_sum - hi.astype(F32)
        mid = r1.astype(BF16)
        low = (r1 - mid.astype(F32)).astype(BF16)
        imp = _dot(hi, ov) + _dot(mid, ov) + _dot(low, ov)

        j_idx = lax.broadcasted_iota(jnp.int32, (tq, LANES), 1)
        t_row = q0 + lax.broadcasted_iota(jnp.int32, (tq, LANES), 0)
        qblk = jnp.right_shift(t_row, blk_shift)
        forced = (j_idx == 0) | (j_idx == qblk) | (j_idx == qblk - 1)
        visible = j_idx * SLC_BLOCK <= t_row
        score = jnp.where(visible, jnp.where(forced, FORCE_BONUS, imp), NEG)
        if nb <= SLC_TOPN:
            sel_t = jnp.ones((nb, tq), F32)
        else:
            sel_t = _rank_below(score.T[:nb], nb, SLC_TOPN).astype(F32)
        bias = _block_bias(sel_t, tq)

        gate = gt_ref[0, :, slab(g)]

        def gate_of(r, branch):
            return jnp.broadcast_to(gate[:, 3 * r + branch:3 * r + branch + 1], (tq, LANES))

        for r in range(r_heads):
            part_sc[g, r] = gate_of(r, 0) * head(o_cmp, r) + gate_of(r, 2) * win_head(r)
            gslc_sc[g, r] = gate_of(r, 1)
        q_ext = jnp.concatenate([q_rot, jnp.concatenate([bias] * r_heads, axis=0)], axis=1)
        qx_sc[g] = q_ext
        return q_ext

    def slc_keys(c):
        ks = pl.ds(pl.multiple_of(c * NSA_KC, NSA_KC), NSA_KC)
        return tuple(sw_ref[0, ks, slab(2 * g)] for g in range(groups)), oh_ref[ks, :]

    bucket = (q0 + tq - 1) // (SLC_TOPN * SLC_BLOCK)
    for b, nb in enumerate(range(SLC_TOPN, n_slc + 1, SLC_TOPN)):
        @pl.when(bucket == b)
        def _(nb=nb):
            q_ext = [front(g, nb) for g in range(groups)]
            m_sc[...] = jnp.full(m_sc.shape, NEG, F32)
            acc_sc[...] = jnp.zeros(acc_sc.shape, F32)
            _put_scores(q_ext, slc_keys, 0, sa_sc)

    c_last = q0 // NSA_KC
    kpos = c_last * NSA_KC + lax.broadcasted_iota(jnp.int32, (rows, NSA_KC), 1)
    tpos = q0 + (lax.broadcasted_iota(jnp.int32, (rows, NSA_KC), 0) & (tq - 1))
    _attend([qx_sc[g] for g in range(groups)], slc_keys, c_last, kpos <= tpos, m_sc, acc_sc, sa_sc, sb_sc)

    for g in range(groups):
        o_slc = _softmax_finish(acc_sc[g])
        outs = [part_sc[g, r] + gslc_sc[g, r] * head(o_slc, r) for r in range(r_heads)]
        for pr in range(r_heads // 2):
            pair = jnp.where(lo, pltpu.roll(outs[2 * pr], HEAD_DIM, 1), outs[2 * pr + 1])
            o_ref[0, :, slab(2 * g + pr)] = pair.astype(BF16)


def _window_bias(seq, half):
    offsets = sorted({t0 - max(t0 - WINDOW, 0) for t0 in range(0, seq, half)})
    i = np.arange(half)[:, None]
    c = np.arange(WINDOW + half)[None, :]
    tabs = []
    for off in offsets:
        d = off + i - c
        tabs.append(np.where((d >= 0) & (d < WINDOW), 0.0, NEG))
    return np.stack(tabs).astype(np.float32)


def _nsa(qp, qr, gt, kvc, sw, ov):
    b, seq = qp.shape[:2]
    groups = NSA_KV_HEADS
    tq = NSA_TQ
    rows = NSA_GROUP * tq
    n_pad = kvc.shape[2]
    half = tq // 2
    wb = jnp.asarray(_window_bias(seq, half))
    last = wb.shape[0] - 1

    def band(u):
        return pl.BlockSpec((None, half, WINDOW + half), lambda i, k: (jnp.minimum(2 * k + u, last), 0, 0))

    return pl.pallas_call(
        _nsa_kernel,
        grid=(b, seq // tq),
        in_specs=[
            pl.BlockSpec((1, tq, NSA_HEADS * LANES), lambda i, k: (i, k, 0)),
            pl.BlockSpec((1, tq, NSA_HEADS * LANES), lambda i, k: (i, k, 0)),
            pl.BlockSpec((1, tq, groups * LANES), lambda i, k: (i, k, 0)),
            band(0), band(1),
            pl.BlockSpec((1, groups, n_pad, LANES), lambda i, k: (i, 0, 0, 0)),
            pl.BlockSpec((1, seq, 2 * groups * LANES), lambda i, k: (i, 0, 0)),
            pl.BlockSpec((n_pad, LANES), lambda i, k: (0, 0)),
            pl.BlockSpec((seq, LANES), lambda i, k: (0, 0)),
        ],
        out_specs=pl.BlockSpec((1, tq, NSA_Q_COLS), lambda i, k: (i, k, 0)),
        out_shape=jax.ShapeDtypeStruct((b, seq, NSA_Q_COLS), BF16),
        scratch_shapes=[pltpu.VMEM((groups, rows, LANES), F32), pltpu.VMEM((groups, rows, 2 * LANES), F32),
                        pltpu.VMEM((groups, rows, NSA_KC), F32), pltpu.VMEM((groups, rows, NSA_KC), F32),
                        pltpu.VMEM((groups, rows, 2 * LANES), BF16),
                        pltpu.VMEM((groups, NSA_GROUP, tq, LANES), F32), pltpu.VMEM((groups, NSA_GROUP, tq, LANES), F32)],
        compiler_params=_params(("parallel", "arbitrary")),
        name="nsa",
    )(qp, qr, gt, wb, wb, kvc, sw, ov, _block_onehot(seq, SLC_BLOCK))


def _moba_kernel(q_ref, kv_ref, km_ref, oh_ref, o_ref, m_sc, acc_sc, sa_sc, sb_sc):
    tq = MOBA_TQ
    heads = MOBA_HEADS_PER_STEP
    q0 = pl.multiple_of(pl.program_id(2) * tq, tq)
    lo = lax.broadcasted_iota(jnp.int32, (tq, LANES), 1) < HEAD_DIM
    n_blk = km_ref.shape[1]
    blk_shift = MOBA_BLOCK.bit_length() - 1
    blk = lax.broadcasted_iota(jnp.int32, (n_blk, tq), 0)
    qblk = jnp.right_shift(q0 + lax.broadcasted_iota(jnp.int32, (n_blk, tq), 1), blk_shift)
    past = blk < qblk

    def slab(h):
        return slice(h * LANES, (h + 1) * LANES)

    q_ext = []
    for h in range(heads):
        q = jnp.where(lo if h % 2 == 0 else ~lo, q_ref[0, :, slab(h // 2)], jnp.zeros((), BF16))
        gate_t = _dot_nt(km_ref[0, :, slab(h)].astype(BF16), q)
        top = _rank_below(jnp.where(past, gate_t, NEG), n_blk, MOBA_TOPK)
        bias = _block_bias(((top & past) | (blk == qblk)).astype(F32), tq)
        q_ext.append(jnp.concatenate([q, bias], axis=1))

    m_sc[...] = jnp.full(m_sc.shape, NEG, F32)
    acc_sc[...] = jnp.zeros(acc_sc.shape, F32)

    def keys(c):
        ks = pl.ds(pl.multiple_of(c * tq, tq), tq)
        return tuple(kv_ref[0, ks, slab(h)] for h in range(heads)), oh_ref[ks, :]

    causal = lax.broadcasted_iota(jnp.int32, (tq, tq), 1) <= lax.broadcasted_iota(jnp.int32, (tq, tq), 0)
    _put_scores(q_ext, keys, 0, sa_sc)
    _attend(q_ext, keys, pl.program_id(2), causal, m_sc, acc_sc, sa_sc, sb_sc)

    for pr in range(heads // 2):
        even = _softmax_finish(acc_sc[2 * pr])
        odd = _softmax_finish(acc_sc[2 * pr + 1])
        o_ref[0, :, slab(pr)] = jnp.where(lo, odd, even).astype(BF16)


def _moba(mq, mkv, km):
    b, seq = mq.shape[:2]
    tq = MOBA_TQ
    n_blk = seq // MOBA_BLOCK
    heads = MOBA_HEADS_PER_STEP
    return pl.pallas_call(
        _moba_kernel,
        grid=(b, MOBA_HEADS // heads, seq // tq),
        in_specs=[
            pl.BlockSpec((1, tq, heads // 2 * LANES), lambda i, j, k: (i, k, j)),
            pl.BlockSpec((1, seq, heads * LANES), lambda i, j, k: (i, 0, j)),
            pl.BlockSpec((1, n_blk, heads * LANES), lambda i, j, k: (i, 0, j)),
            pl.BlockSpec((seq, LANES), lambda i, j, k: (0, 0)),
        ],
        out_specs=pl.BlockSpec((1, tq, heads // 2 * LANES), lambda i, j, k: (i, k, j)),
        out_shape=jax.ShapeDtypeStruct((b, seq, MOBA_HEADS * HEAD_DIM), BF16),
        scratch_shapes=[pltpu.VMEM((heads, tq, LANES), F32), pltpu.VMEM((heads, tq, 2 * LANES), F32),
                        pltpu.VMEM((heads, tq, tq), F32), pltpu.VMEM((heads, tq, tq), F32)],
        compiler_params=_params(("parallel", "parallel", "arbitrary")),
        name="moba",
    )(mq, mkv, km, _block_onehot(seq, MOBA_BLOCK))


def _in_perm():
    d = HEAD_DIM
    perm = np.full((IN_COLS_P,), -1, np.int64)
    perm[SEC_Q:SEC_Q + NSA_Q_COLS] = np.arange(NSA_Q_COLS)
    kv0 = NSA_Q_COLS
    gate0 = kv0 + NSA_KV_COLS
    moba0 = gate0 + NSA_GATE_COLS
    per_group = NSA_GROUP * 3
    for g in range(NSA_KV_HEADS):
        perm[SEC_GATE + g * LANES:SEC_GATE + g * LANES + per_group] = gate0 + g * per_group + np.arange(per_group)

    def kv_cols(which, g):
        return kv0 + (which * NSA_KV_HEADS + g) * d + np.arange(d)

    for g in range(NSA_KV_HEADS):
        base = SEC_SW + g * 2 * LANES
        perm[base:base + d] = kv_cols(2, g)
        perm[base + d:base + 2 * d] = kv_cols(3, g)
        perm[base + 2 * d:base + 3 * d] = kv_cols(4, g)
        perm[base + 3 * d:base + 4 * d] = kv_cols(5, g)
        base = SEC_CMP + g * LANES
        perm[base:base + d] = kv_cols(0, g)
        perm[base + d:base + 2 * d] = kv_cols(1, g)
    perm[SEC_MQ:SEC_MQ + MOBA_HEADS * d] = moba0 + np.arange(MOBA_HEADS * d)
    for h in range(MOBA_HEADS):
        k_cols = moba0 + (MOBA_HEADS + h) * d + np.arange(d)
        v_cols = moba0 + (2 * MOBA_HEADS + h) * d + np.arange(d)
        base = SEC_MKV + h * LANES
        first, second = (k_cols, v_cols) if h % 2 == 0 else (v_cols, k_cols)
        perm[base:base + d] = first
        perm[base + d:base + 2 * d] = second
    return perm


def _take_runs(w, idx, axis):
    parts, i = [], 0
    while i < len(idx):
        j = i + 1
        if idx[i] < 0:
            while j < len(idx) and idx[j] < 0:
                j += 1
            shape = list(w.shape)
            shape[axis] = j - i
            parts.append(jnp.zeros(shape, w.dtype))
        else:
            while j < len(idx) and idx[j] == idx[j - 1] + 1:
                j += 1
            parts.append(lax.slice_in_dim(w, int(idx[i]), int(idx[j - 1]) + 1, axis=axis))
        i = j
    return jnp.concatenate(parts, axis=axis)


def _moba_out_rows():
    d = HEAD_DIM
    rows = []
    for pr in range(MOBA_HEADS // 2):
        rows.append(NSA_Q_COLS + (2 * pr + 1) * d + np.arange(d))
        rows.append(NSA_Q_COLS + (2 * pr) * d + np.arange(d))
    return np.concatenate(rows)


def _overlap_table(seq):
    n_pad = seq // CMP_STRIDE
    n = np.arange(n_pad)[:, None]
    j = np.arange(LANES)[None, :]
    starts = n * CMP_STRIDE
    sb = j * SLC_BLOCK
    ov = (starts < sb + SLC_BLOCK) & (starts + CMP_LEN > sb) & (n < n_pad - 1) & (j < seq // SLC_BLOCK)
    return ov.astype(np.float32)


def _block_onehot(seq, block):
    onehot = np.arange(seq)[:, None] // block == np.arange(LANES)[None, :]
    return jnp.asarray(onehot.astype(np.float32), BF16)


def _rope_lane_tables(seq):
    half = ROT_DIM // 2
    inv_freq = jnp.power(ROPE_THETA, -(jnp.arange(0, ROT_DIM, 2, dtype=F32) / ROT_DIM))
    ang = jnp.arange(seq, dtype=F32)[:, None] * inv_freq[None, :]
    cos, sin = jnp.cos(ang), jnp.sin(ang)
    rest = HEAD_DIM - ROT_DIM
    zeros_h = jnp.zeros((seq, half), F32)
    c = jnp.concatenate([cos, cos, jnp.ones((seq, rest), F32)], axis=1)
    sl = jnp.concatenate([-sin, zeros_h, jnp.zeros((seq, rest), F32)], axis=1)
    sr = jnp.concatenate([zeros_h, sin, jnp.zeros((seq, rest), F32)], axis=1)
    return tuple(jnp.tile(t, (1, LANES // HEAD_DIM)) for t in (c, sl, sr))


def _mixer(x2, b, seq, g_mix, w_p, layer, pos_ck, w_ck1, w_ck2, pos_cv, w_cv1, w_cv2, tabs, ov):
    qp, qr, gt, sw, cmp_in, mq, mkv, km = _inproj(x2, g_mix, w_p, layer, tabs, seq)

    def halves(w_k, w_v):
        wk = w_k.reshape(2, CMP_STRIDE, HEAD_DIM, CMP_HIDDEN)
        wv = w_v.reshape(2, CMP_STRIDE, HEAD_DIM, CMP_HIDDEN)
        z = jnp.zeros_like(wk)
        return jnp.concatenate([jnp.concatenate([wk, z], axis=3), jnp.concatenate([z, wv], axis=3)],
                               axis=2).astype(BF16)

    w1 = halves(w_ck1, w_cv1)
    pos = jnp.concatenate([pos_ck, pos_cv], axis=1).reshape(2, CMP_STRIDE, LANES)
    z2 = jnp.zeros_like(w_ck2)
    w2 = jnp.concatenate([jnp.concatenate([w_ck2, z2], axis=1),
                          jnp.concatenate([z2, w_cv2], axis=1)], axis=0).astype(BF16)
    kvc = _compress(cmp_in.reshape(b, seq, -1), pos, w1[0], w1[1], w2)

    o_nsa = _nsa(qp.reshape(b, seq, -1), qr.reshape(b, seq, -1), gt.reshape(b, seq, -1), kvc,
                 sw.reshape(b, seq, -1), ov)
    o_moba = _moba(mq.reshape(b, seq, -1), mkv.reshape(b, seq, -1), km.reshape(b, seq // MOBA_BLOCK, -1))

    n = b * seq
    return o_nsa.reshape(n, -1), o_moba.reshape(n, -1)


def kernel(x, norm_ffn1, w_ffn1_gate, w_ffn1_up, w_ffn1_down, norm_mix, w_in, pos_ck, w_ck1, w_ck2,
           pos_cv, w_cv1, w_cv2, w_out, norm_ffn2, w_ffn2_gate, w_ffn2_up, w_ffn2_down, norm_final):
    b, seq, d = x.shape
    assert d == D_MODEL and seq % PROJ_ROWS == 0 and SLC_TOPN <= seq // SLC_BLOCK <= LANES
    depth = norm_ffn1.shape[0]
    tabs = _rope_lane_tables(seq)
    ov = jnp.asarray(_overlap_table(seq), BF16)
    x2 = x.reshape(b * seq, d)
    w_p = _take_runs(w_in.astype(BF16), _in_perm(), axis=2)
    w_out_b = w_out.astype(BF16)
    wa = w_out_b[:, :NSA_Q_COLS]
    wm = _take_runs(w_out_b, _moba_out_rows(), axis=1)
    wg1, wu1, wd1, wg2, wu2, wd2 = (w.astype(BF16) for w in (w_ffn1_gate, w_ffn1_up, w_ffn1_down,
                                                            w_ffn2_gate, w_ffn2_up, w_ffn2_down))
    for l in range(depth):
        x2 = _ffn(x2, norm_ffn1[l], wg1, wu1, wd1, l)
        o_nsa, o_moba = _mixer(x2, b, seq, norm_mix[l], w_p, l, pos_ck[l], w_ck1[l], w_ck2[l],
                               pos_cv[l], w_cv1[l], w_cv2[l], tabs, ov)
        x2 = _ffn(x2, norm_ffn2[l], wg2, wu2, wd2, l, mix=(o_nsa, o_moba, wa[l], wm[l]),
                  g_final=norm_final if l == depth - 1 else None)
    return x2.reshape(b, seq, d)
```
